```python
import math
import jax, jax.numpy as jnp
from jax import lax
import numpy as np

D_MODEL = 1024
BATCH = 32
SEQ = 2048
DEPTH = 4

N_A = DEPTH // 2
N_B = DEPTH - N_A
HG_HEAD_DIM = 128
HG_HEADS = D_MODEL // HG_HEAD_DIM
HG_CHUNK = 32
MLA_HEADS = 16
Q_RANK = 256
KV_RANK = 128
NOPE_DIM = 128
ROPE_DIM = 64
V_DIM = 128
ROPE_THETA = 10000.0
Q_BLOCK = 128
N_EXPERTS = 32
TOP_K = 4
D_EXPERT = D_MODEL
SWIGLU_ALPHA = 1.702
SWIGLU_LIMIT = 7.0
EXPERT_BLOCK = 256
DN_ALPHA = (2.0 * DEPTH) ** 0.25
DN_BETA = (8.0 * DEPTH) ** -0.25
LN_EPS = 1e-5
RMS_EPS = 1e-6

kernel_name = 'yoco_hgrn2_mla_moe_deepnorm'

F32 = jnp.float32


def layer_norm(x, g, b):
    xf = x.astype(F32)
    mu = jnp.mean(xf, axis=-1, keepdims=True)
    var = jnp.mean(jnp.square(xf - mu), axis=-1, keepdims=True)
    y = (xf - mu) * lax.rsqrt(var + LN_EPS) * g.astype(F32) + b.astype(F32)
    return y.astype(x.dtype)


def rms_norm(x, g):
    xf = x.astype(F32)
    y = xf * lax.rsqrt(jnp.mean(jnp.square(xf), axis=-1, keepdims=True) + RMS_EPS) * g.astype(F32)
    return y.astype(x.dtype)


def rope_tables(positions):
    inv_freq = ROPE_THETA ** (-jnp.arange(0, ROPE_DIM, 2, dtype=F32) / ROPE_DIM)
    ang = positions.astype(F32)[..., None] * inv_freq
    return jnp.cos(ang), jnp.sin(ang)


def apply_rope(t, cos, sin):
    t1, t2 = jnp.split(t.astype(F32), 2, axis=-1)
    return jnp.concatenate([t1 * cos - t2 * sin, t1 * sin + t2 * cos], axis=-1).astype(t.dtype)


def hgrn2_mixer(x, w_in, g_norm, w_o, lb):
    bsz, s_len, d = x.shape
    H, Dh, C = HG_HEADS, HG_HEAD_DIM, HG_CHUNK
    n_chunks = s_len // C
    q, f, i, g = jnp.split(x @ w_in, 4, axis=-1)
    q = jax.nn.silu(q.astype(F32))
    forget = lb + (1.0 - lb) * jax.nn.sigmoid(f.astype(F32))
    k = 1.0 - forget
    log_f = jnp.log(forget)

    def to_chunks(t):
        return t.astype(F32).reshape(bsz, n_chunks, C, H, Dh).transpose(1, 0, 3, 2, 4)

    causal = jnp.tril(jnp.ones((C, C), dtype=bool))

    def chunk_step(state, inp):
        qc, kc, ic, lc = inp
        b = jnp.cumsum(lc, axis=2)
        o_inter = jnp.einsum('bhtk,bhkv->bhtv', qc * jnp.exp(b), state)
        diff = b[:, :, :, None, :] - b[:, :, None, :, :]
        decay = jnp.exp(jnp.where(causal[:, :, None], diff, -jnp.inf))
        scores = jnp.einsum('bhtk,bhsk,bhtsk->bhts', qc, kc, decay)
        o_intra = jnp.einsum('bhts,bhsv->bhtv', scores, ic)
        b_last = b[:, :, -1:, :]
        state = jnp.exp(b_last[:, :, 0, :])[..., None] * state + jnp.einsum(
            'bhsk,bhsv->bhkv', kc * jnp.exp(b_last - b), ic)
        return state, o_inter + o_intra

    state0 = jnp.zeros((bsz, H, Dh, Dh), F32)
    _, o = lax.scan(chunk_step, state0, (to_chunks(q), to_chunks(k), to_chunks(i), to_chunks(log_f)))
    o = o.transpose(1, 0, 3, 2, 4).reshape(bsz, s_len, H, Dh)
    o = rms_norm(o, g_norm) * jax.nn.silu(g.astype(F32).reshape(bsz, s_len, H, Dh))
    return o.reshape(bsz, s_len, d).astype(x.dtype) @ w_o


def mla_shared_kv(h, kv_w_a, kv_norm, kv_w_b, cos, sin):
    bsz, s_len, _ = h.shape
    c_kv, k_r = jnp.split(h @ kv_w_a, [KV_RANK], axis=-1)
    c_kv = rms_norm(c_kv, kv_norm)
    kv = (c_kv @ kv_w_b).reshape(bsz, s_len, MLA_HEADS, NOPE_DIM + V_DIM)
    k_nope, v = jnp.split(kv, [NOPE_DIM], axis=-1)
    k_rope = apply_rope(k_r, cos, sin)
    return k_nope, k_rope, v


def mla_mixer(x, w_dq, q_norm, w_uq, w_o, k_nope, k_rope, v, cos, sin):
    bsz, s_len, _ = x.shape
    c_q = rms_norm(x @ w_dq, q_norm)
    q = (c_q @ w_uq).reshape(bsz, s_len, MLA_HEADS, NOPE_DIM + ROPE_DIM)
    q_nope, q_rope = jnp.split(q, [NOPE_DIM], axis=-1)
    q_rope = apply_rope(q_rope, cos[:, :, None, :], sin[:, :, None, :])
    scale = (NOPE_DIM + ROPE_DIM) ** -0.5
    outs = []
    for blk in range(s_len // Q_BLOCK):
        lo, hi = blk * Q_BLOCK, (blk + 1) * Q_BLOCK
        s = (jnp.einsum('bqhd,bkhd->bhqk', q_nope[:, lo:hi], k_nope[:, :hi])
             + jnp.einsum('bqhd,bkd->bhqk', q_rope[:, lo:hi], k_rope[:, :hi]))
        s = s.astype(F32) * scale
        mask = jnp.arange(hi)[None, :] <= jnp.arange(lo, hi)[:, None]
        p = jax.nn.softmax(jnp.where(mask, s, -jnp.inf), axis=-1).astype(v.dtype)
        outs.append(jnp.einsum('bhqk,bkhd->bqhd', p, v[:, :hi]))
    o = jnp.concatenate(outs, axis=1).reshape(bsz, s_len, MLA_HEADS * V_DIM)
    return o @ w_o


def moe_ffn(x, router_w, router_b, w1, b1, w2, b2):
    bsz, s_len, d = x.shape
    n_tok = bsz * s_len
    n_assign = n_tok * TOP_K
    x2 = x.reshape(n_tok, d)
    logits = (x2 @ router_w + router_b).astype(F32)
    top_v, top_e = lax.top_k(logits, TOP_K)
    gates = jax.nn.softmax(top_v, axis=-1)
    flat_e = top_e.reshape(n_assign)
    flat_tok = jnp.arange(n_assign, dtype=jnp.int32) // TOP_K
    flat_g = gates.reshape(n_assign)
    order = jnp.argsort(flat_e, stable=True)
    se, stok, sg = flat_e[order], flat_tok[order], flat_g[order]
    counts = jnp.bincount(flat_e, length=N_EXPERTS)
    starts = jnp.cumsum(counts) - counts
    padded = (counts + EXPERT_BLOCK - 1) // EXPERT_BLOCK * EXPERT_BLOCK
    pends = jnp.cumsum(padded)
    pstarts = pends - padded
    dest = pstarts[se] + jnp.arange(n_assign, dtype=jnp.int32) - starts[se]
    n_blocks = -(-n_assign // EXPERT_BLOCK) + N_EXPERTS
    n_rows = n_blocks * EXPERT_BLOCK
    row_tok = jnp.full((n_rows,), n_tok, jnp.int32).at[dest].set(stok)
    row_gate = jnp.zeros((n_rows,), F32).at[dest].set(sg)
    block_e = jnp.minimum(jnp.searchsorted(pends, jnp.arange(n_blocks, dtype=jnp.int32) * EXPERT_BLOCK,
                                           side='right'), N_EXPERTS - 1)

    def expert_block(args):
        tok, g, e = args
        xb = x2[jnp.minimum(tok, n_tok - 1)]
        h = xb @ w1[e] + b1[e]
        x_glu = jnp.minimum(h[:, ::2], SWIGLU_LIMIT)
        x_lin = jnp.clip(h[:, 1::2], -SWIGLU_LIMIT, SWIGLU_LIMIT)
        a = x_glu * jax.nn.sigmoid(SWIGLU_ALPHA * x_glu) * (x_lin + 1.0)
        y = a @ w2[e] + b2[e]
        return y * g[:, None].astype(y.dtype)

    y_rows = lax.map(expert_block, (row_tok.reshape(n_blocks, EXPERT_BLOCK),
                                    row_gate.reshape(n_blocks, EXPERT_BLOCK), block_e))
    out = jnp.zeros((n_tok, d), x.dtype).at[row_tok].add(
        y_rows.reshape(n_rows, d).astype(x.dtype), mode='drop')
    return out.reshape(bsz, s_len, d)


def setup_inputs(seed: int = 0) -> dict:
    key = jax.random.key(seed)
    ks = jax.random.split(key, 24)
    D, E, F = D_MODEL, N_EXPERTS, D_EXPERT
    nrm = lambda k, shape, scale: jax.random.normal(k, shape, F32) * scale
    return {
        'x': jax.random.normal(ks[0], (BATCH, SEQ, D), F32),
        'positions': jnp.broadcast_to(jnp.arange(SEQ, dtype=jnp.int32)[None, :], (BATCH, SEQ)),
        'ln_g': 1.0 + nrm(ks[1], (DEPTH, 2, D), 0.02),
        'ln_b': nrm(ks[2], (DEPTH, 2, D), 0.02),
        'hg_w_in': nrm(ks[3], (N_A, D, 4 * D), D ** -0.5),
        'hg_lb': nrm(ks[4], (N_A, D), 0.5),
        'hg_gnorm': 1.0 + nrm(ks[5], (N_A, HG_HEAD_DIM), 0.02),
        'hg_w_o': nrm(ks[6], (N_A, D, D), D ** -0.5 * DN_BETA),
        'mla_w_dq': nrm(ks[7], (N_B, D, Q_RANK), D ** -0.5),
        'mla_q_norm': 1.0 + nrm(ks[8], (N_B, Q_RANK), 0.02),
        'mla_w_uq': nrm(ks[9], (N_B, Q_RANK, MLA_HEADS * (NOPE_DIM + ROPE_DIM)), Q_RANK ** -0.5),
        'mla_w_o': nrm(ks[10], (N_B, MLA_HEADS * V_DIM, D), (MLA_HEADS * V_DIM) ** -0.5 * DN_BETA),
        'kv_w_a': nrm(ks[11], (D, KV_RANK + ROPE_DIM), D ** -0.5),
        'kv_norm': 1.0 + nrm(ks[12], (KV_RANK,), 0.02),
        'kv_w_b': nrm(ks[13], (KV_RANK, MLA_HEADS * (NOPE_DIM + V_DIM)), KV_RANK ** -0.5),
        'router_w': nrm(ks[14], (DEPTH, D, E), D ** -0.5),
        'router_b': nrm(ks[15], (DEPTH, E), 0.01),
        'moe_w1': nrm(ks[16], (DEPTH, E, D, 2 * F), D ** -0.5),
        'moe_b1': nrm(ks[17], (DEPTH, E, 2 * F), 0.01),
        'moe_w2': nrm(ks[18], (DEPTH, E, F, D), F ** -0.5 * DN_BETA),
        'moe_b2': nrm(ks[19], (DEPTH, E, D), 0.01),
    }


def reference(x, positions, ln_g, ln_b, hg_w_in, hg_lb, hg_gnorm, hg_w_o, mla_w_dq, mla_q_norm,
              mla_w_uq, mla_w_o, kv_w_a, kv_norm, kv_w_b, router_w, router_b, moe_w1, moe_b1,
              moe_w2, moe_b2):
    cos, sin = rope_tables(positions)
    lb_soft = jax.nn.softmax(hg_lb.astype(F32), axis=0)
    lower_bounds = jnp.cumsum(lb_soft, axis=0) - lb_soft[0]
    for layer in range(DEPTH):
        if layer < N_A:
            mix = hgrn2_mixer(x, hg_w_in[layer], hg_gnorm[layer], hg_w_o[layer], lower_bounds[layer])
        else:
            if layer == N_A:
                k_nope, k_rope, v = mla_shared_kv(x, kv_w_a, kv_norm, kv_w_b, cos, sin)
            j = layer - N_A
            mix = mla_mixer(x, mla_w_dq[j], mla_q_norm[j], mla_w_uq[j], mla_w_o[j],
                            k_nope, k_rope, v, cos, sin)
        x = layer_norm(DN_ALPHA * x + mix, ln_g[layer, 0], ln_b[layer, 0])
        ffn = moe_ffn(x, router_w[layer], router_b[layer], moe_w1[layer], moe_b1[layer],
                      moe_w2[layer], moe_b2[layer])
        x = layer_norm(DN_ALPHA * x + ffn, ln_g[layer, 1], ln_b[layer, 1])
    return x
```

```python
import functools

import jax
import jax.numpy as jnp
from jax import lax
from jax.experimental import pallas as pl
from jax.experimental.pallas import tpu as pltpu

F32 = jnp.float32
BF16 = jnp.bfloat16

HG_HEAD_DIM = 128
NOPE_DIM = 128
ROPE_DIM = 64
V_DIM = 128
KV_RANK = 128
ROPE_THETA = 10000.0
TOP_K = 4
SWIGLU_ALPHA = 1.702
SWIGLU_LIMIT = 7.0
LN_EPS = 1e-5
RMS_EPS = 1e-6

LANES = 128
QK_DIM = 256
VMEM_LIMIT = 56 * 1024 * 1024

ROW_TILE = 256
HG_GROUP = 128
ATT_TQ = 512
MOE_BLOCK = 512
DMA_TILE = 256


def _cparams(*sem):
    return pltpu.CompilerParams(dimension_semantics=sem, vmem_limit_bytes=VMEM_LIMIT)


def _layer_norm(y, g, b):
    mu = jnp.mean(y, axis=-1, keepdims=True)
    d = y - mu
    var = jnp.mean(d * d, axis=-1, keepdims=True)
    return d * lax.rsqrt(var + LN_EPS) * g + b


def _dot(a, b):
    return jnp.dot(a, b, preferred_element_type=F32)


def _dot_nt(a, b):
    return lax.dot_general(a, b, (((1,), (1,)), ((), ())), preferred_element_type=F32)


def _dot_tn(a, b):
    return lax.dot_general(a, b, (((0,), (0,)), ((), ())), preferred_element_type=F32)


def _sigmoid(x):
    return 1.0 / (1.0 + jnp.exp(-x))


def _hgrn_kernel(x_ref, w_in_ref, lb_ref, gn_ref, w_o_ref, lg_ref, lbias_ref, out_ref,
                 q_s, f_s, i_s, g_s, mix_s, st_s, *, dn_alpha):
    ts, d = x_ref.shape[1], x_ref.shape[2]
    n_heads = d // HG_HEAD_DIM
    grp = HG_GROUP

    @pl.when(pl.program_id(1) == 0)
    def _():
        st_s[...] = jnp.zeros_like(st_s)

    x = x_ref[0]
    xb = x.astype(BF16)
    lb = lb_ref[...]

    for sec, dst in enumerate((q_s, f_s, i_s, g_s)):
        p = _dot(xb, w_in_ref[:, sec * d:(sec + 1) * d])
        if sec == 0 or sec == 3:
            p = p * _sigmoid(p)
        elif sec == 1:
            p = lb + (1.0 - lb) * _sigmoid(p)
        for h in range(n_heads):
            dst[h] = p[:, h * HG_HEAD_DIM:(h + 1) * HG_HEAD_DIM]

    row = lax.broadcasted_iota(jnp.int32, (grp, grp), 0)
    col = lax.broadcasted_iota(jnp.int32, (grp, grp), 1)
    n_levels = grp.bit_length() - 1

    def head_body(h, carry):
        for r0 in range(0, ts, grp):
            q = q_s[h, r0:r0 + grp, :]
            fg = f_s[h, r0:r0 + grp, :]
            iv = i_s[h, r0:r0 + grp, :].astype(BF16)
            k = 1.0 - fg
            ep, es, et = fg, jnp.ones_like(fg), fg
            a = jnp.where(row == col, _dot_nt(q.astype(BF16), k.astype(BF16)), 0.0)
            for lvl in range(n_levels):
                half = 1 << lvl
                odd = (row & half) != 0
                ql = jnp.where(odd, q * ep, 0.0).astype(BF16)
                kl = jnp.where(odd, 0.0, k * es).astype(BF16)
                al = _dot_nt(ql, kl)
                if lvl + 1 < n_levels:
                    al = jnp.where((row >> (lvl + 1)) == (col >> (lvl + 1)), al, 0.0)
                a = a + al
                prev = pltpu.roll(et, half, 0)
                nxt = pltpu.roll(et, grp - half, 0)
                ep = ep * jnp.where(odd, prev, 1.0)
                es = es * jnp.where(odd, 1.0, nxt)
                et = et * jnp.where(odd, prev, nxt)
            st = st_s[h]
            o = _dot(a.astype(BF16), iv) + _dot_nt((q * ep).astype(BF16), st.astype(BF16))
            st_s[h] = st * et[0:1, :] + _dot_tn(iv, (k * es).astype(BF16))
            ms = jnp.mean(o * o, axis=-1, keepdims=True)
            y = o * lax.rsqrt(ms + RMS_EPS) * gn_ref[...] * g_s[h, r0:r0 + grp, :]
            mix_s[h, r0:r0 + grp, :] = y.astype(BF16)
        return carry

    lax.fori_loop(0, n_heads, head_body, 0)

    acc = dn_alpha * x
    for h in range(n_heads):
        acc = acc + _dot(mix_s[h], w_o_ref[h])
    out_ref[0] = _layer_norm(acc, lg_ref[...], lbias_ref[...])


def _hgrn_layer(x, w_in, lb, gnorm, w_o, ln_g, ln_b, dn_alpha):
    bsz, s_len, d = x.shape
    n_heads = d // HG_HEAD_DIM
    ts = ROW_TILE
    const2 = lambda b, s: (0, 0)
    return pl.pallas_call(
        functools.partial(_hgrn_kernel, dn_alpha=dn_alpha),
        grid=(bsz, s_len // ts),
        in_specs=[
            pl.BlockSpec((1, ts, d), lambda b, s: (b, s, 0)),
            pl.BlockSpec((d, 4 * d), const2),
            pl.BlockSpec((1, d), const2),
            pl.BlockSpec((1, HG_HEAD_DIM), const2),
            pl.BlockSpec((n_heads, HG_HEAD_DIM, d), lambda b, s: (0, 0, 0)),
            pl.BlockSpec((1, d), const2),
            pl.BlockSpec((1, d), const2),
        ],
        out_specs=pl.BlockSpec((1, ts, d), lambda b, s: (b, s, 0)),
        out_shape=jax.ShapeDtypeStruct((bsz, s_len, d), F32),
        scratch_shapes=[
            pltpu.VMEM((n_heads, ts, HG_HEAD_DIM), F32),
            pltpu.VMEM((n_heads, ts, HG_HEAD_DIM), F32),
            pltpu.VMEM((n_heads, ts, HG_HEAD_DIM), F32),
            pltpu.VMEM((n_heads, ts, HG_HEAD_DIM), F32),
            pltpu.VMEM((n_heads, ts, HG_HEAD_DIM), BF16),
            pltpu.VMEM((n_heads, HG_HEAD_DIM, HG_HEAD_DIM), F32),
        ],
        compiler_params=_cparams("parallel", "arbitrary"),
        name="hgrn2_layer",
    )(x, w_in.astype(BF16), lb.reshape(1, d), gnorm.reshape(1, HG_HEAD_DIM),
      w_o.astype(BF16).reshape(n_heads, HG_HEAD_DIM, d), ln_g.reshape(1, d), ln_b.reshape(1, d))


def _rope_lanes(t, cc, ss):
    return t * cc + pltpu.roll(t, LANES // 2, 1) * ss


def _kv_kernel(x_ref, wa_ref, kvn_ref, wb_ref, cc_ref, ss_ref, k_ref, v_ref):
    n_heads = k_ref.shape[1]
    xb = x_ref[0].astype(BF16)
    ckr = _dot(xb, wa_ref[...])
    c = ckr[:, :KV_RANK]
    c = c * lax.rsqrt(jnp.mean(c * c, axis=-1, keepdims=True) + RMS_EPS) * kvn_ref[...]
    kr = _rope_lanes(ckr[:, KV_RANK:], cc_ref[0], ss_ref[0]).astype(BF16)
    cb = c.astype(BF16)
    for h in range(n_heads):
        kv = _dot(cb, wb_ref[:, h * (NOPE_DIM + V_DIM):(h + 1) * (NOPE_DIM + V_DIM)])
        k_ref[0, h, :, :NOPE_DIM] = kv[:, :NOPE_DIM].astype(BF16)
        k_ref[0, h, :, NOPE_DIM:] = kr
        v_ref[0, h] = kv[:, NOPE_DIM:].astype(BF16)


def _shared_kv(x, kv_w_a_l, kv_norm, kv_w_b, cc, ss):
    bsz, s_len, d = x.shape
    n_heads = kv_w_b.shape[1] // (NOPE_DIM + V_DIM)
    ts = ROW_TILE
    const2 = lambda b, s: (0, 0)
    return pl.pallas_call(
        _kv_kernel,
        grid=(bsz, s_len // ts),
        in_specs=[
            pl.BlockSpec((1, ts, d), lambda b, s: (b, s, 0)),
            pl.BlockSpec((d, KV_RANK + LANES), const2),
            pl.BlockSpec((1, KV_RANK), const2),
            pl.BlockSpec((KV_RANK, n_heads * (NOPE_DIM + V_DIM)), const2),
            pl.BlockSpec((1, ts, LANES), lambda b, s: (b, s, 0)),
            pl.BlockSpec((1, ts, LANES), lambda b, s: (b, s, 0)),
        ],
        out_specs=[
            pl.BlockSpec((1, n_heads, ts, QK_DIM), lambda b, s: (b, 0, s, 0)),
            pl.BlockSpec((1, n_heads, ts, V_DIM), lambda b, s: (b, 0, s, 0)),
        ],
        out_shape=[
            jax.ShapeDtypeStruct((bsz, n_heads, s_len, QK_DIM), BF16),
            jax.ShapeDtypeStruct((bsz, n_heads, s_len, V_DIM), BF16),
        ],
        compiler_params=_cparams("parallel", "parallel"),
        name="mla_shared_kv",
    )(x, kv_w_a_l, kv_norm.reshape(1, KV_RANK), kv_w_b.astype(BF16), cc, ss)


def _q_kernel(x_ref, wdq_ref, qn_ref, wuq_ref, cc_ref, ss_ref, q_ref, *, scale):
    n_heads = q_ref.shape[1]
    xb = x_ref[0].astype(BF16)
    c = _dot(xb, wdq_ref[...])
    c = c * lax.rsqrt(jnp.mean(c * c, axis=-1, keepdims=True) + RMS_EPS) * qn_ref[...]
    cb = c.astype(BF16)
    cc = cc_ref[0] * scale
    ss = ss_ref[0] * scale
    for h in range(n_heads):
        qh = _dot(cb, wuq_ref[:, h * QK_DIM:(h + 1) * QK_DIM])
        q_ref[0, h, :, :NOPE_DIM] = (qh[:, :NOPE_DIM] * scale).astype(BF16)
        q_ref[0, h, :, NOPE_DIM:] = _rope_lanes(qh[:, NOPE_DIM:], cc, ss).astype(BF16)


def _mla_queries(x, w_dq, q_norm, w_uq_l, cc, ss, scale):
    bsz, s_len, d = x.shape
    q_rank = w_dq.shape[1]
    n_heads = w_uq_l.shape[1] // QK_DIM
    ts = ROW_TILE
    const2 = lambda b, s: (0, 0)
    return pl.pallas_call(
        functools.partial(_q_kernel, scale=scale),
        grid=(bsz, s_len // ts),
        in_specs=[
            pl.BlockSpec((1, ts, d), lambda b, s: (b, s, 0)),
            pl.BlockSpec((d, q_rank), const2),
            pl.BlockSpec((1, q_rank), const2),
            pl.BlockSpec((q_rank, n_heads * QK_DIM), const2),
            pl.BlockSpec((1, ts, LANES), lambda b, s: (b, s, 0)),
            pl.BlockSpec((1, ts, LANES), lambda b, s: (b, s, 0)),
        ],
        out_specs=pl.BlockSpec((1, n_heads, ts, QK_DIM), lambda b, s: (b, 0, s, 0)),
        out_shape=jax.ShapeDtypeStruct((bsz, n_heads, s_len, QK_DIM), BF16),
        compiler_params=_cparams("parallel", "parallel"),
        name="mla_queries",
    )(x, w_dq.astype(BF16), q_norm.reshape(1, q_rank), w_uq_l, cc, ss)


def _attn_kernel(q_ref, k_ref, v_ref, o_ref):
    s_len = q_ref.shape[2]
    tq = min(ATT_TQ, s_len)
    row = lax.broadcasted_iota(jnp.int32, (tq, tq), 0)
    col = lax.broadcasted_iota(jnp.int32, (tq, tq), 1)
    for qi in range(s_len // tq):
        q = q_ref[0, 0, qi * tq:(qi + 1) * tq, :]
        m = jnp.full((tq, 1), -jnp.inf, F32)
        l = jnp.zeros((tq, 1), F32)
        acc = jnp.zeros((tq, V_DIM), F32)
        for kj in range(qi + 1):
            s = _dot_nt(q, k_ref[0, 0, kj * tq:(kj + 1) * tq, :])
            if kj == qi:
                s = jnp.where(col <= row, s, -jnp.inf)
            m_new = jnp.maximum(m, jnp.max(s, axis=-1, keepdims=True))
            p = jnp.exp(s - m_new)
            corr = jnp.exp(m - m_new)
            l = corr * l + jnp.sum(p, axis=-1, keepdims=True)
            acc = corr * acc + _dot(p.astype(BF16), v_ref[0, 0, kj * tq:(kj + 1) * tq, :])
            m = m_new
        o_ref[0, qi * tq:(qi + 1) * tq, :] = (acc / l).astype(o_ref.dtype)


def _mla_attention(q, k, v):
    bsz, n_heads, s_len, _ = q.shape
    return pl.pallas_call(
        _attn_kernel,
        grid=(bsz, n_heads),
        in_specs=[
            pl.BlockSpec((1, 1, s_len, QK_DIM), lambda b, h: (b, h, 0, 0)),
            pl.BlockSpec((1, 1, s_len, QK_DIM), lambda b, h: (b, h, 0, 0)),
            pl.BlockSpec((1, 1, s_len, V_DIM), lambda b, h: (b, h, 0, 0)),
        ],
        out_specs=pl.BlockSpec((1, s_len, V_DIM), lambda b, h: (b, 0, h)),
        out_shape=jax.ShapeDtypeStruct((bsz, s_len, n_heads * V_DIM), BF16),
        compiler_params=_cparams("parallel", "parallel"),
        name="mla_attention",
    )(q, k, v)


def _proj_ln_kernel(o_ref, x_ref, w_ref, lg_ref, lb_ref, out_ref, *, dn_alpha):
    y = dn_alpha * x_ref[...] + _dot(o_ref[...], w_ref[...])
    out_ref[...] = _layer_norm(y, lg_ref[...], lb_ref[...])


def _proj_residual_ln(o2, x2, w_o, ln_g, ln_b, dn_alpha):
    n_tok, d = x2.shape
    kdim = o2.shape[1]
    ts = ROW_TILE
    const = lambda i: (0, 0)
    return pl.pallas_call(
        functools.partial(_proj_ln_kernel, dn_alpha=dn_alpha),
        grid=(n_tok // ts,),
        in_specs=[
            pl.BlockSpec((ts, kdim), lambda i: (i, 0)),
            pl.BlockSpec((ts, d), lambda i: (i, 0)),
            pl.BlockSpec((kdim, d), const),
            pl.BlockSpec((1, d), const),
            pl.BlockSpec((1, d), const),
        ],
        out_specs=pl.BlockSpec((ts, d), lambda i: (i, 0)),
        out_shape=jax.ShapeDtypeStruct((n_tok, d), F32),
        compiler_params=_cparams("parallel"),
        name="mla_out_proj_ln",
    )(o2, x2, w_o.astype(BF16), ln_g.reshape(1, d), ln_b.reshape(1, d))


def _router_kernel(x_ref, w_ref, b_ref, idx_ref, gate_ref, cnt_ref, carry_s):
    tr = x_ref.shape[0]

    @pl.when(pl.program_id(0) == 0)
    def _():
        carry_s[...] = jnp.zeros_like(carry_s)

    logits = jnp.dot(x_ref[...], w_ref[...], preferred_element_type=F32,
                     precision=lax.Precision.HIGHEST) + b_ref[...]
    lane = lax.broadcasted_iota(jnp.int32, (tr, LANES), 1)
    work = logits
    sel = jnp.zeros((tr, LANES), F32)
    ids, vals = [], []
    for _ in range(TOP_K):
        mx = jnp.max(work, axis=-1, keepdims=True)
        idx = jnp.min(jnp.where(work == mx, lane, LANES), axis=-1, keepdims=True)
        hit = lane == idx
        ids.append(idx)
        vals.append(mx)
        sel = jnp.where(hit, 1.0, sel)
        work = jnp.where(hit, -jnp.inf, work)
    exps = [jnp.exp(v - vals[0]) for v in vals]
    denom = exps[0] + exps[1] + exps[2] + exps[3]

    r = lax.broadcasted_iota(jnp.int32, (tr, tr), 0)
    c = lax.broadcasted_iota(jnp.int32, (tr, tr), 1)
    before = _dot(jnp.where(c < r, 1.0, 0.0).astype(BF16), sel.astype(BF16)) + carry_s[...]
    carry_s[...] = carry_s[...] + jnp.sum(sel, axis=0, keepdims=True)
    cnt_ref[...] = carry_s[...]

    idx_out = jnp.zeros((tr, LANES), jnp.int32)
    gate_out = jnp.zeros((tr, LANES), F32)
    for j in range(TOP_K):
        rank = jnp.sum(jnp.where(lane == ids[j], before, 0.0), axis=-1, keepdims=True)
        idx_out = jnp.where(lane == j, ids[j], idx_out)
        idx_out = jnp.where(lane == TOP_K + j, rank.astype(jnp.int32), idx_out)
        gate_out = jnp.where(lane == j, exps[j] / denom, gate_out)
    idx_ref[...] = idx_out
    gate_ref[...] = gate_out


def _router(x2, router_w, router_b):
    n_tok, d = x2.shape
    n_exp = router_w.shape[1]
    tr = ROW_TILE
    w = jnp.zeros((d, LANES), F32).at[:, :n_exp].set(router_w)
    b = jnp.full((1, LANES), -jnp.inf, F32).at[0, :n_exp].set(router_b)
    const = lambda i: (0, 0)
    return pl.pallas_call(
        _router_kernel,
        grid=(n_tok // tr,),
        in_specs=[
            pl.BlockSpec((tr, d), lambda i: (i, 0)),
            pl.BlockSpec((d, LANES), const),
            pl.BlockSpec((1, LANES), const),
        ],
        out_specs=[
            pl.BlockSpec((tr, LANES), lambda i: (i, 0)),
            pl.BlockSpec((tr, LANES), lambda i: (i, 0)),
            pl.BlockSpec((1, LANES), const),
        ],
        out_shape=[
            jax.ShapeDtypeStruct((n_tok, LANES), jnp.int32),
            jax.ShapeDtypeStruct((n_tok, LANES), F32),
            jax.ShapeDtypeStruct((1, LANES), F32),
        ],
        scratch_shapes=[pltpu.VMEM((1, LANES), F32)],
        compiler_params=_cparams("arbitrary"),
        name="moe_router",
    )(x2, w, b)


def _dispatch_kernel(dest_ref, x_hbm, xs_hbm, sem):
    n = dest_ref.shape[2]
    base = pl.program_id(0) * (n // TOP_K)

    def row_copy(a):
        return pltpu.make_async_copy(x_hbm.at[pl.ds(base + a // TOP_K, 1)],
                                     xs_hbm.at[pl.ds(dest_ref[0, 0, a], 1)], sem)

    def start(a, c):
        row_copy(a).start()
        return c

    def wait(a, c):
        row_copy(a).wait()
        return c

    lax.fori_loop(0, n, start, 0)
    lax.fori_loop(0, n, wait, 0)


def _dispatch(x2, dest, n_rows):
    n_tok, d = x2.shape
    td = DMA_TILE
    n_tiles = n_tok // td
    return pl.pallas_call(
        _dispatch_kernel,
        grid=(n_tiles,),
        in_specs=[
            pl.BlockSpec((1, 1, td * TOP_K), lambda i: (i, 0, 0), memory_space=pltpu.SMEM),
            pl.BlockSpec(memory_space=pl.ANY),
        ],
        out_specs=pl.BlockSpec(memory_space=pl.ANY),
        out_shape=jax.ShapeDtypeStruct((n_rows, d), F32),
        scratch_shapes=[pltpu.SemaphoreType.DMA(())],
        compiler_params=_cparams("arbitrary"),
        name="moe_dispatch",
    )(dest.reshape(n_tiles, 1, td * TOP_K), x2)


def _expert_kernel(be_ref, nu_ref, xs_ref, w1_ref, b1_ref, w2_ref, b2_ref, ys_ref):
    @pl.when(pl.program_id(0) < nu_ref[0])
    def _():
        f = w2_ref.shape[1]
        h = _dot(xs_ref[...].astype(BF16), w1_ref[0]) + b1_ref[0]
        glu = jnp.minimum(h[:, :f], SWIGLU_LIMIT)
        lin = jnp.clip(h[:, f:], -SWIGLU_LIMIT, SWIGLU_LIMIT)
        a = glu * _sigmoid(SWIGLU_ALPHA * glu) * (lin + 1.0)
        ys_ref[...] = _dot(a.astype(BF16), w2_ref[0]) + b2_ref[0]


def _experts(xs, block_e, n_used, w1, b1, w2, b2):
    n_rows, d = xs.shape
    n_exp, f = w2.shape[0], w2.shape[1]
    blk = MOE_BLOCK
    n_blocks = n_rows // blk
    row_map = lambda i, be, nu: (jnp.minimum(i, nu[0] - 1), 0)
    exp_map = lambda i, be, nu: (be[i], 0, 0)
    grid_spec = pltpu.PrefetchScalarGridSpec(
        num_scalar_prefetch=2,
        grid=(n_blocks,),
        in_specs=[
            pl.BlockSpec((blk, d), row_map),
            pl.BlockSpec((1, d, 2 * f), exp_map),
            pl.BlockSpec((1, 1, 2 * f), exp_map),
            pl.BlockSpec((1, f, d), exp_map),
            pl.BlockSpec((1, 1, d), exp_map),
        ],
        out_specs=pl.BlockSpec((blk, d), row_map),
    )
    return pl.pallas_call(
        _expert_kernel,
        grid_spec=grid_spec,
        out_shape=jax.ShapeDtypeStruct((n_rows, d), F32),
        compiler_params=_cparams("arbitrary"),
        name="moe_experts",
    )(block_e, n_used, xs, w1, b1.reshape(n_exp, 1, 2 * f), w2, b2.reshape(n_exp, 1, d))


def _combine_kernel(dest_ref, ys_hbm, x_ref, gate_ref, lg_ref, lb_ref, out_ref, buf, sem, *, dn_alpha):
    n = dest_ref.shape[2]

    def row_copy(a):
        return pltpu.make_async_copy(ys_hbm.at[pl.ds(dest_ref[0, 0, a], 1)],
                                     buf.at[a % TOP_K, pl.ds(a // TOP_K, 1)], sem)

    def start(a, c):
        row_copy(a).start()
        return c

    def wait(a, c):
        row_copy(a).wait()
        return c

    lax.fori_loop(0, n, start, 0)
    lax.fori_loop(0, n, wait, 0)

    gates = gate_ref[...]
    y = dn_alpha * x_ref[...]
    for j in range(TOP_K):
        y = y + gates[:, j:j + 1] * buf[j]
    out_ref[...] = _layer_norm(y, lg_ref[...], lb_ref[...])


def _combine_ln(ys, dest, gates, x2, ln_g, ln_b, dn_alpha):
    n_tok, d = x2.shape
    td = DMA_TILE
    n_tiles = n_tok // td
    const = lambda i: (0, 0)
    return pl.pallas_call(
        functools.partial(_combine_kernel, dn_alpha=dn_alpha),
        grid=(n_tiles,),
        in_specs=[
            pl.BlockSpec((1, 1, td * TOP_K), lambda i: (i, 0, 0), memory_space=pltpu.SMEM),
            pl.BlockSpec(memory_space=pl.ANY),
            pl.BlockSpec((td, d), lambda i: (i, 0)),
            pl.BlockSpec((td, LANES), lambda i: (i, 0)),
            pl.BlockSpec((1, d), const),
            pl.BlockSpec((1, d), const),
        ],
        out_specs=pl.BlockSpec((td, d), lambda i: (i, 0)),
        out_shape=jax.ShapeDtypeStruct((n_tok, d), F32),
        scratch_shapes=[pltpu.VMEM((TOP_K, td, d), F32), pltpu.SemaphoreType.DMA(())],
        compiler_params=_cparams("arbitrary"),
        name="moe_combine_ln",
    )(dest.reshape(n_tiles, 1, td * TOP_K), ys, x2, gates, ln_g.reshape(1, d), ln_b.reshape(1, d))


def _moe_layer(x2, router_w, router_b, w1, b1, w2, b2, ln_g, ln_b, dn_alpha):
    n_tok, d = x2.shape
    n_exp, f = w2.shape[0], w2.shape[1]
    blk = MOE_BLOCK
    n_blocks = -(-(n_tok * TOP_K) // blk) + n_exp
    n_rows = n_blocks * blk

    idx, gates, counts = _router(x2, router_w, router_b)

    cnt = counts[0, :n_exp].astype(jnp.int32)
    padded = (cnt + blk - 1) // blk * blk
    pends = jnp.cumsum(padded)
    pstarts = pends - padded
    block_e = jnp.minimum(
        jnp.searchsorted(pends, jnp.arange(n_blocks, dtype=jnp.int32) * blk, side='right'),
        n_exp - 1).astype(jnp.int32)
    n_used = (pends[-1:] // blk).astype(jnp.int32)
    dest = (pstarts[idx[:, :TOP_K]] + idx[:, TOP_K:2 * TOP_K]).astype(jnp.int32)

    w1_l = jnp.concatenate([w1[:, :, 0::2], w1[:, :, 1::2]], axis=-1).astype(BF16)
    b1_l = jnp.concatenate([b1[:, 0::2], b1[:, 1::2]], axis=-1)

    xs = _dispatch(x2, dest, n_rows)
    ys = _experts(xs, block_e, n_used, w1_l, b1_l, w2.astype(BF16), b2)
    return _combine_ln(ys, dest, gates, x2, ln_g, ln_b, dn_alpha)


def _rope_lane_tables(positions):
    half = ROPE_DIM // 2
    inv_freq = ROPE_THETA ** (-jnp.arange(0, ROPE_DIM, 2, dtype=F32) / ROPE_DIM)
    ang = positions.astype(F32)[..., None] * inv_freq
    cos, sin = jnp.cos(ang), jnp.sin(ang)
    z = jnp.zeros_like(cos)
    assert 4 * half == LANES
    return (jnp.concatenate([cos, z, cos, z], axis=-1),
            jnp.concatenate([-sin, z, sin, z], axis=-1))


def _rope_lane_columns(w_rope):
    half = ROPE_DIM // 2
    z = jnp.zeros(w_rope.shape[:-1] + (half,), w_rope.dtype)
    return jnp.concatenate([w_rope[..., :half], z, w_rope[..., half:], z], axis=-1)


def kernel(x, positions, ln_g, ln_b, hg_w_in, hg_lb, hg_gnorm, hg_w_o, mla_w_dq, mla_q_norm,
           mla_w_uq, mla_w_o, kv_w_a, kv_norm, kv_w_b, router_w, router_b, moe_w1, moe_b1,
           moe_w2, moe_b2):
    bsz, s_len, d = x.shape
    depth = ln_g.shape[0]
    n_a = hg_w_in.shape[0]
    dn_alpha = (2.0 * depth) ** 0.25
    scale = (NOPE_DIM + ROPE_DIM) ** -0.5

    lb_soft = jax.nn.softmax(hg_lb.astype(F32), axis=0)
    lower_bounds = jnp.cumsum(lb_soft, axis=0) - lb_soft[0]

    q_rank = mla_w_uq.shape[1]
    mla_heads = mla_w_uq.shape[2] // (NOPE_DIM + ROPE_DIM)
    cc = ss = k_full = v_full = None

    for layer in range(depth):
        if layer < n_a:
            x = _hgrn_layer(x, hg_w_in[layer], lower_bounds[layer], hg_gnorm[layer], hg_w_o[layer],
                            ln_g[layer, 0], ln_b[layer, 0], dn_alpha)
        else:
            j = layer - n_a
            if layer == n_a:
                cc, ss = _rope_lane_tables(positions)
                kv_w_a_l = jnp.concatenate(
                    [kv_w_a[:, :KV_RANK], _rope_lane_columns(kv_w_a[:, KV_RANK:])], axis=-1).astype(BF16)
                k_full, v_full = _shared_kv(x, kv_w_a_l, kv_norm, kv_w_b, cc, ss)
            w_uq = mla_w_uq[j].reshape(q_rank, mla_heads, NOPE_DIM + ROPE_DIM)
            w_uq_l = jnp.concatenate(
                [w_uq[..., :NOPE_DIM], _rope_lane_columns(w_uq[..., NOPE_DIM:])],
                axis=-1).reshape(q_rank, mla_heads * QK_DIM).astype(BF16)
            q_full = _mla_queries(x, mla_w_dq[j], mla_q_norm[j], w_uq_l, cc, ss, scale)
            o = _mla_attention(q_full, k_full, v_full)
            x = _proj_residual_ln(o.reshape(bsz * s_len, -1), x.reshape(bsz * s_len, d), mla_w_o[j],
                                  ln_g[layer, 0], ln_b[layer, 0], dn_alpha).reshape(bsz, s_len, d)
        x = _moe_layer(x.reshape(bsz * s_len, d), router_w[layer], router_b[layer], moe_w1[layer],
                       moe_b1[layer], moe_w2[layer], moe_b2[layer], ln_g[layer, 1], ln_b[layer, 1],
                       dn_alpha).reshape(bsz, s_len, d)
    return x
```

```python
import functools

import jax
import jax.numpy as jnp
from jax import lax
from jax.experimental import pallas as pl
from jax.experimental.pallas import tpu as pltpu

F32 = jnp.float32
BF16 = jnp.bfloat16

HG_HEAD_DIM = 128
NOPE_DIM = 128
ROPE_DIM = 64
V_DIM = 128
KV_RANK = 128
ROPE_THETA = 10000.0
TOP_K = 4
SWIGLU_ALPHA = 1.702
SWIGLU_LIMIT = 7.0
LN_EPS = 1e-5
RMS_EPS = 1e-6

LANES = 128
ROW_SUBLANES = 8
QK_DIM = 256
VMEM_LIMIT = 56 * 1024 * 1024

ROW_TILE = 256
HG_GROUP = 128
ATT_TQ = 512
MOE_BLOCK = 512
DMA_TILE = 256


def _cparams(*sem):
    return pltpu.CompilerParams(dimension_semantics=sem, vmem_limit_bytes=VMEM_LIMIT)


def _layer_norm(y, g, b):
    mu = jnp.mean(y, axis=-1, keepdims=True)
    d = y - mu
    var = jnp.mean(d * d, axis=-1, keepdims=True)
    return d * lax.rsqrt(var + LN_EPS) * g + b


def _dot(a, b):
    return jnp.dot(a, b, preferred_element_type=F32)


def _dot_nt(a, b):
    return lax.dot_general(a, b, (((1,), (1,)), ((), ())), preferred_element_type=F32)


def _dot_tn(a, b):
    return lax.dot_general(a, b, (((0,), (0,)), ((), ())), preferred_element_type=F32)


def _sigmoid(x):
    return 1.0 / (1.0 + jnp.exp(-x))


def _hgrn_kernel(x_ref, w_in_ref, lb_ref, gn_ref, w_o_ref, lg_ref, lbias_ref, out_ref,
                 q_s, f_s, i_s, g_s, mix_s, st_s, *, dn_alpha):
    ts, d = x_ref.shape[1], x_ref.shape[2]
    n_heads = d // HG_HEAD_DIM
    grp = HG_GROUP

    @pl.when(pl.program_id(1) == 0)
    def _():
        st_s[...] = jnp.zeros_like(st_s)

    x = x_ref[0]
    xb = x.astype(BF16)
    lb = lb_ref[...]

    for sec, dst in enumerate((q_s, f_s, i_s, g_s)):
        p = _dot(xb, w_in_ref[:, sec * d:(sec + 1) * d])
        if sec == 0 or sec == 3:
            p = p * _sigmoid(p)
        elif sec == 1:
            p = lb + (1.0 - lb) * _sigmoid(p)
        for h in range(n_heads):
            dst[h] = p[:, h * HG_HEAD_DIM:(h + 1) * HG_HEAD_DIM]

    row = lax.broadcasted_iota(jnp.int32, (grp, grp), 0)
    col = lax.broadcasted_iota(jnp.int32, (grp, grp), 1)
    n_levels = grp.bit_length() - 1

    def head_body(h, carry):
        for r0 in range(0, ts, grp):
            q = q_s[h, r0:r0 + grp, :]
            fg = f_s[h, r0:r0 + grp, :]
            iv = i_s[h, r0:r0 + grp, :].astype(BF16)
            k = 1.0 - fg
            ep, es, et = fg, jnp.ones_like(fg), fg
            a = jnp.where(row == col, _dot_nt(q.astype(BF16), k.astype(BF16)), 0.0)
            for lvl in range(n_levels):
                half = 1 << lvl
                odd = (row & half) != 0
                ql = jnp.where(odd, q * ep, 0.0).astype(BF16)
                kl = jnp.where(odd, 0.0, k * es).astype(BF16)
                al = _dot_nt(ql, kl)
                if lvl + 1 < n_levels:
                    al = jnp.where((row >> (lvl + 1)) == (col >> (lvl + 1)), al, 0.0)
                a = a + al
                prev = pltpu.roll(et, half, 0)
                nxt = pltpu.roll(et, grp - half, 0)
                ep = ep * jnp.where(odd, prev, 1.0)
                es = es * jnp.where(odd, 1.0, nxt)
                et = et * jnp.where(odd, prev, nxt)
            st = st_s[h]
            o = _dot(a.astype(BF16), iv) + _dot_nt((q * ep).astype(BF16), st.astype(BF16))
            st_s[h] = st * et[0:1, :] + _dot_tn(iv, (k * es).astype(BF16))
            ms = jnp.mean(o * o, axis=-1, keepdims=True)
            y = o * lax.rsqrt(ms + RMS_EPS) * gn_ref[...] * g_s[h, r0:r0 + grp, :]
            mix_s[h, r0:r0 + grp, :] = y.astype(BF16)
        return carry

    lax.fori_loop(0, n_heads, head_body, 0)

    acc = dn_alpha * x
    for h in range(n_heads):
        acc = acc + _dot(mix_s[h], w_o_ref[h])
    out_ref[0] = _layer_norm(acc, lg_ref[...], lbias_ref[...])


def _hgrn_layer(x, w_in, lb, gnorm, w_o, ln_g, ln_b, dn_alpha):
    bsz, s_len, d = x.shape
    n_heads = d // HG_HEAD_DIM
    ts = ROW_TILE
    const2 = lambda b, s: (0, 0)
    return pl.pallas_call(
        functools.partial(_hgrn_kernel, dn_alpha=dn_alpha),
        grid=(bsz, s_len // ts),
        in_specs=[
            pl.BlockSpec((1, ts, d), lambda b, s: (b, s, 0)),
            pl.BlockSpec((d, 4 * d), const2),
            pl.BlockSpec((1, d), const2),
            pl.BlockSpec((1, HG_HEAD_DIM), const2),
            pl.BlockSpec((n_heads, HG_HEAD_DIM, d), lambda b, s: (0, 0, 0)),
            pl.BlockSpec((1, d), const2),
            pl.BlockSpec((1, d), const2),
        ],
        out_specs=pl.BlockSpec((1, ts, d), lambda b, s: (b, s, 0)),
        out_shape=jax.ShapeDtypeStruct((bsz, s_len, d), F32),
        scratch_shapes=[
            pltpu.VMEM((n_heads, ts, HG_HEAD_DIM), F32),
            pltpu.VMEM((n_heads, ts, HG_HEAD_DIM), F32),
            pltpu.VMEM((n_heads, ts, HG_HEAD_DIM), F32),
            pltpu.VMEM((n_heads, ts, HG_HEAD_DIM), F32),
            pltpu.VMEM((n_heads, ts, HG_HEAD_DIM), BF16),
            pltpu.VMEM((n_heads, HG_HEAD_DIM, HG_HEAD_DIM), F32),
        ],
        compiler_params=_cparams("parallel", "arbitrary"),
        name="hgrn2_layer",
    )(x, w_in.astype(BF16), lb.reshape(1, d), gnorm.reshape(1, HG_HEAD_DIM),
      w_o.astype(BF16).reshape(n_heads, HG_HEAD_DIM, d), ln_g.reshape(1, d), ln_b.reshape(1, d))


def _rope_lanes(t, cc, ss):
    return t * cc + pltpu.roll(t, LANES // 2, 1) * ss


def _kv_kernel(x_ref, wa_ref, kvn_ref, wb_ref, cc_ref, ss_ref, k_ref, v_ref):
    n_heads = k_ref.shape[1]
    xb = x_ref[0].astype(BF16)
    ckr = _dot(xb, wa_ref[...])
    c = ckr[:, :KV_RANK]
    c = c * lax.rsqrt(jnp.mean(c * c, axis=-1, keepdims=True) + RMS_EPS) * kvn_ref[...]
    kr = _rope_lanes(ckr[:, KV_RANK:], cc_ref[0], ss_ref[0]).astype(BF16)
    cb = c.astype(BF16)
    for h in range(n_heads):
        kv = _dot(cb, wb_ref[:, h * (NOPE_DIM + V_DIM):(h + 1) * (NOPE_DIM + V_DIM)])
        k_ref[0, h, :, :NOPE_DIM] = kv[:, :NOPE_DIM].astype(BF16)
        k_ref[0, h, :, NOPE_DIM:] = kr
        v_ref[0, h] = kv[:, NOPE_DIM:].astype(BF16)


def _shared_kv(x, kv_w_a_l, kv_norm, kv_w_b, cc, ss):
    bsz, s_len, d = x.shape
    n_heads = kv_w_b.shape[1] // (NOPE_DIM + V_DIM)
    ts = ROW_TILE
    const2 = lambda b, s: (0, 0)
    return pl.pallas_call(
        _kv_kernel,
        grid=(bsz, s_len // ts),
        in_specs=[
            pl.BlockSpec((1, ts, d), lambda b, s: (b, s, 0)),
            pl.BlockSpec((d, KV_RANK + LANES), const2),
            pl.BlockSpec((1, KV_RANK), const2),
            pl.BlockSpec((KV_RANK, n_heads * (NOPE_DIM + V_DIM)), const2),
            pl.BlockSpec((1, ts, LANES), lambda b, s: (b, s, 0)),
            pl.BlockSpec((1, ts, LANES), lambda b, s: (b, s, 0)),
        ],
        out_specs=[
            pl.BlockSpec((1, n_heads, ts, QK_DIM), lambda b, s: (b, 0, s, 0)),
            pl.BlockSpec((1, n_heads, ts, V_DIM), lambda b, s: (b, 0, s, 0)),
        ],
        out_shape=[
            jax.ShapeDtypeStruct((bsz, n_heads, s_len, QK_DIM), BF16),
            jax.ShapeDtypeStruct((bsz, n_heads, s_len, V_DIM), BF16),
        ],
        compiler_params=_cparams("parallel", "parallel"),
        name="mla_shared_kv",
    )(x, kv_w_a_l, kv_norm.reshape(1, KV_RANK), kv_w_b.astype(BF16), cc, ss)


def _q_kernel(x_ref, wdq_ref, qn_ref, wuq_ref, cc_ref, ss_ref, q_ref, *, scale):
    n_heads = q_ref.shape[1]
    xb = x_ref[0].astype(BF16)
    c = _dot(xb, wdq_ref[...])
    c = c * lax.rsqrt(jnp.mean(c * c, axis=-1, keepdims=True) + RMS_EPS) * qn_ref[...]
    cb = c.astype(BF16)
    cc = cc_ref[0] * scale
    ss = ss_ref[0] * scale
    for h in range(n_heads):
        qh = _dot(cb, wuq_ref[:, h * QK_DIM:(h + 1) * QK_DIM])
        q_ref[0, h, :, :NOPE_DIM] = (qh[:, :NOPE_DIM] * scale).astype(BF16)
        q_ref[0, h, :, NOPE_DIM:] = _rope_lanes(qh[:, NOPE_DIM:], cc, ss).astype(BF16)


def _mla_queries(x, w_dq, q_norm, w_uq_l, cc, ss, scale):
    bsz, s_len, d = x.shape
    q_rank = w_dq.shape[1]
    n_heads = w_uq_l.shape[1] // QK_DIM
    ts = ROW_TILE
    const2 = lambda b, s: (0, 0)
    return pl.pallas_call(
        functools.partial(_q_kernel, scale=scale),
        grid=(bsz, s_len // ts),
        in_specs=[
            pl.BlockSpec((1, ts, d), lambda b, s: (b, s, 0)),
            pl.BlockSpec((d, q_rank), const2),
            pl.BlockSpec((1, q_rank), const2),
            pl.BlockSpec((q_rank, n_heads * QK_DIM), const2),
            pl.BlockSpec((1, ts, LANES), lambda b, s: (b, s, 0)),
            pl.BlockSpec((1, ts, LANES), lambda b, s: (b, s, 0)),
        ],
        out_specs=pl.BlockSpec((1, n_heads, ts, QK_DIM), lambda b, s: (b, 0, s, 0)),
        out_shape=jax.ShapeDtypeStruct((bsz, n_heads, s_len, QK_DIM), BF16),
        compiler_params=_cparams("parallel", "parallel"),
        name="mla_queries",
    )(x, w_dq.astype(BF16), q_norm.reshape(1, q_rank), w_uq_l, cc, ss)


def _attn_kernel(q_ref, k_ref, v_ref, o_ref):
    s_len = q_ref.shape[2]
    tq = min(ATT_TQ, s_len)
    row = lax.broadcasted_iota(jnp.int32, (tq, tq), 0)
    col = lax.broadcasted_iota(jnp.int32, (tq, tq), 1)
    for qi in range(s_len // tq):
        q = q_ref[0, 0, qi * tq:(qi + 1) * tq, :]
        m = jnp.full((tq, 1), -jnp.inf, F32)
        l = jnp.zeros((tq, 1), F32)
        acc = jnp.zeros((tq, V_DIM), F32)
        for kj in range(qi + 1):
            s = _dot_nt(q, k_ref[0, 0, kj * tq:(kj + 1) * tq, :])
            if kj == qi:
                s = jnp.where(col <= row, s, -jnp.inf)
            m_new = jnp.maximum(m, jnp.max(s, axis=-1, keepdims=True))
            p = jnp.exp(s - m_new)
            corr = jnp.exp(m - m_new)
            l = corr * l + jnp.sum(p, axis=-1, keepdims=True)
            acc = corr * acc + _dot(p.astype(BF16), v_ref[0, 0, kj * tq:(kj + 1) * tq, :])
            m = m_new
        o_ref[0, qi * tq:(qi + 1) * tq, :] = (acc / l).astype(o_ref.dtype)


def _mla_attention(q, k, v):
    bsz, n_heads, s_len, _ = q.shape
    return pl.pallas_call(
        _attn_kernel,
        grid=(bsz, n_heads),
        in_specs=[
            pl.BlockSpec((1, 1, s_len, QK_DIM), lambda b, h: (b, h, 0, 0)),
            pl.BlockSpec((1, 1, s_len, QK_DIM), lambda b, h: (b, h, 0, 0)),
            pl.BlockSpec((1, 1, s_len, V_DIM), lambda b, h: (b, h, 0, 0)),
        ],
        out_specs=pl.BlockSpec((1, s_len, V_DIM), lambda b, h: (b, 0, h)),
        out_shape=jax.ShapeDtypeStruct((bsz, s_len, n_heads * V_DIM), BF16),
        compiler_params=_cparams("parallel", "parallel"),
        name="mla_attention",
    )(q, k, v)


def _proj_ln_kernel(o_ref, x_ref, w_ref, lg_ref, lb_ref, out_ref, *, dn_alpha):
    y = dn_alpha * x_ref[...] + _dot(o_ref[...], w_ref[...])
    out_ref[...] = _layer_norm(y, lg_ref[...], lb_ref[...])


def _proj_residual_ln(o2, x2, w_o, ln_g, ln_b, dn_alpha):
    n_tok, d = x2.shape
    kdim = o2.shape[1]
    ts = ROW_TILE
    const = lambda i: (0, 0)
    return pl.pallas_call(
        functools.partial(_proj_ln_kernel, dn_alpha=dn_alpha),
        grid=(n_tok // ts,),
        in_specs=[
            pl.BlockSpec((ts, kdim), lambda i: (i, 0)),
            pl.BlockSpec((ts, d), lambda i: (i, 0)),
            pl.BlockSpec((kdim, d), const),
            pl.BlockSpec((1, d), const),
            pl.BlockSpec((1, d), const),
        ],
        out_specs=pl.BlockSpec((ts, d), lambda i: (i, 0)),
        out_shape=jax.ShapeDtypeStruct((n_tok, d), F32),
        compiler_params=_cparams("parallel"),
        name="mla_out_proj_ln",
    )(o2, x2, w_o.astype(BF16), ln_g.reshape(1, d), ln_b.reshape(1, d))


def _router_kernel(x_ref, w_ref, b_ref, idx_ref, gate_ref, cnt_ref, carry_s):
    tr = x_ref.shape[0]

    @pl.when(pl.program_id(0) == 0)
    def _():
        carry_s[...] = jnp.zeros_like(carry_s)

    logits = jnp.dot(x_ref[...], w_ref[...], preferred_element_type=F32,
                     precision=lax.Precision.HIGHEST) + b_ref[...]
    lane = lax.broadcasted_iota(jnp.int32, (tr, LANES), 1)
    work = logits
    sel = jnp.zeros((tr, LANES), F32)
    ids, vals = [], []
    for _ in range(TOP_K):
        mx = jnp.max(work, axis=-1, keepdims=True)
        idx = jnp.min(jnp.where(work == mx, lane, LANES), axis=-1, keepdims=True)
        hit = lane == idx
        ids.append(idx)
        vals.append(mx)
        sel = jnp.where(hit, 1.0, sel)
        work = jnp.where(hit, -jnp.inf, work)
    exps = [jnp.exp(v - vals[0]) for v in vals]
    denom = exps[0] + exps[1] + exps[2] + exps[3]

    r = lax.broadcasted_iota(jnp.int32, (tr, tr), 0)
    c = lax.broadcasted_iota(jnp.int32, (tr, tr), 1)
    before = _dot(jnp.where(c < r, 1.0, 0.0).astype(BF16), sel.astype(BF16)) + carry_s[...]
    carry_s[...] = carry_s[...] + jnp.sum(sel, axis=0, keepdims=True)
    cnt_ref[...] = carry_s[...]

    idx_out = jnp.zeros((tr, LANES), jnp.int32)
    gate_out = jnp.zeros((tr, LANES), F32)
    for j in range(TOP_K):
        rank = jnp.sum(jnp.where(lane == ids[j], before, 0.0), axis=-1, keepdims=True)
        idx_out = jnp.where(lane == j, ids[j], idx_out)
        idx_out = jnp.where(lane == TOP_K + j, rank.astype(jnp.int32), idx_out)
        gate_out = jnp.where(lane == j, exps[j] / denom, gate_out)
    idx_ref[...] = idx_out
    gate_ref[...] = gate_out


def _router(x2, router_w, router_b):
    n_tok, d = x2.shape
    n_exp = router_w.shape[1]
    tr = ROW_TILE
    w = jnp.zeros((d, LANES), F32).at[:, :n_exp].set(router_w)
    b = jnp.full((1, LANES), -jnp.inf, F32).at[0, :n_exp].set(router_b)
    const = lambda i: (0, 0)
    return pl.pallas_call(
        _router_kernel,
        grid=(n_tok // tr,),
        in_specs=[
            pl.BlockSpec((tr, d), lambda i: (i, 0)),
            pl.BlockSpec((d, LANES), const),
            pl.BlockSpec((1, LANES), const),
        ],
        out_specs=[
            pl.BlockSpec((tr, LANES), lambda i: (i, 0)),
            pl.BlockSpec((tr, LANES), lambda i: (i, 0)),
            pl.BlockSpec((1, LANES), const),
        ],
        out_shape=[
            jax.ShapeDtypeStruct((n_tok, LANES), jnp.int32),
            jax.ShapeDtypeStruct((n_tok, LANES), F32),
            jax.ShapeDtypeStruct((1, LANES), F32),
        ],
        scratch_shapes=[pltpu.VMEM((1, LANES), F32)],
        compiler_params=_cparams("arbitrary"),
        name="moe_router",
    )(x2, w, b)


def _rows_to_tiles(tile_ref, val):
    for s in range(ROW_SUBLANES):
        tile_ref[pl.ds(s, val.shape[0], stride=ROW_SUBLANES), :] = val[:, s * LANES:(s + 1) * LANES]


def _dispatch_kernel(dest_ref, x_ref, xs_hbm, stage, sems):
    td = x_ref.shape[0]
    step = pl.program_id(0)
    slot = step % 2

    def row_copy(t, j, sl):
        src = stage.at[sl, pl.ds(pl.multiple_of(t * ROW_SUBLANES, ROW_SUBLANES), ROW_SUBLANES)]
        d = dest_ref[0, 0, t * TOP_K + j]
        dst = xs_hbm.at[pl.ds(pl.multiple_of(d * ROW_SUBLANES, ROW_SUBLANES), ROW_SUBLANES)]
        return pltpu.make_async_copy(src, dst, sems.at[sl])

    def wait_slot(sl):
        def body(t, c):
            for j in range(TOP_K):
                row_copy(t, j, sl).wait()
            return c
        lax.fori_loop(0, td, body, 0)

    @pl.when(step >= 2)
    def _():
        wait_slot(slot)

    _rows_to_tiles(stage.at[slot], x_ref[...])

    def start(t, c):
        for j in range(TOP_K):
            row_copy(t, j, slot).start()
        return c
    lax.fori_loop(0, td, start, 0)

    @pl.when(step == pl.num_programs(0) - 1)
    def _():
        wait_slot(slot)

        @pl.when(step >= 1)
        def _():
            wait_slot(1 - slot)


def _dispatch(x2, dest, n_rows):
    n_tok, d = x2.shape
    assert d == ROW_SUBLANES * LANES
    td = DMA_TILE
    n_tiles = n_tok // td
    return pl.pallas_call(
        _dispatch_kernel,
        grid=(n_tiles,),
        in_specs=[
            pl.BlockSpec((1, 1, td * TOP_K), lambda i: (i, 0, 0), memory_space=pltpu.SMEM),
            pl.BlockSpec((td, d), lambda i: (i, 0)),
        ],
        out_specs=pl.BlockSpec(memory_space=pl.ANY),
        out_shape=jax.ShapeDtypeStruct((n_rows * ROW_SUBLANES, LANES), F32),
        scratch_shapes=[pltpu.VMEM((2, td * ROW_SUBLANES, LANES), F32), pltpu.SemaphoreType.DMA((2,))],
        compiler_params=_cparams("arbitrary"),
        name="moe_dispatch",
    )(dest.reshape(n_tiles, 1, td * TOP_K), x2)


def _expert_kernel(be_ref, nu_ref, xs_ref, w1_ref, b1_ref, w2_ref, b2_ref, ys_ref, xb_s):
    @pl.when(pl.program_id(0) < nu_ref[0])
    def _():
        f = w2_ref.shape[1]
        blk = xb_s.shape[0]
        for s in range(ROW_SUBLANES):
            xb_s[:, s * LANES:(s + 1) * LANES] = xs_ref[pl.ds(s, blk, stride=ROW_SUBLANES), :].astype(BF16)
        h = _dot(xb_s[...], w1_ref[0]) + b1_ref[0]
        glu = jnp.minimum(h[:, :f], SWIGLU_LIMIT)
        lin = jnp.clip(h[:, f:], -SWIGLU_LIMIT, SWIGLU_LIMIT)
        a = glu * _sigmoid(SWIGLU_ALPHA * glu) * (lin + 1.0)
        _rows_to_tiles(ys_ref, _dot(a.astype(BF16), w2_ref[0]) + b2_ref[0])


def _experts(xs, block_e, n_used, w1, b1, w2, b2):
    n_exp, f, d = w2.shape
    blk = MOE_BLOCK
    n_blocks = xs.shape[0] // (blk * ROW_SUBLANES)
    row_map = lambda i, be, nu: (jnp.minimum(i, nu[0] - 1), 0)
    exp_map = lambda i, be, nu: (be[i], 0, 0)
    grid_spec = pltpu.PrefetchScalarGridSpec(
        num_scalar_prefetch=2,
        grid=(n_blocks,),
        in_specs=[
            pl.BlockSpec((blk * ROW_SUBLANES, LANES), row_map),
            pl.BlockSpec((1, d, 2 * f), exp_map),
            pl.BlockSpec((1, 1, 2 * f), exp_map),
            pl.BlockSpec((1, f, d), exp_map),
            pl.BlockSpec((1, 1, d), exp_map),
        ],
        out_specs=pl.BlockSpec((blk * ROW_SUBLANES, LANES), row_map),
        scratch_shapes=[pltpu.VMEM((blk, d), BF16)],
    )
    return pl.pallas_call(
        _expert_kernel,
        grid_spec=grid_spec,
        out_shape=jax.ShapeDtypeStruct(xs.shape, F32),
        compiler_params=_cparams("arbitrary"),
        name="moe_experts",
    )(block_e, n_used, xs, w1, b1.reshape(n_exp, 1, 2 * f), w2, b2.reshape(n_exp, 1, d))


def _combine_kernel(dcur_ref, dnext_ref, ys_hbm, x_ref, gate_ref, lg_ref, lb_ref, out_ref, buf, sems,
                    *, dn_alpha):
    td = x_ref.shape[0]
    step = pl.program_id(0)
    slot = step % 2

    def row_copy(dest_ref, t, j, sl):
        d = dest_ref[0, 0, t * TOP_K + j]
        src = ys_hbm.at[pl.ds(pl.multiple_of(d * ROW_SUBLANES, ROW_SUBLANES), ROW_SUBLANES)]
        dst = buf.at[sl, j, pl.ds(pl.multiple_of(t * ROW_SUBLANES, ROW_SUBLANES), ROW_SUBLANES)]
        return pltpu.make_async_copy(src, dst, sems.at[sl])

    def start_tile(dest_ref, sl):
        def body(t, c):
            for j in range(TOP_K):
                row_copy(dest_ref, t, j, sl).start()
            return c
        lax.fori_loop(0, td, body, 0)

    @pl.when(step == 0)
    def _():
        start_tile(dcur_ref, slot)

    @pl.when(step + 1 < pl.num_programs(0))
    def _():
        start_tile(dnext_ref, 1 - slot)

    def wait(t, c):
        for j in range(TOP_K):
            row_copy(dcur_ref, t, j, slot).wait()
        return c
    lax.fori_loop(0, td, wait, 0)

    gates = gate_ref[...]
    cols = []
    for s in range(ROW_SUBLANES):
        c = gates[:, 0:1] * buf[slot, 0, pl.ds(s, td, stride=ROW_SUBLANES), :]
        for j in range(1, TOP_K):
            c = c + gates[:, j:j + 1] * buf[slot, j, pl.ds(s, td, stride=ROW_SUBLANES), :]
        cols.append(c)
    y = dn_alpha * x_ref[...] + jnp.concatenate(cols, axis=-1)
    out_ref[...] = _layer_norm(y, lg_ref[...], lb_ref[...])


def _combine_ln(ys, dest, gates, x2, ln_g, ln_b, dn_alpha):
    n_tok, d = x2.shape
    td = DMA_TILE
    n_tiles = n_tok // td
    const = lambda i: (0, 0)
    dest3 = dest.reshape(n_tiles, 1, td * TOP_K)
    return pl.pallas_call(
        functools.partial(_combine_kernel, dn_alpha=dn_alpha),
        grid=(n_tiles,),
        in_specs=[
            pl.BlockSpec((1, 1, td * TOP_K), lambda i: (i, 0, 0), memory_space=pltpu.SMEM),
            pl.BlockSpec((1, 1, td * TOP_K), lambda i: (jnp.minimum(i + 1, n_tiles - 1), 0, 0),
                         memory_space=pltpu.SMEM),
            pl.BlockSpec(memory_space=pl.ANY),
            pl.BlockSpec((td, d), lambda i: (i, 0)),
            pl.BlockSpec((td, LANES), lambda i: (i, 0)),
            pl.BlockSpec((1, d), const),
            pl.BlockSpec((1, d), const),
        ],
        out_specs=pl.BlockSpec((td, d), lambda i: (i, 0)),
        out_shape=jax.ShapeDtypeStruct((n_tok, d), F32),
        scratch_shapes=[pltpu.VMEM((2, TOP_K, td * ROW_SUBLANES, LANES), F32),
                        pltpu.SemaphoreType.DMA((2,))],
        compiler_params=_cparams("arbitrary"),
        name="moe_combine_ln",
    )(dest3, dest3, ys, x2, gates, ln_g.reshape(1, d), ln_b.reshape(1, d))


def _w1_prep_kernel(w_ref, p_ref, out_ref):
    f = w_ref.shape[2] // 2
    width = p_ref.shape[0]
    for c in range(2 * f // width):
        t = _dot(w_ref[0, :, c * width:(c + 1) * width].astype(BF16), p_ref[...])
        out_ref[0, :, c * (width // 2):(c + 1) * (width // 2)] = t[:, :width // 2].astype(BF16)
        out_ref[0, :, f + c * (width // 2):f + (c + 1) * (width // 2)] = t[:, width // 2:].astype(BF16)


def _w1_prep(w1):
    n_exp, d, f2 = w1.shape
    width = 2 * LANES
    rows = ROW_TILE
    r = lax.broadcasted_iota(jnp.int32, (width, width), 0)
    c = lax.broadcasted_iota(jnp.int32, (width, width), 1)
    perm = (c == (r % 2) * (width // 2) + r // 2).astype(BF16)
    return pl.pallas_call(
        _w1_prep_kernel,
        grid=(n_exp, d // rows),
        in_specs=[
            pl.BlockSpec((1, rows, f2), lambda e, i: (e, i, 0)),
            pl.BlockSpec((width, width), lambda e, i: (0, 0)),
        ],
        out_specs=pl.BlockSpec((1, rows, f2), lambda e, i: (e, i, 0)),
        out_shape=jax.ShapeDtypeStruct((n_exp, d, f2), BF16),
        compiler_params=_cparams("parallel", "parallel"),
        name="moe_w1_prep",
    )(w1, perm)


def _moe_layer(x2, router_w, router_b, w1, b1, w2, b2, ln_g, ln_b, dn_alpha):
    n_tok, d = x2.shape
    n_exp, f = w2.shape[0], w2.shape[1]
    blk = MOE_BLOCK
    n_blocks = -(-(n_tok * TOP_K) // blk) + n_exp
    n_rows = n_blocks * blk

    idx, gates, counts = _router(x2, router_w, router_b)

    cnt = counts[0, :n_exp].astype(jnp.int32)
    padded = (cnt + blk - 1) // blk * blk
    pends = jnp.cumsum(padded)
    pstarts = pends - padded
    block_start = jnp.arange(n_blocks, dtype=jnp.int32) * blk
    block_e = jnp.minimum(jnp.sum((pends[None, :] <= block_start[:, None]).astype(jnp.int32), axis=1),
                          n_exp - 1).astype(jnp.int32)
    n_used = (pends[-1:] // blk).astype(jnp.int32)
    dest = (pstarts[idx[:, :TOP_K]] + idx[:, TOP_K:2 * TOP_K]).astype(jnp.int32)

    b1_l = jnp.concatenate([b1[:, 0::2], b1[:, 1::2]], axis=-1)

    xs = _dispatch(x2, dest, n_rows)
    ys = _experts(xs, block_e, n_used, _w1_prep(w1), b1_l, w2.astype(BF16), b2)
    return _combine_ln(ys, dest, gates, x2, ln_g, ln_b, dn_alpha)


def _rope_lane_tables(positions):
    half = ROPE_DIM // 2
    inv_freq = ROPE_THETA ** (-jnp.arange(0, ROPE_DIM, 2, dtype=F32) / ROPE_DIM)
    ang = positions.astype(F32)[..., None] * inv_freq
    cos, sin = jnp.cos(ang), jnp.sin(ang)
    z = jnp.zeros_like(cos)
    assert 4 * half == LANES
    return (jnp.concatenate([cos, z, cos, z], axis=-1),
            jnp.concatenate([-sin, z, sin, z], axis=-1))


def _rope_lane_columns(w_rope):
    half = ROPE_DIM // 2
    z = jnp.zeros(w_rope.shape[:-1] + (half,), w_rope.dtype)
    return jnp.concatenate([w_rope[..., :half], z, w_rope[..., half:], z], axis=-1)


def kernel(x, positions, ln_g, ln_b, hg_w_in, hg_lb, hg_gnorm, hg_w_o, mla_w_dq, mla_q_norm,
           mla_w_uq, mla_w_o, kv_w_a, kv_norm, kv_w_b, router_w, router_b, moe_w1, moe_b1,
           moe_w2, moe_b2):
    bsz, s_len, d = x.shape
    depth = ln_g.shape[0]
    n_a = hg_w_in.shape[0]
    dn_alpha = (2.0 * depth) ** 0.25
    scale = (NOPE_DIM + ROPE_DIM) ** -0.5

    lb_soft = jax.nn.softmax(hg_lb.astype(F32), axis=0)
    lower_bounds = jnp.cumsum(lb_soft, axis=0) - lb_soft[0]

    q_rank = mla_w_uq.shape[1]
    mla_heads = mla_w_uq.shape[2] // (NOPE_DIM + ROPE_DIM)
    cc = ss = k_full = v_full = None

    for layer in range(depth):
        if layer < n_a:
            x = _hgrn_layer(x, hg_w_in[layer], lower_bounds[layer], hg_gnorm[layer], hg_w_o[layer],
                            ln_g[layer, 0], ln_b[layer, 0], dn_alpha)
        else:
            j = layer - n_a
            if layer == n_a:
                cc, ss = _rope_lane_tables(positions)
                kv_w_a_l = jnp.concatenate(
                    [kv_w_a[:, :KV_RANK], _rope_lane_columns(kv_w_a[:, KV_RANK:])], axis=-1).astype(BF16)
                k_full, v_full = _shared_kv(x, kv_w_a_l, kv_norm, kv_w_b, cc, ss)
            w_uq = mla_w_uq[j].reshape(q_rank, mla_heads, NOPE_DIM + ROPE_DIM)
            w_uq_l = jnp.concatenate(
                [w_uq[..., :NOPE_DIM], _rope_lane_columns(w_uq[..., NOPE_DIM:])],
                axis=-1).reshape(q_rank, mla_heads * QK_DIM).astype(BF16)
            q_full = _mla_queries(x, mla_w_dq[j], mla_q_norm[j], w_uq_l, cc, ss, scale)
            o = _mla_attention(q_full, k_full, v_full)
            x = _proj_residual_ln(o.reshape(bsz * s_len, -1), x.reshape(bsz * s_len, d), mla_w_o[j],
                                  ln_g[layer, 0], ln_b[layer, 0], dn_alpha).reshape(bsz, s_len, d)
        x = _moe_layer(x.reshape(bsz * s_len, d), router_w[layer], router_b[layer], moe_w1[layer],
                       moe_b1[layer], moe_w2[layer], moe_b2[layer], ln_g[layer, 1], ln_b[layer, 1],
                       dn_alpha).reshape(bsz, s_len, d)
    return x
```

```python
import functools

import jax
import jax.numpy as jnp
from jax import lax
from jax.experimental import pallas as pl
from jax.experimental.pallas import tpu as pltpu

F32 = jnp.float32
BF16 = jnp.bfloat16

HG_HEAD_DIM = 128
NOPE_DIM = 128
ROPE_DIM = 64
V_DIM = 128
KV_RANK = 128
ROPE_THETA = 10000.0
TOP_K = 4
SWIGLU_ALPHA = 1.702
SWIGLU_LIMIT = 7.0
LN_EPS = 1e-5
RMS_EPS = 1e-6

LANES = 128
ROW_SUBLANES = 8
QK_DIM = 256
VMEM_LIMIT = 56 * 1024 * 1024

ROW_TILE = 256
HG_GROUP = 128
HG_HEAD_UNROLL = 4
ATT_TQ = 512
MOE_BLOCK = 512
DMA_TILE = 256


def _cparams(*sem):
    return pltpu.CompilerParams(dimension_semantics=sem, vmem_limit_bytes=VMEM_LIMIT)


def _layer_norm(y, g, b):
    mu = jnp.mean(y, axis=-1, keepdims=True)
    d = y - mu
    var = jnp.mean(d * d, axis=-1, keepdims=True)
    return d * lax.rsqrt(var + LN_EPS) * g + b


def _dot(a, b):
    return jnp.dot(a, b, preferred_element_type=F32)


def _dot_nt(a, b):
    return lax.dot_general(a, b, (((1,), (1,)), ((), ())), preferred_element_type=F32)


def _dot_tn(a, b):
    return lax.dot_general(a, b, (((0,), (0,)), ((), ())), preferred_element_type=F32)


def _sigmoid(x):
    return 1.0 / (1.0 + jnp.exp(-x))


def _hgrn_kernel(x_ref, w_in_ref, lb_ref, gn_ref, w_o_ref, lg_ref, lbias_ref, out_ref,
                 q_s, f_s, i_s, g_s, mix_s, st_s, *, dn_alpha):
    ts, d = x_ref.shape[1], x_ref.shape[2]
    n_heads = d // HG_HEAD_DIM
    grp = HG_GROUP

    @pl.when(pl.program_id(1) == 0)
    def _():
        st_s[...] = jnp.zeros_like(st_s)

    x = x_ref[0]
    xb = x.astype(BF16)
    lb = lb_ref[...]

    for sec, dst in enumerate((q_s, f_s, i_s, g_s)):
        p = _dot(xb, w_in_ref[:, sec * d:(sec + 1) * d])
        if sec == 0 or sec == 3:
            p = p * _sigmoid(p)
        elif sec == 1:
            p = lb + (1.0 - lb) * _sigmoid(p)
        for h in range(n_heads):
            dst[h] = p[:, h * HG_HEAD_DIM:(h + 1) * HG_HEAD_DIM]

    row = lax.broadcasted_iota(jnp.int32, (grp, grp), 0)
    col = lax.broadcasted_iota(jnp.int32, (grp, grp), 1)
    n_levels = grp.bit_length() - 1
    pair_masks = [(((row >> lvl) ^ (col >> lvl)) == 1) & (row > col) for lvl in range(n_levels)]

    def head_body(h, carry):
        for r0 in range(0, ts, grp):
            q = q_s[h, r0:r0 + grp, :]
            fg = f_s[h, r0:r0 + grp, :]
            iv = i_s[h, r0:r0 + grp, :].astype(BF16)
            k = 1.0 - fg
            ep, es, et = fg, None, fg
            a = jnp.where(row == col, _dot_nt(q.astype(BF16), k.astype(BF16)), 0.0)
            for lvl in range(n_levels):
                half = 1 << lvl
                kl = k if es is None else k * es
                a = jnp.where(pair_masks[lvl], _dot_nt((q * ep).astype(BF16), kl.astype(BF16)), a)
                if half < ROW_SUBLANES:
                    odd = (row & half) != 0
                    et3 = et.reshape(grp // ROW_SUBLANES, ROW_SUBLANES, LANES)
                    other = jnp.where(odd, pltpu.roll(et3, half, 1).reshape(et.shape),
                                      pltpu.roll(et3, ROW_SUBLANES - half, 1).reshape(et.shape))
                    ep = jnp.where(odd, ep * other, ep)
                    es = jnp.where(odd, 1.0, other) if es is None else jnp.where(odd, es, es * other)
                    et = et * other
                else:
                    ep_p, es_p, et_p = [], [], []
                    for b0 in range(0, grp, 2 * half):
                        lo, mid, hi = b0, b0 + half, b0 + 2 * half
                        tot = et[lo:mid] * et[mid:hi]
                        ep_p += [ep[lo:mid], ep[mid:hi] * et[lo:mid]]
                        es_p += [es[lo:mid] * et[mid:hi], es[mid:hi]]
                        et_p += [tot, tot]
                    ep = jnp.concatenate(ep_p, axis=0)
                    es = jnp.concatenate(es_p, axis=0)
                    et = jnp.concatenate(et_p, axis=0)
            st = st_s[h]
            o = _dot(a.astype(BF16), iv) + _dot_nt((q * ep).astype(BF16), st.astype(BF16))
            st_s[h] = st * et[0:1, :] + _dot_tn(iv, (k * es).astype(BF16))
            ms = jnp.mean(o * o, axis=-1, keepdims=True)
            y = o * lax.rsqrt(ms + RMS_EPS) * gn_ref[...] * g_s[h, r0:r0 + grp, :]
            mix_s[h, r0:r0 + grp, :] = y.astype(BF16)
        return carry

    lax.fori_loop(0, n_heads, head_body, 0, unroll=HG_HEAD_UNROLL)

    acc = dn_alpha * x
    for h in range(0, n_heads, 2):
        acc = acc + _dot(jnp.concatenate([mix_s[h], mix_s[h + 1]], axis=-1), w_o_ref[h // 2])
    out_ref[0] = _layer_norm(acc, lg_ref[...], lbias_ref[...])


def _hgrn_layer(x, w_in, lb, gnorm, w_o, ln_g, ln_b, dn_alpha):
    bsz, s_len, d = x.shape
    n_heads = d // HG_HEAD_DIM
    ts = ROW_TILE
    const2 = lambda b, s: (0, 0)
    return pl.pallas_call(
        functools.partial(_hgrn_kernel, dn_alpha=dn_alpha),
        grid=(bsz, s_len // ts),
        in_specs=[
            pl.BlockSpec((1, ts, d), lambda b, s: (b, s, 0)),
            pl.BlockSpec((d, 4 * d), const2),
            pl.BlockSpec((1, d), const2),
            pl.BlockSpec((1, HG_HEAD_DIM), const2),
            pl.BlockSpec((n_heads // 2, 2 * HG_HEAD_DIM, d), lambda b, s: (0, 0, 0)),
            pl.BlockSpec((1, d), const2),
            pl.BlockSpec((1, d), const2),
        ],
        out_specs=pl.BlockSpec((1, ts, d), lambda b, s: (b, s, 0)),
        out_shape=jax.ShapeDtypeStruct((bsz, s_len, d), F32),
        scratch_shapes=[
            pltpu.VMEM((n_heads, ts, HG_HEAD_DIM), F32),
            pltpu.VMEM((n_heads, ts, HG_HEAD_DIM), F32),
            pltpu.VMEM((n_heads, ts, HG_HEAD_DIM), F32),
            pltpu.VMEM((n_heads, ts, HG_HEAD_DIM), F32),
            pltpu.VMEM((n_heads, ts, HG_HEAD_DIM), BF16),
            pltpu.VMEM((n_heads, HG_HEAD_DIM, HG_HEAD_DIM), F32),
        ],
        compiler_params=_cparams("parallel", "arbitrary"),
        name="hgrn2_layer",
    )(x, w_in.astype(BF16), lb.reshape(1, d), gnorm.reshape(1, HG_HEAD_DIM),
      w_o.astype(BF16).reshape(n_heads // 2, 2 * HG_HEAD_DIM, d), ln_g.reshape(1, d), ln_b.reshape(1, d))


def _rope_lanes(t, cc, ss):
    return t * cc + pltpu.roll(t, LANES // 2, 1) * ss


def _kv_kernel(x_ref, wa_ref, kvn_ref, wb_ref, cc_ref, ss_ref, k_ref, v_ref):
    n_heads = k_ref.shape[1]
    xb = x_ref[0].astype(BF16)
    ckr = _dot(xb, wa_ref[...])
    c = ckr[:, :KV_RANK]
    c = c * lax.rsqrt(jnp.mean(c * c, axis=-1, keepdims=True) + RMS_EPS) * kvn_ref[...]
    kr = _rope_lanes(ckr[:, KV_RANK:], cc_ref[0], ss_ref[0]).astype(BF16)
    cb = c.astype(BF16)
    for h in range(n_heads):
        kv = _dot(cb, wb_ref[:, h * (NOPE_DIM + V_DIM):(h + 1) * (NOPE_DIM + V_DIM)])
        k_ref[0, h, :, :NOPE_DIM] = kv[:, :NOPE_DIM].astype(BF16)
        k_ref[0, h, :, NOPE_DIM:] = kr
        v_ref[0, h] = kv[:, NOPE_DIM:].astype(BF16)


def _shared_kv(x, kv_w_a_l, kv_norm, kv_w_b, cc, ss):
    bsz, s_len, d = x.shape
    n_heads = kv_w_b.shape[1] // (NOPE_DIM + V_DIM)
    ts = ROW_TILE
    const2 = lambda b, s: (0, 0)
    return pl.pallas_call(
        _kv_kernel,
        grid=(bsz, s_len // ts),
        in_specs=[
            pl.BlockSpec((1, ts, d), lambda b, s: (b, s, 0)),
            pl.BlockSpec((d, KV_RANK + LANES), const2),
            pl.BlockSpec((1, KV_RANK), const2),
            pl.BlockSpec((KV_RANK, n_heads * (NOPE_DIM + V_DIM)), const2),
            pl.BlockSpec((1, ts, LANES), lambda b, s: (b, s, 0)),
            pl.BlockSpec((1, ts, LANES), lambda b, s: (b, s, 0)),
        ],
        out_specs=[
            pl.BlockSpec((1, n_heads, ts, QK_DIM), lambda b, s: (b, 0, s, 0)),
            pl.BlockSpec((1, n_heads, ts, V_DIM), lambda b, s: (b, 0, s, 0)),
        ],
        out_shape=[
            jax.ShapeDtypeStruct((bsz, n_heads, s_len, QK_DIM), BF16),
            jax.ShapeDtypeStruct((bsz, n_heads, s_len, V_DIM), BF16),
        ],
        compiler_params=_cparams("parallel", "parallel"),
        name="mla_shared_kv",
    )(x, kv_w_a_l, kv_norm.reshape(1, KV_RANK), kv_w_b.astype(BF16), cc, ss)


def _q_kernel(x_ref, wdq_ref, qn_ref, wuq_ref, cc_ref, ss_ref, q_ref, *, scale):
    n_heads = q_ref.shape[1]
    xb = x_ref[0].astype(BF16)
    c = _dot(xb, wdq_ref[...])
    c = c * lax.rsqrt(jnp.mean(c * c, axis=-1, keepdims=True) + RMS_EPS) * qn_ref[...]
    cb = c.astype(BF16)
    cc = cc_ref[0] * scale
    ss = ss_ref[0] * scale
    for h in range(n_heads):
        qh = _dot(cb, wuq_ref[:, h * QK_DIM:(h + 1) * QK_DIM])
        q_ref[0, h, :, :NOPE_DIM] = (qh[:, :NOPE_DIM] * scale).astype(BF16)
        q_ref[0, h, :, NOPE_DIM:] = _rope_lanes(qh[:, NOPE_DIM:], cc, ss).astype(BF16)


def _mla_queries(x, w_dq, q_norm, w_uq_l, cc, ss, scale):
    bsz, s_len, d = x.shape
    q_rank = w_dq.shape[1]
    n_heads = w_uq_l.shape[1] // QK_DIM
    ts = ROW_TILE
    const2 = lambda b, s: (0, 0)
    return pl.pallas_call(
        functools.partial(_q_kernel, scale=scale),
        grid=(bsz, s_len // ts),
        in_specs=[
            pl.BlockSpec((1, ts, d), lambda b, s: (b, s, 0)),
            pl.BlockSpec((d, q_rank), const2),
            pl.BlockSpec((1, q_rank), const2),
            pl.BlockSpec((q_rank, n_heads * QK_DIM), const2),
            pl.BlockSpec((1, ts, LANES), lambda b, s: (b, s, 0)),
            pl.BlockSpec((1, ts, LANES), lambda b, s: (b, s, 0)),
        ],
        out_specs=pl.BlockSpec((1, n_heads, ts, QK_DIM), lambda b, s: (b, 0, s, 0)),
        out_shape=jax.ShapeDtypeStruct((bsz, n_heads, s_len, QK_DIM), BF16),
        compiler_params=_cparams("parallel", "parallel"),
        name="mla_queries",
    )(x, w_dq.astype(BF16), q_norm.reshape(1, q_rank), w_uq_l, cc, ss)


def _attn_kernel(q_ref, k_ref, v_ref, o_ref):
    s_len = q_ref.shape[2]
    tq = min(ATT_TQ, s_len)
    row = lax.broadcasted_iota(jnp.int32, (tq, tq), 0)
    col = lax.broadcasted_iota(jnp.int32, (tq, tq), 1)
    for qi in range(s_len // tq):
        q = q_ref[0, 0, qi * tq:(qi + 1) * tq, :]
        m = jnp.full((tq, 1), -jnp.inf, F32)
        l = jnp.zeros((tq, 1), F32)
        acc = jnp.zeros((tq, V_DIM), F32)
        for kj in range(qi + 1):
            s = _dot_nt(q, k_ref[0, 0, kj * tq:(kj + 1) * tq, :])
            if kj == qi:
                s = jnp.where(col <= row, s, -jnp.inf)
            m_new = jnp.maximum(m, jnp.max(s, axis=-1, keepdims=True))
            p = jnp.exp(s - m_new)
            corr = jnp.exp(m - m_new)
            l = corr * l + jnp.sum(p, axis=-1, keepdims=True)
            acc = corr * acc + _dot(p.astype(BF16), v_ref[0, 0, kj * tq:(kj + 1) * tq, :])
            m = m_new
        o_ref[0, qi * tq:(qi + 1) * tq, :] = (acc / l).astype(o_ref.dtype)


def _mla_attention(q, k, v):
    bsz, n_heads, s_len, _ = q.shape
    return pl.pallas_call(
        _attn_kernel,
        grid=(bsz, n_heads),
        in_specs=[
            pl.BlockSpec((1, 1, s_len, QK_DIM), lambda b, h: (b, h, 0, 0)),
            pl.BlockSpec((1, 1, s_len, QK_DIM), lambda b, h: (b, h, 0, 0)),
            pl.BlockSpec((1, 1, s_len, V_DIM), lambda b, h: (b, h, 0, 0)),
        ],
        out_specs=pl.BlockSpec((1, s_len, V_DIM), lambda b, h: (b, 0, h)),
        out_shape=jax.ShapeDtypeStruct((bsz, s_len, n_heads * V_DIM), BF16),
        compiler_params=_cparams("parallel", "parallel"),
        name="mla_attention",
    )(q, k, v)


def _proj_ln_kernel(o_ref, x_ref, w_ref, lg_ref, lb_ref, out_ref, *, dn_alpha):
    y = dn_alpha * x_ref[...] + _dot(o_ref[...], w_ref[...])
    out_ref[...] = _layer_norm(y, lg_ref[...], lb_ref[...])


def _proj_residual_ln(o2, x2, w_o, ln_g, ln_b, dn_alpha):
    n_tok, d = x2.shape
    kdim = o2.shape[1]
    ts = ROW_TILE
    const = lambda i: (0, 0)
    return pl.pallas_call(
        functools.partial(_proj_ln_kernel, dn_alpha=dn_alpha),
        grid=(n_tok // ts,),
        in_specs=[
            pl.BlockSpec((ts, kdim), lambda i: (i, 0)),
            pl.BlockSpec((ts, d), lambda i: (i, 0)),
            pl.BlockSpec((kdim, d), const),
            pl.BlockSpec((1, d), const),
            pl.BlockSpec((1, d), const),
        ],
        out_specs=pl.BlockSpec((ts, d), lambda i: (i, 0)),
        out_shape=jax.ShapeDtypeStruct((n_tok, d), F32),
        compiler_params=_cparams("parallel"),
        name="mla_out_proj_ln",
    )(o2, x2, w_o.astype(BF16), ln_g.reshape(1, d), ln_b.reshape(1, d))


def _router_kernel(x_ref, whi_ref, wlo_ref, b_ref, idx_ref, gate_ref, cnt_ref, carry_s):
    tr = x_ref.shape[0]

    @pl.when(pl.program_id(0) == 0)
    def _():
        carry_s[...] = jnp.zeros_like(carry_s)

    x = x_ref[...]
    x_hi = x.astype(BF16)
    x_lo = (x - x_hi.astype(F32)).astype(BF16)
    logits = (_dot(x_hi, whi_ref[...]) + (_dot(x_lo, whi_ref[...]) + _dot(x_hi, wlo_ref[...]))
              + b_ref[...])
    lane = lax.broadcasted_iota(jnp.int32, (tr, LANES), 1)
    work = logits
    sel = jnp.zeros((tr, LANES), F32)
    ids, vals = [], []
    for _ in range(TOP_K):
        mx = jnp.max(work, axis=-1, keepdims=True)
        idx = jnp.min(jnp.where(work == mx, lane, LANES), axis=-1, keepdims=True)
        hit = lane == idx
        ids.append(idx)
        vals.append(mx)
        sel = jnp.where(hit, 1.0, sel)
        work = jnp.where(hit, -jnp.inf, work)
    exps = [jnp.exp(v - vals[0]) for v in vals]
    denom = exps[0] + exps[1] + exps[2] + exps[3]

    r = lax.broadcasted_iota(jnp.int32, (tr, tr), 0)
    c = lax.broadcasted_iota(jnp.int32, (tr, tr), 1)
    before = _dot(jnp.where(c < r, 1.0, 0.0).astype(BF16), sel.astype(BF16)) + carry_s[...]
    carry_s[...] = carry_s[...] + jnp.sum(sel, axis=0, keepdims=True)
    cnt_ref[...] = carry_s[...]

    idx_out = jnp.zeros((tr, LANES), jnp.int32)
    gate_out = jnp.zeros((tr, LANES), F32)
    for j in range(TOP_K):
        rank = jnp.sum(jnp.where(lane == ids[j], before, 0.0), axis=-1, keepdims=True)
        idx_out = jnp.where(lane == j, ids[j], idx_out)
        idx_out = jnp.where(lane == TOP_K + j, rank.astype(jnp.int32), idx_out)
        gate_out = jnp.where(lane == j, exps[j] / denom, gate_out)
    idx_ref[...] = idx_out
    gate_ref[...] = gate_out


def _router(x2, router_w, router_b):
    n_tok, d = x2.shape
    n_exp = router_w.shape[1]
    tr = ROW_TILE
    w = jnp.zeros((d, LANES), F32).at[:, :n_exp].set(router_w)
    w_hi = w.astype(BF16)
    w_lo = (w - w_hi.astype(F32)).astype(BF16)
    b = jnp.full((1, LANES), -jnp.inf, F32).at[0, :n_exp].set(router_b)
    const = lambda i: (0, 0)
    return pl.pallas_call(
        _router_kernel,
        grid=(n_tok // tr,),
        in_specs=[
            pl.BlockSpec((tr, d), lambda i: (i, 0)),
            pl.BlockSpec((d, LANES), const),
            pl.BlockSpec((d, LANES), const),
            pl.BlockSpec((1, LANES), const),
        ],
        out_specs=[
            pl.BlockSpec((tr, LANES), lambda i: (i, 0)),
            pl.BlockSpec((tr, LANES), lambda i: (i, 0)),
            pl.BlockSpec((1, LANES), const),
        ],
        out_shape=[
            jax.ShapeDtypeStruct((n_tok, LANES), jnp.int32),
            jax.ShapeDtypeStruct((n_tok, LANES), F32),
            jax.ShapeDtypeStruct((1, LANES), F32),
        ],
        scratch_shapes=[pltpu.VMEM((1, LANES), F32)],
        compiler_params=_cparams("arbitrary"),
        name="moe_router",
    )(x2, w_hi, w_lo, b)


def _rows_to_tiles(tile_ref, val):
    for s in range(ROW_SUBLANES):
        tile_ref[pl.ds(s, val.shape[0], stride=ROW_SUBLANES), :] = val[:, s * LANES:(s + 1) * LANES]


def _dispatch_kernel(dest_ref, x_ref, xs_hbm, stage, sems):
    td = x_ref.shape[0]
    step = pl.program_id(0)
    slot = step % 2

    def row_copy(t, j, sl):
        src = stage.at[sl, pl.ds(pl.multiple_of(t * ROW_SUBLANES, ROW_SUBLANES), ROW_SUBLANES)]
        d = dest_ref[0, 0, t * TOP_K + j]
        dst = xs_hbm.at[pl.ds(pl.multiple_of(d * ROW_SUBLANES, ROW_SUBLANES), ROW_SUBLANES)]
        return pltpu.make_async_copy(src, dst, sems.at[sl])

    def wait_slot(sl):
        def body(t, c):
            for j in range(TOP_K):
                row_copy(t, j, sl).wait()
            return c
        lax.fori_loop(0, td, body, 0)

    @pl.when(step >= 2)
    def _():
        wait_slot(slot)

    _rows_to_tiles(stage.at[slot], x_ref[...])

    def start(t, c):
        for j in range(TOP_K):
            row_copy(t, j, slot).start(priority=j % 2)
        return c
    lax.fori_loop(0, td, start, 0)

    @pl.when(step == pl.num_programs(0) - 1)
    def _():
        wait_slot(slot)

        @pl.when(step >= 1)
        def _():
            wait_slot(1 - slot)


def _dispatch(x2, dest, n_rows):
    n_tok, d = x2.shape
    assert d == ROW_SUBLANES * LANES
    td = DMA_TILE
    n_tiles = n_tok // td
    return pl.pallas_call(
        _dispatch_kernel,
        grid=(n_tiles,),
        in_specs=[
            pl.BlockSpec((1, 1, td * TOP_K), lambda i: (i, 0, 0), memory_space=pltpu.SMEM),
            pl.BlockSpec((td, d), lambda i: (i, 0)),
        ],
        out_specs=pl.BlockSpec(memory_space=pl.ANY),
        out_shape=jax.ShapeDtypeStruct((n_rows * ROW_SUBLANES, LANES), F32),
        scratch_shapes=[pltpu.VMEM((2, td * ROW_SUBLANES, LANES), F32), pltpu.SemaphoreType.DMA((2,))],
        compiler_params=_cparams("arbitrary"),
        name="moe_dispatch",
    )(dest.reshape(n_tiles, 1, td * TOP_K), x2)


def _expert_kernel(be_ref, nu_ref, xs_ref, w1_ref, b1_ref, w2_ref, b2_ref, ys_ref, xb_s):
    @pl.when(pl.program_id(0) < nu_ref[0])
    def _():
        f = w2_ref.shape[1]
        blk = xb_s.shape[0]
        for s in range(ROW_SUBLANES):
            xb_s[:, s * LANES:(s + 1) * LANES] = xs_ref[pl.ds(s, blk, stride=ROW_SUBLANES), :].astype(BF16)
        h = _dot(xb_s[...], w1_ref[0]) + b1_ref[0]
        glu = jnp.minimum(h[:, :f], SWIGLU_LIMIT)
        lin = jnp.clip(h[:, f:], -SWIGLU_LIMIT, SWIGLU_LIMIT)
        a = glu * _sigmoid(SWIGLU_ALPHA * glu) * (lin + 1.0)
        _rows_to_tiles(ys_ref, _dot(a.astype(BF16), w2_ref[0]) + b2_ref[0])


def _experts(xs, block_e, n_used, w1, b1, w2, b2):
    n_exp, f, d = w2.shape
    blk = MOE_BLOCK
    n_blocks = xs.shape[0] // (blk * ROW_SUBLANES)
    row_map = lambda i, be, nu: (jnp.minimum(i, nu[0] - 1), 0)
    exp_map = lambda i, be, nu: (be[i], 0, 0)
    grid_spec = pltpu.PrefetchScalarGridSpec(
        num_scalar_prefetch=2,
        grid=(n_blocks,),
        in_specs=[
            pl.BlockSpec((blk * ROW_SUBLANES, LANES), row_map),
            pl.BlockSpec((1, d, 2 * f), exp_map),
            pl.BlockSpec((1, 1, 2 * f), exp_map),
            pl.BlockSpec((1, f, d), exp_map),
            pl.BlockSpec((1, 1, d), exp_map),
        ],
        out_specs=pl.BlockSpec((blk * ROW_SUBLANES, LANES), row_map),
        scratch_shapes=[pltpu.VMEM((blk, d), BF16)],
    )
    return pl.pallas_call(
        _expert_kernel,
        grid_spec=grid_spec,
        out_shape=jax.ShapeDtypeStruct(xs.shape, F32),
        compiler_params=_cparams("arbitrary"),
        name="moe_experts",
    )(block_e, n_used, xs, w1, b1.reshape(n_exp, 1, 2 * f), w2, b2.reshape(n_exp, 1, d))


def _combine_kernel(dcur_ref, dnext_ref, ys_hbm, x_ref, gate_ref, lg_ref, lb_ref, out_ref, buf, sems,
                    *, dn_alpha):
    td = x_ref.shape[0]
    step = pl.program_id(0)
    slot = step % 2

    def row_copy(dest_ref, t, j, sl):
        d = dest_ref[0, 0, t * TOP_K + j]
        src = ys_hbm.at[pl.ds(pl.multiple_of(d * ROW_SUBLANES, ROW_SUBLANES), ROW_SUBLANES)]
        dst = buf.at[sl, j, pl.ds(pl.multiple_of(t * ROW_SUBLANES, ROW_SUBLANES), ROW_SUBLANES)]
        return pltpu.make_async_copy(src, dst, sems.at[sl])

    def start_tile(dest_ref, sl):
        def body(t, c):
            for j in range(TOP_K):
                row_copy(dest_ref, t, j, sl).start(priority=j % 2)
            return c
        lax.fori_loop(0, td, body, 0)

    @pl.when(step == 0)
    def _():
        start_tile(dcur_ref, slot)

    @pl.when(step + 1 < pl.num_programs(0))
    def _():
        start_tile(dnext_ref, 1 - slot)

    def wait(t, c):
        for j in range(TOP_K):
            row_copy(dcur_ref, t, j, slot).wait()
        return c
    lax.fori_loop(0, td, wait, 0)

    gates = gate_ref[...]
    cols = []
    for s in range(ROW_SUBLANES):
        c = gates[:, 0:1] * buf[slot, 0, pl.ds(s, td, stride=ROW_SUBLANES), :]
        for j in range(1, TOP_K):
            c = c + gates[:, j:j + 1] * buf[slot, j, pl.ds(s, td, stride=ROW_SUBLANES), :]
        cols.append(c)
    y = dn_alpha * x_ref[...] + jnp.concatenate(cols, axis=-1)
    out_ref[...] = _layer_norm(y, lg_ref[...], lb_ref[...])


def _combine_ln(ys, dest, gates, x2, ln_g, ln_b, dn_alpha):
    n_tok, d = x2.shape
    td = DMA_TILE
    n_tiles = n_tok // td
    const = lambda i: (0, 0)
    dest3 = dest.reshape(n_tiles, 1, td * TOP_K)
    return pl.pallas_call(
        functools.partial(_combine_kernel, dn_alpha=dn_alpha),
        grid=(n_tiles,),
        in_specs=[
            pl.BlockSpec((1, 1, td * TOP_K), lambda i: (i, 0, 0), memory_space=pltpu.SMEM),
            pl.BlockSpec((1, 1, td * TOP_K), lambda i: (jnp.minimum(i + 1, n_tiles - 1), 0, 0),
                         memory_space=pltpu.SMEM),
            pl.BlockSpec(memory_space=pl.ANY),
            pl.BlockSpec((td, d), lambda i: (i, 0)),
            pl.BlockSpec((td, LANES), lambda i: (i, 0)),
            pl.BlockSpec((1, d), const),
            pl.BlockSpec((1, d), const),
        ],
        out_specs=pl.BlockSpec((td, d), lambda i: (i, 0)),
        out_shape=jax.ShapeDtypeStruct((n_tok, d), F32),
        scratch_shapes=[pltpu.VMEM((2, TOP_K, td * ROW_SUBLANES, LANES), F32),
                        pltpu.SemaphoreType.DMA((2,))],
        compiler_params=_cparams("arbitrary"),
        name="moe_combine_ln",
    )(dest3, dest3, ys, x2, gates, ln_g.reshape(1, d), ln_b.reshape(1, d))


def _w1_prep_kernel(w_ref, p_ref, out_ref):
    f = w_ref.shape[2] // 2
    width = p_ref.shape[0]
    for c in range(2 * f // width):
        t = _dot(w_ref[0, :, c * width:(c + 1) * width].astype(BF16), p_ref[...])
        out_ref[0, :, c * (width // 2):(c + 1) * (width // 2)] = t[:, :width // 2].astype(BF16)
        out_ref[0, :, f + c * (width // 2):f + (c + 1) * (width // 2)] = t[:, width // 2:].astype(BF16)


def _w1_prep(w1):
    n_exp, d, f2 = w1.shape
    width = 2 * LANES
    rows = ROW_TILE
    r = lax.broadcasted_iota(jnp.int32, (width, width), 0)
    c = lax.broadcasted_iota(jnp.int32, (width, width), 1)
    perm = (c == (r % 2) * (width // 2) + r // 2).astype(BF16)
    return pl.pallas_call(
        _w1_prep_kernel,
        grid=(n_exp, d // rows),
        in_specs=[
            pl.BlockSpec((1, rows, f2), lambda e, i: (e, i, 0)),
            pl.BlockSpec((width, width), lambda e, i: (0, 0)),
        ],
        out_specs=pl.BlockSpec((1, rows, f2), lambda e, i: (e, i, 0)),
        out_shape=jax.ShapeDtypeStruct((n_exp, d, f2), BF16),
        compiler_params=_cparams("parallel", "parallel"),
        name="moe_w1_prep",
    )(w1, perm)


def _moe_layer(x2, router_w, router_b, w1, b1, w2, b2, ln_g, ln_b, dn_alpha):
    n_tok, d = x2.shape
    n_exp, f = w2.shape[0], w2.shape[1]
    blk = MOE_BLOCK
    n_blocks = -(-(n_tok * TOP_K) // blk) + n_exp
    n_rows = n_blocks * blk

    idx, gates, counts = _router(x2, router_w, router_b)

    cnt = counts[0, :n_exp].astype(jnp.int32)
    padded = (cnt + blk - 1) // blk * blk
    pends = jnp.cumsum(padded)
    pstarts = pends - padded
    block_start = jnp.arange(n_blocks, dtype=jnp.int32) * blk
    block_e = jnp.minimum(jnp.sum((pends[None, :] <= block_start[:, None]).astype(jnp.int32), axis=1),
                          n_exp - 1).astype(jnp.int32)
    n_used = (pends[-1:] // blk).astype(jnp.int32)
    dest = (pstarts[idx[:, :TOP_K]] + idx[:, TOP_K:2 * TOP_K]).astype(jnp.int32)

    b1_l = jnp.concatenate([b1[:, 0::2], b1[:, 1::2]], axis=-1)

    xs = _dispatch(x2, dest, n_rows)
    ys = _experts(xs, block_e, n_used, _w1_prep(w1), b1_l, w2.astype(BF16), b2)
    return _combine_ln(ys, dest, gates, x2, ln_g, ln_b, dn_alpha)


def _rope_lane_tables(positions):
    half = ROPE_DIM // 2
    inv_freq = ROPE_THETA ** (-jnp.arange(0, ROPE_DIM, 2, dtype=F32) / ROPE_DIM)
    ang = positions.astype(F32)[..., None] * inv_freq
    cos, sin = jnp.cos(ang), jnp.sin(ang)
    z = jnp.zeros_like(cos)
    assert 4 * half == LANES
    return (jnp.concatenate([cos, z, cos, z], axis=-1),
            jnp.concatenate([-sin, z, sin, z], axis=-1))


def _rope_lane_columns(w_rope):
    half = ROPE_DIM // 2
    z = jnp.zeros(w_rope.shape[:-1] + (half,), w_rope.dtype)
    return jnp.concatenate([w_rope[..., :half], z, w_rope[..., half:], z], axis=-1)


def kernel(x, positions, ln_g, ln_b, hg_w_in, hg_lb, hg_gnorm, hg_w_o, mla_w_dq, mla_q_norm,
           mla_w_uq, mla_w_o, kv_w_a, kv_norm, kv_w_b, router_w, router_b, moe_w1, moe_b1,
           moe_w2, moe_b2):
    bsz, s_len, d = x.shape
    depth = ln_g.shape[0]
    n_a = hg_w_in.shape[0]
    dn_alpha = (2.0 * depth) ** 0.25
    scale = (NOPE_DIM + ROPE_DIM) ** -0.5

    lb_soft = jax.nn.softmax(hg_lb.astype(F32), axis=0)
    lower_bounds = jnp.cumsum(lb_soft, axis=0) - lb_soft[0]

    q_rank = mla_w_uq.shape[1]
    mla_heads = mla_w_uq.shape[2] // (NOPE_DIM + ROPE_DIM)
    cc = ss = k_full = v_full = None

    for layer in range(depth):
        if layer < n_a:
            x = _hgrn_layer(x, hg_w_in[layer], lower_bounds[layer], hg_gnorm[layer], hg_w_o[layer],
                            ln_g[layer, 0], ln_b[layer, 0], dn_alpha)
        else:
            j = layer - n_a
            if layer == n_a:
                cc, ss = _rope_lane_tables(positions)
                kv_w_a_l = jnp.concatenate(
                    [kv_w_a[:, :KV_RANK], _rope_lane_columns(kv_w_a[:, KV_RANK:])], axis=-1).astype(BF16)
                k_full, v_full = _shared_kv(x, kv_w_a_l, kv_norm, kv_w_b, cc, ss)
            w_uq = mla_w_uq[j].reshape(q_rank, mla_heads, NOPE_DIM + ROPE_DIM)
            w_uq_l = jnp.concatenate(
                [w_uq[..., :NOPE_DIM], _rope_lane_columns(w_uq[..., NOPE_DIM:])],
                axis=-1).reshape(q_rank, mla_heads * QK_DIM).astype(BF16)
            q_full = _mla_queries(x, mla_w_dq[j], mla_q_norm[j], w_uq_l, cc, ss, scale)
            o = _mla_attention(q_full, k_full, v_full)
            x = _proj_residual_ln(o.reshape(bsz * s_len, -1), x.reshape(bsz * s_len, d), mla_w_o[j],
                                  ln_g[layer, 0], ln_b[layer, 0], dn_alpha).reshape(bsz, s_len, d)
        x = _moe_layer(x.reshape(bsz * s_len, d), router_w[layer], router_b[layer], moe_w1[layer],
                       moe_b1[layer], moe_w2[layer], moe_b2[layer], ln_g[layer, 1], ln_b[layer, 1],
                       dn_alpha).reshape(bsz, s_len, d)
    return x
```

```python
import functools

import jax
import jax.numpy as jnp
from jax import lax
from jax.experimental import pallas as pl
from jax.experimental.pallas import tpu as pltpu

F32 = jnp.float32
BF16 = jnp.bfloat16

HG_HEAD_DIM = 128
NOPE_DIM = 128
ROPE_DIM = 64
V_DIM = 128
KV_RANK = 128
ROPE_THETA = 10000.0
TOP_K = 4
SWIGLU_ALPHA = 1.702
SWIGLU_LIMIT = 7.0
LN_EPS = 1e-5
RMS_EPS = 1e-6

LANES = 128
ROW_SUBLANES = 8
QK_DIM = 256
VMEM_LIMIT = 56 * 1024 * 1024

ROW_TILE = 256
HG_GROUP = 128
HG_HEAD_UNROLL = 4
ATT_TQ = 512
MOE_BLOCK = 512
SEG_TILE = 512
SEG_ALIGN = ROW_SUBLANES
SEG_BITS = (SEG_TILE * TOP_K // SEG_ALIGN).bit_length()
PERM_CHUNK = 256


def _cparams(*sem):
    return pltpu.CompilerParams(dimension_semantics=sem, vmem_limit_bytes=VMEM_LIMIT)


def _layer_norm(y, g, b):
    mu = jnp.mean(y, axis=-1, keepdims=True)
    d = y - mu
    var = jnp.mean(d * d, axis=-1, keepdims=True)
    return d * lax.rsqrt(var + LN_EPS) * g + b


def _dot(a, b):
    return jnp.dot(a, b, preferred_element_type=F32)


def _dot_nt(a, b):
    return lax.dot_general(a, b, (((1,), (1,)), ((), ())), preferred_element_type=F32)


def _dot_tn(a, b):
    return lax.dot_general(a, b, (((0,), (0,)), ((), ())), preferred_element_type=F32)


def _sigmoid(x):
    return 1.0 / (1.0 + jnp.exp(-x))


def _hgrn_kernel(x_ref, w_in_ref, lb_ref, gn_ref, w_o_ref, lg_ref, lbias_ref, out_ref,
                 q_s, f_s, i_s, g_s, mix_s, st_s, *, dn_alpha):
    ts, d = x_ref.shape[1], x_ref.shape[2]
    n_heads = d // HG_HEAD_DIM
    grp = HG_GROUP

    @pl.when(pl.program_id(1) == 0)
    def _():
        st_s[...] = jnp.zeros_like(st_s)

    x = x_ref[0]
    xb = x.astype(BF16)
    lb = lb_ref[...]

    for sec, dst in enumerate((q_s, f_s, i_s, g_s)):
        p = _dot(xb, w_in_ref[:, sec * d:(sec + 1) * d])
        if sec == 0 or sec == 3:
            p = p * _sigmoid(p)
        elif sec == 1:
            p = lb + (1.0 - lb) * _sigmoid(p)
        for h in range(n_heads):
            dst[h] = p[:, h * HG_HEAD_DIM:(h + 1) * HG_HEAD_DIM]

    row = lax.broadcasted_iota(jnp.int32, (grp, grp), 0)
    col = lax.broadcasted_iota(jnp.int32, (grp, grp), 1)
    n_levels = grp.bit_length() - 1
    pair_masks = [(((row >> lvl) ^ (col >> lvl)) == 1) & (row > col) for lvl in range(n_levels)]

    def head_body(h, carry):
        for r0 in range(0, ts, grp):
            q = q_s[h, r0:r0 + grp, :]
            fg = f_s[h, r0:r0 + grp, :]
            iv = i_s[h, r0:r0 + grp, :].astype(BF16)
            k = 1.0 - fg
            ep, es, et = fg, None, fg
            a = jnp.where(row == col, _dot_nt(q.astype(BF16), k.astype(BF16)), 0.0)
            for lvl in range(n_levels):
                half = 1 << lvl
                kl = k if es is None else k * es
                a = jnp.where(pair_masks[lvl], _dot_nt((q * ep).astype(BF16), kl.astype(BF16)), a)
                if half < ROW_SUBLANES:
                    odd = (row & half) != 0
                    et3 = et.reshape(grp // ROW_SUBLANES, ROW_SUBLANES, LANES)
                    other = jnp.where(odd, pltpu.roll(et3, half, 1).reshape(et.shape),
                                      pltpu.roll(et3, ROW_SUBLANES - half, 1).reshape(et.shape))
                    ep = jnp.where(odd, ep * other, ep)
                    es = jnp.where(odd, 1.0, other) if es is None else jnp.where(odd, es, es * other)
                    et = et * other
                else:
                    ep_p, es_p, et_p = [], [], []
                    for b0 in range(0, grp, 2 * half):
                        lo, mid, hi = b0, b0 + half, b0 + 2 * half
                        tot = et[lo:mid] * et[mid:hi]
                        ep_p += [ep[lo:mid], ep[mid:hi] * et[lo:mid]]
                        es_p += [es[lo:mid] * et[mid:hi], es[mid:hi]]
                        et_p += [tot, tot]
                    ep = jnp.concatenate(ep_p, axis=0)
                    es = jnp.concatenate(es_p, axis=0)
                    et = jnp.concatenate(et_p, axis=0)
            st = st_s[h]
            o = _dot(a.astype(BF16), iv) + _dot_nt((q * ep).astype(BF16), st.astype(BF16))
            st_s[h] = st * et[0:1, :] + _dot_tn(iv, (k * es).astype(BF16))
            ms = jnp.mean(o * o, axis=-1, keepdims=True)
            y = o * lax.rsqrt(ms + RMS_EPS) * gn_ref[...] * g_s[h, r0:r0 + grp, :]
            mix_s[h, r0:r0 + grp, :] = y.astype(BF16)
        return carry

    lax.fori_loop(0, n_heads, head_body, 0, unroll=HG_HEAD_UNROLL)

    acc = dn_alpha * x
    for h in range(0, n_heads, 2):
        acc = acc + _dot(jnp.concatenate([mix_s[h], mix_s[h + 1]], axis=-1), w_o_ref[h // 2])
    out_ref[0] = _layer_norm(acc, lg_ref[...], lbias_ref[...])


def _hgrn_layer(x, w_in, lb, gnorm, w_o, ln_g, ln_b, dn_alpha):
    bsz, s_len, d = x.shape
    n_heads = d // HG_HEAD_DIM
    ts = ROW_TILE
    const2 = lambda b, s: (0, 0)
    return pl.pallas_call(
        functools.partial(_hgrn_kernel, dn_alpha=dn_alpha),
        grid=(bsz, s_len // ts),
        in_specs=[
            pl.BlockSpec((1, ts, d), lambda b, s: (b, s, 0)),
            pl.BlockSpec((d, 4 * d), const2),
            pl.BlockSpec((1, d), const2),
            pl.BlockSpec((1, HG_HEAD_DIM), const2),
            pl.BlockSpec((n_heads // 2, 2 * HG_HEAD_DIM, d), lambda b, s: (0, 0, 0)),
            pl.BlockSpec((1, d), const2),
            pl.BlockSpec((1, d), const2),
        ],
        out_specs=pl.BlockSpec((1, ts, d), lambda b, s: (b, s, 0)),
        out_shape=jax.ShapeDtypeStruct((bsz, s_len, d), F32),
        scratch_shapes=[
            pltpu.VMEM((n_heads, ts, HG_HEAD_DIM), F32),
            pltpu.VMEM((n_heads, ts, HG_HEAD_DIM), F32),
            pltpu.VMEM((n_heads, ts, HG_HEAD_DIM), F32),
            pltpu.VMEM((n_heads, ts, HG_HEAD_DIM), F32),
            pltpu.VMEM((n_heads, ts, HG_HEAD_DIM), BF16),
            pltpu.VMEM((n_heads, HG_HEAD_DIM, HG_HEAD_DIM), F32),
        ],
        compiler_params=_cparams("parallel", "arbitrary"),
        name="hgrn2_layer",
    )(x, w_in.astype(BF16), lb.reshape(1, d), gnorm.reshape(1, HG_HEAD_DIM),
      w_o.astype(BF16).reshape(n_heads // 2, 2 * HG_HEAD_DIM, d), ln_g.reshape(1, d), ln_b.reshape(1, d))


def _rope_lanes(t, cc, ss):
    return t * cc + pltpu.roll(t, LANES // 2, 1) * ss


def _kv_kernel(x_ref, wa_ref, kvn_ref, wb_ref, cc_ref, ss_ref, k_ref, v_ref):
    n_heads = k_ref.shape[1]
    xb = x_ref[0].astype(BF16)
    ckr = _dot(xb, wa_ref[...])
    c = ckr[:, :KV_RANK]
    c = c * lax.rsqrt(jnp.mean(c * c, axis=-1, keepdims=True) + RMS_EPS) * kvn_ref[...]
    kr = _rope_lanes(ckr[:, KV_RANK:], cc_ref[0], ss_ref[0]).astype(BF16)
    cb = c.astype(BF16)
    for h in range(n_heads):
        kv = _dot(cb, wb_ref[:, h * (NOPE_DIM + V_DIM):(h + 1) * (NOPE_DIM + V_DIM)])
        k_ref[0, h, :, :NOPE_DIM] = kv[:, :NOPE_DIM].astype(BF16)
        k_ref[0, h, :, NOPE_DIM:] = kr
        v_ref[0, h] = kv[:, NOPE_DIM:].astype(BF16)


def _shared_kv(x, kv_w_a_l, kv_norm, kv_w_b, cc, ss):
    bsz, s_len, d = x.shape
    n_heads = kv_w_b.shape[1] // (NOPE_DIM + V_DIM)
    ts = ROW_TILE
    const2 = lambda b, s: (0, 0)
    return pl.pallas_call(
        _kv_kernel,
        grid=(bsz, s_len // ts),
        in_specs=[
            pl.BlockSpec((1, ts, d), lambda b, s: (b, s, 0)),
            pl.BlockSpec((d, KV_RANK + LANES), const2),
            pl.BlockSpec((1, KV_RANK), const2),
            pl.BlockSpec((KV_RANK, n_heads * (NOPE_DIM + V_DIM)), const2),
            pl.BlockSpec((1, ts, LANES), lambda b, s: (b, s, 0)),
            pl.BlockSpec((1, ts, LANES), lambda b, s: (b, s, 0)),
        ],
        out_specs=[
            pl.BlockSpec((1, n_heads, ts, QK_DIM), lambda b, s: (b, 0, s, 0)),
            pl.BlockSpec((1, n_heads, ts, V_DIM), lambda b, s: (b, 0, s, 0)),
        ],
        out_shape=[
            jax.ShapeDtypeStruct((bsz, n_heads, s_len, QK_DIM), BF16),
            jax.ShapeDtypeStruct((bsz, n_heads, s_len, V_DIM), BF16),
        ],
        compiler_params=_cparams("parallel", "parallel"),
        name="mla_shared_kv",
    )(x, kv_w_a_l, kv_norm.reshape(1, KV_RANK), kv_w_b.astype(BF16), cc, ss)


def _q_kernel(x_ref, wdq_ref, qn_ref, wuq_ref, cc_ref, ss_ref, q_ref, *, scale):
    n_heads = q_ref.shape[1]
    xb = x_ref[0].astype(BF16)
    c = _dot(xb, wdq_ref[...])
    c = c * lax.rsqrt(jnp.mean(c * c, axis=-1, keepdims=True) + RMS_EPS) * qn_ref[...]
    cb = c.astype(BF16)
    cc = cc_ref[0] * scale
    ss = ss_ref[0] * scale
    for h in range(n_heads):
        qh = _dot(cb, wuq_ref[:, h * QK_DIM:(h + 1) * QK_DIM])
        q_ref[0, h, :, :NOPE_DIM] = (qh[:, :NOPE_DIM] * scale).astype(BF16)
        q_ref[0, h, :, NOPE_DIM:] = _rope_lanes(qh[:, NOPE_DIM:], cc, ss).astype(BF16)


def _mla_queries(x, w_dq, q_norm, w_uq_l, cc, ss, scale):
    bsz, s_len, d = x.shape
    q_rank = w_dq.shape[1]
    n_heads = w_uq_l.shape[1] // QK_DIM
    ts = ROW_TILE
    const2 = lambda b, s: (0, 0)
    return pl.pallas_call(
        functools.partial(_q_kernel, scale=scale),
        grid=(bsz, s_len // ts),
        in_specs=[
            pl.BlockSpec((1, ts, d), lambda b, s: (b, s, 0)),
            pl.BlockSpec((d, q_rank), const2),
            pl.BlockSpec((1, q_rank), const2),
            pl.BlockSpec((q_rank, n_heads * QK_DIM), const2),
            pl.BlockSpec((1, ts, LANES), lambda b, s: (b, s, 0)),
            pl.BlockSpec((1, ts, LANES), lambda b, s: (b, s, 0)),
        ],
        out_specs=pl.BlockSpec((1, n_heads, ts, QK_DIM), lambda b, s: (b, 0, s, 0)),
        out_shape=jax.ShapeDtypeStruct((bsz, n_heads, s_len, QK_DIM), BF16),
        compiler_params=_cparams("parallel", "parallel"),
        name="mla_queries",
    )(x, w_dq.astype(BF16), q_norm.reshape(1, q_rank), w_uq_l, cc, ss)


def _attn_kernel(q_ref, k_ref, v_ref, o_ref):
    s_len = q_ref.shape[2]
    tq = min(ATT_TQ, s_len)
    row = lax.broadcasted_iota(jnp.int32, (tq, tq), 0)
    col = lax.broadcasted_iota(jnp.int32, (tq, tq), 1)
    for qi in range(s_len // tq):
        q = q_ref[0, 0, qi * tq:(qi + 1) * tq, :]
        m = jnp.full((tq, 1), -jnp.inf, F32)
        l = jnp.zeros((tq, 1), F32)
        acc = jnp.zeros((tq, V_DIM), F32)
        for kj in range(qi + 1):
            s = _dot_nt(q, k_ref[0, 0, kj * tq:(kj + 1) * tq, :])
            if kj == qi:
                s = jnp.where(col <= row, s, -jnp.inf)
            m_new = jnp.maximum(m, jnp.max(s, axis=-1, keepdims=True))
            p = jnp.exp(s - m_new)
            corr = jnp.exp(m - m_new)
            l = corr * l + jnp.sum(p, axis=-1, keepdims=True)
            acc = corr * acc + _dot(p.astype(BF16), v_ref[0, 0, kj * tq:(kj + 1) * tq, :])
            m = m_new
        o_ref[0, qi * tq:(qi + 1) * tq, :] = (acc / l).astype(o_ref.dtype)


def _mla_attention(q, k, v):
    bsz, n_heads, s_len, _ = q.shape
    return pl.pallas_call(
        _attn_kernel,
        grid=(bsz, n_heads),
        in_specs=[
            pl.BlockSpec((1, 1, s_len, QK_DIM), lambda b, h: (b, h, 0, 0)),
            pl.BlockSpec((1, 1, s_len, QK_DIM), lambda b, h: (b, h, 0, 0)),
            pl.BlockSpec((1, 1, s_len, V_DIM), lambda b, h: (b, h, 0, 0)),
        ],
        out_specs=pl.BlockSpec((1, s_len, V_DIM), lambda b, h: (b, 0, h)),
        out_shape=jax.ShapeDtypeStruct((bsz, s_len, n_heads * V_DIM), BF16),
        compiler_params=_cparams("parallel", "parallel"),
        name="mla_attention",
    )(q, k, v)


def _proj_ln_kernel(o_ref, x_ref, w_ref, lg_ref, lb_ref, out_ref, *, dn_alpha):
    y = dn_alpha * x_ref[...] + _dot(o_ref[...], w_ref[...])
    out_ref[...] = _layer_norm(y, lg_ref[...], lb_ref[...])


def _proj_residual_ln(o2, x2, w_o, ln_g, ln_b, dn_alpha):
    n_tok, d = x2.shape
    kdim = o2.shape[1]
    ts = ROW_TILE
    const = lambda i: (0, 0)
    return pl.pallas_call(
        functools.partial(_proj_ln_kernel, dn_alpha=dn_alpha),
        grid=(n_tok // ts,),
        in_specs=[
            pl.BlockSpec((ts, kdim), lambda i: (i, 0)),
            pl.BlockSpec((ts, d), lambda i: (i, 0)),
            pl.BlockSpec((kdim, d), const),
            pl.BlockSpec((1, d), const),
            pl.BlockSpec((1, d), const),
        ],
        out_specs=pl.BlockSpec((ts, d), lambda i: (i, 0)),
        out_shape=jax.ShapeDtypeStruct((n_tok, d), F32),
        compiler_params=_cparams("parallel"),
        name="mla_out_proj_ln",
    )(o2, x2, w_o.astype(BF16), ln_g.reshape(1, d), ln_b.reshape(1, d))


def _split_hi_lo(v):
    hi = lax.bitcast_convert_type(lax.bitcast_convert_type(v, jnp.uint32) & jnp.uint32(0xFFFF0000), F32)
    return hi.astype(BF16), (v - hi).astype(BF16)


def _router_kernel(x_ref, whi_ref, wlo_ref, b_ref, pos_ref, gate_ref, meta_ref):
    tr = x_ref.shape[0]

    x_hi, x_lo = _split_hi_lo(x_ref[...])
    logits = (_dot(x_hi, whi_ref[...]) + (_dot(x_lo, whi_ref[...]) + _dot(x_hi, wlo_ref[...]))
              + b_ref[...])
    lane = lax.broadcasted_iota(jnp.int32, (tr, LANES), 1)
    work = logits
    sel = jnp.zeros((tr, LANES), F32)
    ids, vals = [], []
    for _ in range(TOP_K):
        mx = jnp.max(work, axis=-1, keepdims=True)
        idx = jnp.min(jnp.where(work == mx, lane, LANES), axis=-1, keepdims=True)
        hit = lane == idx
        ids.append(idx)
        vals.append(mx)
        sel = jnp.where(hit, 1.0, sel)
        work = jnp.where(hit, -jnp.inf, work)
    exps = [jnp.exp(v - vals[0]) for v in vals]
    denom = exps[0] + exps[1] + exps[2] + exps[3]

    r = lax.broadcasted_iota(jnp.int32, (tr, tr), 0)
    c = lax.broadcasted_iota(jnp.int32, (tr, tr), 1)
    before = _dot(jnp.where(c < r, 1.0, 0.0).astype(BF16), sel.astype(BF16))
    seg8 = jnp.floor((jnp.sum(sel, axis=0, keepdims=True) + (SEG_ALIGN - 1)) * (1.0 / SEG_ALIGN))
    er = lax.broadcasted_iota(jnp.int32, (LANES, LANES), 0)
    ec = lax.broadcasted_iota(jnp.int32, (LANES, LANES), 1)
    start8 = _dot(jnp.broadcast_to(seg8, (ROW_SUBLANES, LANES)).astype(BF16),
                  jnp.where(er < ec, 1.0, 0.0).astype(BF16))[0:1]
    slot = start8 * SEG_ALIGN + before

    pos_out = jnp.zeros((tr, LANES), F32)
    gate_out = jnp.zeros((tr, LANES), F32)
    for j in range(TOP_K):
        pos = jnp.sum(jnp.where(lane == ids[j], slot, 0.0), axis=-1, keepdims=True)
        pos_out = jnp.where(lane == j, pos, pos_out)
        gate_out = jnp.where(lane == j, exps[j] / denom, gate_out)
    pos_ref[...] = pos_out
    gate_ref[...] = gate_out
    row8 = lax.broadcasted_iota(jnp.int32, (ROW_SUBLANES, LANES), 0)
    meta_ref[...] = jnp.where(row8 == 0, seg8, jnp.where(row8 == 1, start8, 0.0))


def _router(x2, router_w, router_b):
    n_tok, d = x2.shape
    n_exp = router_w.shape[1]
    tr = SEG_TILE
    w_hi, w_lo = _split_hi_lo(jnp.zeros((d, LANES), F32).at[:, :n_exp].set(router_w))
    b = jnp.full((1, LANES), -jnp.inf, F32).at[0, :n_exp].set(router_b)
    const = lambda i: (0, 0)
    return pl.pallas_call(
        _router_kernel,
        grid=(n_tok // tr,),
        in_specs=[
            pl.BlockSpec((tr, d), lambda i: (i, 0)),
            pl.BlockSpec((d, LANES), const),
            pl.BlockSpec((d, LANES), const),
            pl.BlockSpec((1, LANES), const),
        ],
        out_specs=[
            pl.BlockSpec((tr, LANES), lambda i: (i, 0)),
            pl.BlockSpec((tr, LANES), lambda i: (i, 0)),
            pl.BlockSpec((ROW_SUBLANES, LANES), lambda i: (i, 0)),
        ],
        out_shape=[
            jax.ShapeDtypeStruct((n_tok, LANES), F32),
            jax.ShapeDtypeStruct((n_tok, LANES), F32),
            jax.ShapeDtypeStruct((n_tok // tr * ROW_SUBLANES, LANES), F32),
        ],
        compiler_params=_cparams("parallel"),
        name="moe_router",
    )(x2, w_hi, w_lo, b)


def _segment_copies(tabs, tile, n_exp, local_ref, hbm_ref, sem, to_hbm, act):
    seg8_ref, start8_ref, gstart8_ref = tabs

    def body(e, c):
        n = seg8_ref[tile * n_exp + e]
        lo = start8_ref[tile * n_exp + e]
        go = gstart8_ref[tile * n_exp + e]
        for bit in range(SEG_BITS):
            rows = SEG_ALIGN << bit

            @pl.when(((n >> bit) & 1) == 1)
            def _():
                off = (n >> (bit + 1)) << (bit + 1)
                loc = local_ref.at[pl.ds(pl.multiple_of((lo + off) * SEG_ALIGN, SEG_ALIGN), rows)]
                hbm = hbm_ref.at[pl.ds(pl.multiple_of((go + off) * SEG_ALIGN, SEG_ALIGN), rows)]
                act(pltpu.make_async_copy(loc, hbm, sem) if to_hbm else pltpu.make_async_copy(hbm, loc, sem))
        return c

    lax.fori_loop(0, n_exp, body, 0)


def _dispatch_kernel(seg8_ref, start8_ref, gstart8_ref, x_ref, pos_ref, xs_hbm, buf, sems, *, n_exp):
    td = x_ref.shape[0]
    seg_rows = buf.shape[1]
    tabs = (seg8_ref, start8_ref, gstart8_ref)
    step = pl.program_id(0)
    slot = step % 2

    def copies(tile, sl, act):
        _segment_copies(tabs, tile, n_exp, buf.at[sl], xs_hbm, sems.at[sl], True, act)

    @pl.when(step >= 2)
    def _():
        copies(step - 2, slot, lambda cp: cp.wait())

    xb = x_ref[...].astype(BF16)
    pos_t = pos_ref[...].T
    for r0 in range(0, seg_rows, PERM_CHUNK):
        rr = (lax.broadcasted_iota(jnp.int32, (PERM_CHUNK, td), 0) + r0).astype(F32)
        hit = pos_t[0:1] == rr
        for j in range(1, TOP_K):
            hit = hit | (pos_t[j:j + 1] == rr)
        buf[slot, r0:r0 + PERM_CHUNK, :] = _dot(jnp.where(hit, 1.0, 0.0).astype(BF16), xb)

    copies(step, slot, lambda cp: cp.start())

    @pl.when(step == pl.num_programs(0) - 1)
    def _():
        copies(step, slot, lambda cp: cp.wait())

        @pl.when(step >= 1)
        def _():
            copies(step - 1, 1 - slot, lambda cp: cp.wait())


def _seg_rows(td, n_exp):
    return -(-(td * TOP_K + n_exp * (SEG_ALIGN - 1)) // PERM_CHUNK) * PERM_CHUNK


def _dispatch(x2, pos, tabs, n_rows, n_exp):
    n_tok, d = x2.shape
    td = SEG_TILE
    grid_spec = pltpu.PrefetchScalarGridSpec(
        num_scalar_prefetch=3,
        grid=(n_tok // td,),
        in_specs=[
            pl.BlockSpec((td, d), lambda i, *_: (i, 0)),
            pl.BlockSpec((td, LANES), lambda i, *_: (i, 0)),
        ],
        out_specs=pl.BlockSpec(memory_space=pl.ANY),
        scratch_shapes=[pltpu.VMEM((2, _seg_rows(td, n_exp), d), F32), pltpu.SemaphoreType.DMA((2,))],
    )
    return pl.pallas_call(
        functools.partial(_dispatch_kernel, n_exp=n_exp),
        grid_spec=grid_spec,
        out_shape=jax.ShapeDtypeStruct((n_rows, d), F32),
        compiler_params=_cparams("arbitrary"),
        name="moe_dispatch",
    )(*tabs, x2, pos)


def _expert_kernel(be_ref, nu_ref, xs_ref, w1_ref, b1_ref, w2_ref, b2_ref, ys_ref):
    @pl.when(pl.program_id(0) < nu_ref[0])
    def _():
        f = w2_ref.shape[1]
        h = _dot(xs_ref[...].astype(BF16), w1_ref[0]) + b1_ref[0]
        glu = jnp.minimum(h[:, :f], SWIGLU_LIMIT)
        lin = jnp.clip(h[:, f:], -SWIGLU_LIMIT, SWIGLU_LIMIT)
        a = glu * _sigmoid(SWIGLU_ALPHA * glu) * (lin + 1.0)
        ys_ref[...] = _dot(a.astype(BF16), w2_ref[0]) + b2_ref[0]


def _experts(xs, block_e, n_used, w1, b1, w2, b2):
    n_exp, f, d = w2.shape
    blk = MOE_BLOCK
    n_blocks = xs.shape[0] // blk
    row_map = lambda i, be, nu: (jnp.minimum(i, nu[0] - 1), 0)
    exp_map = lambda i, be, nu: (be[i], 0, 0)
    grid_spec = pltpu.PrefetchScalarGridSpec(
        num_scalar_prefetch=2,
        grid=(n_blocks,),
        in_specs=[
            pl.BlockSpec((blk, d), row_map),
            pl.BlockSpec((1, d, 2 * f), exp_map),
            pl.BlockSpec((1, 1, 2 * f), exp_map),
            pl.BlockSpec((1, f, d), exp_map),
            pl.BlockSpec((1, 1, d), exp_map),
        ],
        out_specs=pl.BlockSpec((blk, d), row_map),
    )
    return pl.pallas_call(
        _expert_kernel,
        grid_spec=grid_spec,
        out_shape=jax.ShapeDtypeStruct(xs.shape, F32),
        compiler_params=_cparams("arbitrary"),
        name="moe_experts",
    )(block_e, n_used, xs, w1, b1.reshape(n_exp, 1, 2 * f), w2, b2.reshape(n_exp, 1, d))


def _combine_kernel(seg8_ref, start8_ref, gstart8_ref, ys_hbm, x_ref, pos_ref, gate_ref, lg_ref, lb_ref,
                    out_ref, buf, sems, *, n_exp, dn_alpha):
    td = x_ref.shape[0]
    seg_rows = buf.shape[1]
    tabs = (seg8_ref, start8_ref, gstart8_ref)
    step = pl.program_id(0)
    slot = step % 2

    def copies(tile, sl, act):
        _segment_copies(tabs, tile, n_exp, buf.at[sl], ys_hbm, sems.at[sl], False, act)

    @pl.when(step == 0)
    def _():
        buf[...] = jnp.zeros_like(buf)
        copies(step, slot, lambda cp: cp.start())

    @pl.when(step + 1 < pl.num_programs(0))
    def _():
        copies(step + 1, 1 - slot, lambda cp: cp.start())

    copies(step, slot, lambda cp: cp.wait())

    pos = pos_ref[...]
    gates = gate_ref[...]
    acc = dn_alpha * x_ref[...]
    for r0 in range(0, seg_rows, PERM_CHUNK):
        cc = (lax.broadcasted_iota(jnp.int32, (td, PERM_CHUNK), 1) + r0).astype(F32)
        g = jnp.where(pos[:, 0:1] == cc, gates[:, 0:1], 0.0)
        for j in range(1, TOP_K):
            g = g + jnp.where(pos[:, j:j + 1] == cc, gates[:, j:j + 1], 0.0)
        acc = acc + _dot(g.astype(BF16), buf[slot, r0:r0 + PERM_CHUNK, :].astype(BF16))
    out_ref[...] = _layer_norm(acc, lg_ref[...], lb_ref[...])


def _combine_ln(ys, pos, gates, tabs, x2, ln_g, ln_b, n_exp, dn_alpha):
    n_tok, d = x2.shape
    td = SEG_TILE
    const = lambda i, *_: (0, 0)
    tile = lambda i, *_: (i, 0)
    grid_spec = pltpu.PrefetchScalarGridSpec(
        num_scalar_prefetch=3,
        grid=(n_tok // td,),
        in_specs=[
            pl.BlockSpec(memory_space=pl.ANY),
            pl.BlockSpec((td, d), tile),
            pl.BlockSpec((td, LANES), tile),
            pl.BlockSpec((td, LANES), tile),
            pl.BlockSpec((1, d), const),
            pl.BlockSpec((1, d), const),
        ],
        out_specs=pl.BlockSpec((td, d), tile),
        scratch_shapes=[pltpu.VMEM((2, _seg_rows(td, n_exp), d), F32), pltpu.SemaphoreType.DMA((2,))],
    )
    return pl.pallas_call(
        functools.partial(_combine_kernel, n_exp=n_exp, dn_alpha=dn_alpha),
        grid_spec=grid_spec,
        out_shape=jax.ShapeDtypeStruct((n_tok, d), F32),
        compiler_params=_cparams("arbitrary"),
        name="moe_combine_ln",
    )(*tabs, ys, x2, pos, gates, ln_g.reshape(1, d), ln_b.reshape(1, d))


def _w1_prep_kernel(w_ref, p_ref, out_ref):
    f = w_ref.shape[2] // 2
    width = p_ref.shape[0]
    for c in range(2 * f // width):
        t = _dot(w_ref[0, :, c * width:(c + 1) * width].astype(BF16), p_ref[...])
        out_ref[0, :, c * (width // 2):(c + 1) * (width // 2)] = t[:, :width // 2].astype(BF16)
        out_ref[0, :, f + c * (width // 2):f + (c + 1) * (width // 2)] = t[:, width // 2:].astype(BF16)


def _w1_prep(w1):
    n_exp, d, f2 = w1.shape
    width = 2 * LANES
    rows = ROW_TILE
    r = lax.broadcasted_iota(jnp.int32, (width, width), 0)
    c = lax.broadcasted_iota(jnp.int32, (width, width), 1)
    perm = (c == (r % 2) * (width // 2) + r // 2).astype(BF16)
    return pl.pallas_call(
        _w1_prep_kernel,
        grid=(n_exp, d // rows),
        in_specs=[
            pl.BlockSpec((1, rows, f2), lambda e, i: (e, i, 0)),
            pl.BlockSpec((width, width), lambda e, i: (0, 0)),
        ],
        out_specs=pl.BlockSpec((1, rows, f2), lambda e, i: (e, i, 0)),
        out_shape=jax.ShapeDtypeStruct((n_exp, d, f2), BF16),
        compiler_params=_cparams("parallel", "parallel"),
        name="moe_w1_prep",
    )(w1, perm)


def _moe_layer(x2, router_w, router_b, w1, b1, w2, b2, ln_g, ln_b, dn_alpha):
    n_tok, d = x2.shape
    n_exp = w2.shape[0]
    n_tiles = n_tok // SEG_TILE
    blk8 = MOE_BLOCK // SEG_ALIGN
    n_blocks = -(-(n_tok * TOP_K + n_tiles * n_exp * (SEG_ALIGN - 1)) // MOE_BLOCK) + n_exp
    n_rows = n_blocks * MOE_BLOCK

    pos, gates, meta = _router(x2, router_w, router_b)

    meta = meta.reshape(n_tiles, ROW_SUBLANES, LANES)
    seg8 = meta[:, 0, :n_exp].astype(jnp.int32)
    start8 = meta[:, 1, :n_exp].astype(jnp.int32)
    padded8 = (jnp.sum(seg8, axis=0) + blk8 - 1) // blk8 * blk8
    pends8 = jnp.cumsum(padded8)
    gstart8 = (pends8 - padded8)[None, :] + jnp.cumsum(seg8, axis=0) - seg8
    block_start8 = jnp.arange(n_blocks, dtype=jnp.int32) * blk8
    block_e = jnp.minimum(jnp.sum((pends8[None, :] <= block_start8[:, None]).astype(jnp.int32), axis=1),
                          n_exp - 1).astype(jnp.int32)
    n_used = (pends8[-1:] // blk8).astype(jnp.int32)
    tabs = (seg8.reshape(-1), start8.reshape(-1), gstart8.reshape(-1).astype(jnp.int32))

    b1_l = jnp.concatenate([b1[:, 0::2], b1[:, 1::2]], axis=-1)

    xs = _dispatch(x2, pos, tabs, n_rows, n_exp)
    ys = _experts(xs, block_e, n_used, _w1_prep(w1), b1_l, w2.astype(BF16), b2)
    return _combine_ln(ys, pos, gates, tabs, x2, ln_g, ln_b, n_exp, dn_alpha)


def _rope_lane_tables(positions):
    half = ROPE_DIM // 2
    inv_freq = ROPE_THETA ** (-jnp.arange(0, ROPE_DIM, 2, dtype=F32) / ROPE_DIM)
    ang = positions.astype(F32)[..., None] * inv_freq
    cos, sin = jnp.cos(ang), jnp.sin(ang)
    z = jnp.zeros_like(cos)
    assert 4 * half == LANES
    return (jnp.concatenate([cos, z, cos, z], axis=-1),
            jnp.concatenate([-sin, z, sin, z], axis=-1))


def _rope_lane_columns(w_rope):
    half = ROPE_DIM // 2
    z = jnp.zeros(w_rope.shape[:-1] + (half,), w_rope.dtype)
    return jnp.concatenate([w_rope[..., :half], z, w_rope[..., half:], z], axis=-1)


def kernel(x, positions, ln_g, ln_b, hg_w_in, hg_lb, hg_gnorm, hg_w_o, mla_w_dq, mla_q_norm,
           mla_w_uq, mla_w_o, kv_w_a, kv_norm, kv_w_b, router_w, router_b, moe_w1, moe_b1,
           moe_w2, moe_b2):
    bsz, s_len, d = x.shape
    depth = ln_g.shape[0]
    n_a = hg_w_in.shape[0]
    dn_alpha = (2.0 * depth) ** 0.25
    scale = (NOPE_DIM + ROPE_DIM) ** -0.5

    lb_soft = jax.nn.softmax(hg_lb.astype(F32), axis=0)
    lower_bounds = jnp.cumsum(lb_soft, axis=0) - lb_soft[0]

    q_rank = mla_w_uq.shape[1]
    mla_heads = mla_w_uq.shape[2] // (NOPE_DIM + ROPE_DIM)
    cc = ss = k_full = v_full = None

    for layer in range(depth):
        if layer < n_a:
            x = _hgrn_layer(x, hg_w_in[layer], lower_bounds[layer], hg_gnorm[layer], hg_w_o[layer],
                            ln_g[layer, 0], ln_b[layer, 0], dn_alpha)
        else:
            j = layer - n_a
            if layer == n_a:
                cc, ss = _rope_lane_tables(positions)
                kv_w_a_l = jnp.concatenate(
                    [kv_w_a[:, :KV_RANK], _rope_lane_columns(kv_w_a[:, KV_RANK:])], axis=-1).astype(BF16)
                k_full, v_full = _shared_kv(x, kv_w_a_l, kv_norm, kv_w_b, cc, ss)
            w_uq = mla_w_uq[j].reshape(q_rank, mla_heads, NOPE_DIM + ROPE_DIM)
            w_uq_l = jnp.concatenate(
                [w_uq[..., :NOPE_DIM], _rope_lane_columns(w_uq[..., NOPE_DIM:])],
                axis=-1).reshape(q_rank, mla_heads * QK_DIM).astype(BF16)
            q_full = _mla_queries(x, mla_w_dq[j], mla_q_norm[j], w_uq_l, cc, ss, scale)
            o = _mla_attention(q_full, k_full, v_full)
            x = _proj_residual_ln(o.reshape(bsz * s_len, -1), x.reshape(bsz * s_len, d), mla_w_o[j],
                                  ln_g[layer, 0], ln_b[layer, 0], dn_alpha).reshape(bsz, s_len, d)
        x = _moe_layer(x.reshape(bsz * s_len, d), router_w[layer], router_b[layer], moe_w1[layer],
                       moe_b1[layer], moe_w2[layer], moe_b2[layer], ln_g[layer, 1], ln_b[layer, 1],
                       dn_alpha).reshape(bsz, s_len, d)
    return x
```

```python
import functools

import jax
import jax.numpy as jnp
from jax import lax
from jax.experimental import pallas as pl
from jax.experimental.pallas import tpu as pltpu

F32 = jnp.float32
BF16 = jnp.bfloat16

HG_HEAD_DIM = 128
NOPE_DIM = 128
ROPE_DIM = 64
V_DIM = 128
KV_RANK = 128
ROPE_THETA = 10000.0
TOP_K = 4
SWIGLU_ALPHA = 1.702
SWIGLU_LIMIT = 7.0
LN_EPS = 1e-5
RMS_EPS = 1e-6

LANES = 128
ROW_SUBLANES = 8
QK_DIM = 256
VMEM_LIMIT = 56 * 1024 * 1024

ROW_TILE = 256
HG_GROUP = 128
HG_HEAD_UNROLL = 8
ATT_TQ = 512
MOE_BLOCK = 512
SEG_TILE = 512
SEG_ALIGN = ROW_SUBLANES
SEG_BITS = (SEG_TILE * TOP_K // SEG_ALIGN).bit_length()
SEG_LOW_BITS = 4
PERM_CHUNK = 256


def _cparams(*sem):
    return pltpu.CompilerParams(dimension_semantics=sem, vmem_limit_bytes=VMEM_LIMIT)


def _layer_norm(y, g, b):
    mu = jnp.mean(y, axis=-1, keepdims=True)
    d = y - mu
    var = jnp.mean(d * d, axis=-1, keepdims=True)
    return d * lax.rsqrt(var + LN_EPS) * g + b


def _dot(a, b):
    return jnp.dot(a, b, preferred_element_type=F32)


def _dot_nt(a, b):
    return lax.dot_general(a, b, (((1,), (1,)), ((), ())), preferred_element_type=F32)


def _dot_tn(a, b):
    return lax.dot_general(a, b, (((0,), (0,)), ((), ())), preferred_element_type=F32)


def _sigmoid(x):
    return 1.0 / (1.0 + jnp.exp(-x))


def _hgrn_kernel(x_ref, w_in_ref, lb_ref, gn_ref, w_o_ref, lg_ref, lbias_ref, out_ref,
                 q_s, f_s, i_s, g_s, mix_s, st_s, *, dn_alpha):
    ts, d = x_ref.shape[1], x_ref.shape[2]
    n_heads = d // HG_HEAD_DIM
    grp = HG_GROUP

    @pl.when(pl.program_id(1) == 0)
    def _():
        st_s[...] = jnp.zeros_like(st_s)

    x = x_ref[0]
    xb = x.astype(BF16)
    lb = lb_ref[...]

    for sec, dst in enumerate((q_s, f_s, i_s, g_s)):
        p = _dot(xb, w_in_ref[:, sec * d:(sec + 1) * d])
        if sec == 0 or sec == 3:
            p = p * _sigmoid(p)
        elif sec == 1:
            p = lb + (1.0 - lb) * _sigmoid(p)
        for h in range(n_heads):
            dst[h] = p[:, h * HG_HEAD_DIM:(h + 1) * HG_HEAD_DIM]

    row = lax.broadcasted_iota(jnp.int32, (grp, grp), 0)
    col = lax.broadcasted_iota(jnp.int32, (grp, grp), 1)
    n_levels = grp.bit_length() - 1
    pair_masks = [(((row >> lvl) ^ (col >> lvl)) == 1) & (row > col) for lvl in range(n_levels)]

    def head_body(h, carry):
        for r0 in range(0, ts, grp):
            q = q_s[h, r0:r0 + grp, :]
            fg = f_s[h, r0:r0 + grp, :]
            iv = i_s[h, r0:r0 + grp, :].astype(BF16)
            k = 1.0 - fg
            ep, es, et = fg, None, fg
            a = jnp.where(row == col, _dot_nt(q.astype(BF16), k.astype(BF16)), 0.0)
            for lvl in range(n_levels):
                half = 1 << lvl
                kl = k if es is None else k * es
                a = jnp.where(pair_masks[lvl], _dot_nt((q * ep).astype(BF16), kl.astype(BF16)), a)
                if half < ROW_SUBLANES:
                    odd = (row & half) != 0
                    et3 = et.reshape(grp // ROW_SUBLANES, ROW_SUBLANES, LANES)
                    other = jnp.where(odd, pltpu.roll(et3, half, 1).reshape(et.shape),
                                      pltpu.roll(et3, ROW_SUBLANES - half, 1).reshape(et.shape))
                    ep = jnp.where(odd, ep * other, ep)
                    es = jnp.where(odd, 1.0, other) if es is None else jnp.where(odd, es, es * other)
                    et = et * other
                else:
                    ep_p, es_p, et_p = [], [], []
                    for b0 in range(0, grp, 2 * half):
                        lo, mid, hi = b0, b0 + half, b0 + 2 * half
                        tot = et[lo:mid] * et[mid:hi]
                        ep_p += [ep[lo:mid], ep[mid:hi] * et[lo:mid]]
                        es_p += [es[lo:mid] * et[mid:hi], es[mid:hi]]
                        et_p += [tot, tot]
                    ep = jnp.concatenate(ep_p, axis=0)
                    es = jnp.concatenate(es_p, axis=0)
                    et = jnp.concatenate(et_p, axis=0)
            st = st_s[h]
            o = _dot(a.astype(BF16), iv) + _dot_nt((q * ep).astype(BF16), st.astype(BF16))
            st_s[h] = st * et[0:1, :] + _dot_tn(iv, (k * es).astype(BF16))
            ms = jnp.mean(o * o, axis=-1, keepdims=True)
            y = o * lax.rsqrt(ms + RMS_EPS) * gn_ref[...] * g_s[h, r0:r0 + grp, :]
            mix_s[h, r0:r0 + grp, :] = y.astype(BF16)
        return carry

    lax.fori_loop(0, n_heads, head_body, 0, unroll=HG_HEAD_UNROLL)

    acc = dn_alpha * x
    for h in range(0, n_heads, 2):
        acc = acc + _dot(jnp.concatenate([mix_s[h], mix_s[h + 1]], axis=-1), w_o_ref[h // 2])
    out_ref[0] = _layer_norm(acc, lg_ref[...], lbias_ref[...])


def _hgrn_layer(x, w_in, lb, gnorm, w_o, ln_g, ln_b, dn_alpha):
    bsz, s_len, d = x.shape
    n_heads = d // HG_HEAD_DIM
    ts = ROW_TILE
    const2 = lambda b, s: (0, 0)
    return pl.pallas_call(
        functools.partial(_hgrn_kernel, dn_alpha=dn_alpha),
        grid=(bsz, s_len // ts),
        in_specs=[
            pl.BlockSpec((1, ts, d), lambda b, s: (b, s, 0)),
            pl.BlockSpec((d, 4 * d), const2),
            pl.BlockSpec((1, d), const2),
            pl.BlockSpec((1, HG_HEAD_DIM), const2),
            pl.BlockSpec((n_heads // 2, 2 * HG_HEAD_DIM, d), lambda b, s: (0, 0, 0)),
            pl.BlockSpec((1, d), const2),
            pl.BlockSpec((1, d), const2),
        ],
        out_specs=pl.BlockSpec((1, ts, d), lambda b, s: (b, s, 0)),
        out_shape=jax.ShapeDtypeStruct((bsz, s_len, d), F32),
        scratch_shapes=[
            pltpu.VMEM((n_heads, ts, HG_HEAD_DIM), F32),
            pltpu.VMEM((n_heads, ts, HG_HEAD_DIM), F32),
            pltpu.VMEM((n_heads, ts, HG_HEAD_DIM), F32),
            pltpu.VMEM((n_heads, ts, HG_HEAD_DIM), F32),
            pltpu.VMEM((n_heads, ts, HG_HEAD_DIM), BF16),
            pltpu.VMEM((n_heads, HG_HEAD_DIM, HG_HEAD_DIM), F32),
        ],
        compiler_params=_cparams("parallel", "arbitrary"),
        name="hgrn2_layer",
    )(x, w_in.astype(BF16), lb.reshape(1, d), gnorm.reshape(1, HG_HEAD_DIM),
      w_o.astype(BF16).reshape(n_heads // 2, 2 * HG_HEAD_DIM, d), ln_g.reshape(1, d), ln_b.reshape(1, d))


def _rope_lanes(t, cc, ss):
    return t * cc + pltpu.roll(t, LANES // 2, 1) * ss


def _kv_kernel(x_ref, wa_ref, kvn_ref, wb_ref, cc_ref, ss_ref, k_ref, v_ref):
    n_heads = k_ref.shape[1]
    xb = x_ref[0].astype(BF16)
    ckr = _dot(xb, wa_ref[...])
    c = ckr[:, :KV_RANK]
    c = c * lax.rsqrt(jnp.mean(c * c, axis=-1, keepdims=True) + RMS_EPS) * kvn_ref[...]
    kr = _rope_lanes(ckr[:, KV_RANK:], cc_ref[0], ss_ref[0]).astype(BF16)
    cb = c.astype(BF16)
    for h in range(n_heads):
        kv = _dot(cb, wb_ref[:, h * (NOPE_DIM + V_DIM):(h + 1) * (NOPE_DIM + V_DIM)])
        k_ref[0, h, :, :NOPE_DIM] = kv[:, :NOPE_DIM].astype(BF16)
        k_ref[0, h, :, NOPE_DIM:] = kr
        v_ref[0, h] = kv[:, NOPE_DIM:].astype(BF16)


def _shared_kv(x, kv_w_a_l, kv_norm, kv_w_b, cc, ss):
    bsz, s_len, d = x.shape
    n_heads = kv_w_b.shape[1] // (NOPE_DIM + V_DIM)
    ts = ROW_TILE
    const2 = lambda b, s: (0, 0)
    return pl.pallas_call(
        _kv_kernel,
        grid=(bsz, s_len // ts),
        in_specs=[
            pl.BlockSpec((1, ts, d), lambda b, s: (b, s, 0)),
            pl.BlockSpec((d, KV_RANK + LANES), const2),
            pl.BlockSpec((1, KV_RANK), const2),
            pl.BlockSpec((KV_RANK, n_heads * (NOPE_DIM + V_DIM)), const2),
            pl.BlockSpec((1, ts, LANES), lambda b, s: (b, s, 0)),
            pl.BlockSpec((1, ts, LANES), lambda b, s: (b, s, 0)),
        ],
        out_specs=[
            pl.BlockSpec((1, n_heads, ts, QK_DIM), lambda b, s: (b, 0, s, 0)),
            pl.BlockSpec((1, n_heads, ts, V_DIM), lambda b, s: (b, 0, s, 0)),
        ],
        out_shape=[
            jax.ShapeDtypeStruct((bsz, n_heads, s_len, QK_DIM), BF16),
            jax.ShapeDtypeStruct((bsz, n_heads, s_len, V_DIM), BF16),
        ],
        compiler_params=_cparams("parallel", "parallel"),
        name="mla_shared_kv",
    )(x, kv_w_a_l, kv_norm.reshape(1, KV_RANK), kv_w_b.astype(BF16), cc, ss)


def _q_kernel(x_ref, wdq_ref, qn_ref, wuq_ref, cc_ref, ss_ref, q_ref, *, scale):
    n_heads = q_ref.shape[1]
    xb = x_ref[0].astype(BF16)
    c = _dot(xb, wdq_ref[...])
    c = c * lax.rsqrt(jnp.mean(c * c, axis=-1, keepdims=True) + RMS_EPS) * qn_ref[...]
    cb = c.astype(BF16)
    cc = cc_ref[0] * scale
    ss = ss_ref[0] * scale
    for h in range(n_heads):
        qh = _dot(cb, wuq_ref[:, h * QK_DIM:(h + 1) * QK_DIM])
        q_ref[0, h, :, :NOPE_DIM] = (qh[:, :NOPE_DIM] * scale).astype(BF16)
        q_ref[0, h, :, NOPE_DIM:] = _rope_lanes(qh[:, NOPE_DIM:], cc, ss).astype(BF16)


def _mla_queries(x, w_dq, q_norm, w_uq_l, cc, ss, scale):
    bsz, s_len, d = x.shape
    q_rank = w_dq.shape[1]
    n_heads = w_uq_l.shape[1] // QK_DIM
    ts = ROW_TILE
    const2 = lambda b, s: (0, 0)
    return pl.pallas_call(
        functools.partial(_q_kernel, scale=scale),
        grid=(bsz, s_len // ts),
        in_specs=[
            pl.BlockSpec((1, ts, d), lambda b, s: (b, s, 0)),
            pl.BlockSpec((d, q_rank), const2),
            pl.BlockSpec((1, q_rank), const2),
            pl.BlockSpec((q_rank, n_heads * QK_DIM), const2),
            pl.BlockSpec((1, ts, LANES), lambda b, s: (b, s, 0)),
            pl.BlockSpec((1, ts, LANES), lambda b, s: (b, s, 0)),
        ],
        out_specs=pl.BlockSpec((1, n_heads, ts, QK_DIM), lambda b, s: (b, 0, s, 0)),
        out_shape=jax.ShapeDtypeStruct((bsz, n_heads, s_len, QK_DIM), BF16),
        compiler_params=_cparams("parallel", "parallel"),
        name="mla_queries",
    )(x, w_dq.astype(BF16), q_norm.reshape(1, q_rank), w_uq_l, cc, ss)


def _attn_kernel(q_ref, k_ref, v_ref, o_ref):
    s_len = q_ref.shape[2]
    tq = min(ATT_TQ, s_len)
    row = lax.broadcasted_iota(jnp.int32, (tq, tq), 0)
    col = lax.broadcasted_iota(jnp.int32, (tq, tq), 1)
    for qi in range(s_len // tq):
        q = q_ref[0, 0, qi * tq:(qi + 1) * tq, :]
        m = jnp.full((tq, 1), -jnp.inf, F32)
        l = jnp.zeros((tq, 1), F32)
        acc = jnp.zeros((tq, V_DIM), F32)
        for kj in range(qi + 1):
            s = _dot_nt(q, k_ref[0, 0, kj * tq:(kj + 1) * tq, :])
            if kj == qi:
                s = jnp.where(col <= row, s, -jnp.inf)
            m_new = jnp.maximum(m, jnp.max(s, axis=-1, keepdims=True))
            p = jnp.exp(s - m_new)
            corr = jnp.exp(m - m_new)
            l = corr * l + jnp.sum(p, axis=-1, keepdims=True)
            acc = corr * acc + _dot(p.astype(BF16), v_ref[0, 0, kj * tq:(kj + 1) * tq, :])
            m = m_new
        o_ref[0, qi * tq:(qi + 1) * tq, :] = (acc / l).astype(o_ref.dtype)


def _mla_attention(q, k, v):
    bsz, n_heads, s_len, _ = q.shape
    return pl.pallas_call(
        _attn_kernel,
        grid=(bsz, n_heads),
        in_specs=[
            pl.BlockSpec((1, 1, s_len, QK_DIM), lambda b, h: (b, h, 0, 0)),
            pl.BlockSpec((1, 1, s_len, QK_DIM), lambda b, h: (b, h, 0, 0)),
            pl.BlockSpec((1, 1, s_len, V_DIM), lambda b, h: (b, h, 0, 0)),
        ],
        out_specs=pl.BlockSpec((1, s_len, V_DIM), lambda b, h: (b, 0, h)),
        out_shape=jax.ShapeDtypeStruct((bsz, s_len, n_heads * V_DIM), BF16),
        compiler_params=_cparams("parallel", "parallel"),
        name="mla_attention",
    )(q, k, v)


def _proj_ln_kernel(o_ref, x_ref, w_ref, lg_ref, lb_ref, out_ref, *, dn_alpha):
    y = dn_alpha * x_ref[...] + _dot(o_ref[...], w_ref[...])
    out_ref[...] = _layer_norm(y, lg_ref[...], lb_ref[...])


def _proj_residual_ln(o2, x2, w_o, ln_g, ln_b, dn_alpha):
    n_tok, d = x2.shape
    kdim = o2.shape[1]
    ts = ROW_TILE
    const = lambda i: (0, 0)
    return pl.pallas_call(
        functools.partial(_proj_ln_kernel, dn_alpha=dn_alpha),
        grid=(n_tok // ts,),
        in_specs=[
            pl.BlockSpec((ts, kdim), lambda i: (i, 0)),
            pl.BlockSpec((ts, d), lambda i: (i, 0)),
            pl.BlockSpec((kdim, d), const),
            pl.BlockSpec((1, d), const),
            pl.BlockSpec((1, d), const),
        ],
        out_specs=pl.BlockSpec((ts, d), lambda i: (i, 0)),
        out_shape=jax.ShapeDtypeStruct((n_tok, d), F32),
        compiler_params=_cparams("parallel"),
        name="mla_out_proj_ln",
    )(o2, x2, w_o.astype(BF16), ln_g.reshape(1, d), ln_b.reshape(1, d))


def _split_hi_lo(v):
    hi = lax.bitcast_convert_type(lax.bitcast_convert_type(v, jnp.uint32) & jnp.uint32(0xFFFF0000), F32)
    return hi.astype(BF16), (v - hi).astype(BF16)


def _router_kernel(x_ref, whi_ref, wlo_ref, b_ref, pos_ref, gate_ref, meta_ref):
    tr = x_ref.shape[0]

    x_hi, x_lo = _split_hi_lo(x_ref[...])
    logits = (_dot(x_hi, whi_ref[...]) + (_dot(x_lo, whi_ref[...]) + _dot(x_hi, wlo_ref[...]))
              + b_ref[...])
    lane = lax.broadcasted_iota(jnp.int32, (tr, LANES), 1)
    work = logits
    sel = jnp.zeros((tr, LANES), F32)
    ids, vals = [], []
    for _ in range(TOP_K):
        mx = jnp.max(work, axis=-1, keepdims=True)
        idx = jnp.min(jnp.where(work == mx, lane, LANES), axis=-1, keepdims=True)
        hit = lane == idx
        ids.append(idx)
        vals.append(mx)
        sel = jnp.where(hit, 1.0, sel)
        work = jnp.where(hit, -jnp.inf, work)
    exps = [jnp.exp(v - vals[0]) for v in vals]
    denom = exps[0] + exps[1] + exps[2] + exps[3]

    r = lax.broadcasted_iota(jnp.int32, (tr, tr), 0)
    c = lax.broadcasted_iota(jnp.int32, (tr, tr), 1)
    before = _dot(jnp.where(c < r, 1.0, 0.0).astype(BF16), sel.astype(BF16))
    seg8 = jnp.floor((jnp.sum(sel, axis=0, keepdims=True) + (SEG_ALIGN - 1)) * (1.0 / SEG_ALIGN))
    er = lax.broadcasted_iota(jnp.int32, (LANES, LANES), 0)
    ec = lax.broadcasted_iota(jnp.int32, (LANES, LANES), 1)
    start8 = _dot(jnp.broadcast_to(seg8, (ROW_SUBLANES, LANES)).astype(BF16),
                  jnp.where(er < ec, 1.0, 0.0).astype(BF16))[0:1]
    slot = start8 * SEG_ALIGN + before

    pos_out = jnp.zeros((tr, LANES), F32)
    gate_out = jnp.zeros((tr, LANES), F32)
    for j in range(TOP_K):
        pos = jnp.sum(jnp.where(lane == ids[j], slot, 0.0), axis=-1, keepdims=True)
        pos_out = jnp.where(lane == j, pos, pos_out)
        gate_out = jnp.where(lane == j, exps[j] / denom, gate_out)
    pos_ref[...] = pos_out
    gate_ref[...] = gate_out
    row8 = lax.broadcasted_iota(jnp.int32, (ROW_SUBLANES, LANES), 0)
    meta_ref[...] = jnp.where(row8 == 0, seg8, jnp.where(row8 == 1, start8, 0.0))


def _router(x2, router_w, router_b):
    n_tok, d = x2.shape
    n_exp = router_w.shape[1]
    tr = SEG_TILE
    w_hi, w_lo = _split_hi_lo(jnp.zeros((d, LANES), F32).at[:, :n_exp].set(router_w))
    b = jnp.full((1, LANES), -jnp.inf, F32).at[0, :n_exp].set(router_b)
    const = lambda i: (0, 0)
    return pl.pallas_call(
        _router_kernel,
        grid=(n_tok // tr,),
        in_specs=[
            pl.BlockSpec((tr, d), lambda i: (i, 0)),
            pl.BlockSpec((d, LANES), const),
            pl.BlockSpec((d, LANES), const),
            pl.BlockSpec((1, LANES), const),
        ],
        out_specs=[
            pl.BlockSpec((tr, LANES), lambda i: (i, 0)),
            pl.BlockSpec((tr, LANES), lambda i: (i, 0)),
            pl.BlockSpec((ROW_SUBLANES, LANES), lambda i: (i, 0)),
        ],
        out_shape=[
            jax.ShapeDtypeStruct((n_tok, LANES), F32),
            jax.ShapeDtypeStruct((n_tok, LANES), F32),
            jax.ShapeDtypeStruct((n_tok // tr * ROW_SUBLANES, LANES), F32),
        ],
        compiler_params=_cparams("parallel"),
        name="moe_router",
    )(x2, w_hi, w_lo, b)


def _segment_copy(local_ref, hbm_ref, sem, to_hbm, lo8, go8, bit):
    rows = SEG_ALIGN << bit
    aligned = lambda v8: v8 * SEG_ALIGN if isinstance(v8, int) else pl.multiple_of(v8 * SEG_ALIGN, SEG_ALIGN)
    loc = local_ref.at[pl.ds(aligned(lo8), rows)]
    hbm = hbm_ref.at[pl.ds(aligned(go8), rows)]
    return pltpu.make_async_copy(loc, hbm, sem) if to_hbm else pltpu.make_async_copy(hbm, loc, sem)


def _segment_starts(tabs, tile, n_exp, local_ref, hbm_ref, sem, to_hbm):
    seg8_ref, start8_ref, gstart8_ref, _ = tabs

    def body(e, c):
        n = seg8_ref[tile * n_exp + e]
        lo = start8_ref[tile * n_exp + e]
        go = gstart8_ref[tile * n_exp + e]

        def bits(lo_bit, hi_bit):
            for bit in range(lo_bit, hi_bit):
                @pl.when(((n >> bit) & 1) == 1)
                def _():
                    off = (n >> (bit + 1)) << (bit + 1)
                    _segment_copy(local_ref, hbm_ref, sem, to_hbm, lo + off, go + off, bit).start()

        bits(0, SEG_LOW_BITS)

        @pl.when(n >= (1 << SEG_LOW_BITS))
        def _():
            bits(SEG_LOW_BITS, SEG_BITS)
        return c

    lax.fori_loop(0, n_exp, body, 0)


def _segment_wait(tabs, tile, local_ref, hbm_ref, sem, to_hbm):
    total = tabs[3][tile]
    for bit in range(SEG_BITS):
        @pl.when(((total >> bit) & 1) == 1)
        def _():
            _segment_copy(local_ref, hbm_ref, sem, to_hbm, 0, 0, bit).wait()


def _dispatch_kernel(seg8_ref, start8_ref, gstart8_ref, tot8_ref, x_ref, pos_ref, xs_hbm, buf, sems, *, n_exp):
    td = x_ref.shape[0]
    seg_rows = buf.shape[1]
    tabs = (seg8_ref, start8_ref, gstart8_ref, tot8_ref)
    step = pl.program_id(0)
    slot = step % 2

    def wait(tile, sl):
        _segment_wait(tabs, tile, buf.at[sl], xs_hbm, sems.at[sl], True)

    @pl.when(step >= 2)
    def _():
        wait(step - 2, slot)

    xb = x_ref[...].astype(BF16)
    pos_t = pos_ref[...].T
    for r0 in range(0, seg_rows, PERM_CHUNK):
        rr = (lax.broadcasted_iota(jnp.int32, (PERM_CHUNK, td), 0) + r0).astype(F32)
        hit = pos_t[0:1] == rr
        for j in range(1, TOP_K):
            hit = hit | (pos_t[j:j + 1] == rr)
        buf[slot, r0:r0 + PERM_CHUNK, :] = _dot(jnp.where(hit, 1.0, 0.0).astype(BF16), xb)

    _segment_starts(tabs, step, n_exp, buf.at[slot], xs_hbm, sems.at[slot], True)

    @pl.when(step == pl.num_programs(0) - 1)
    def _():
        wait(step, slot)

        @pl.when(step >= 1)
        def _():
            wait(step - 1, 1 - slot)


def _seg_rows(td, n_exp):
    return -(-(td * TOP_K + n_exp * (SEG_ALIGN - 1)) // PERM_CHUNK) * PERM_CHUNK


def _dispatch(x2, pos, tabs, n_rows, n_exp):
    n_tok, d = x2.shape
    td = SEG_TILE
    grid_spec = pltpu.PrefetchScalarGridSpec(
        num_scalar_prefetch=len(tabs),
        grid=(n_tok // td,),
        in_specs=[
            pl.BlockSpec((td, d), lambda i, *_: (i, 0)),
            pl.BlockSpec((td, LANES), lambda i, *_: (i, 0)),
        ],
        out_specs=pl.BlockSpec(memory_space=pl.ANY),
        scratch_shapes=[pltpu.VMEM((2, _seg_rows(td, n_exp), d), F32), pltpu.SemaphoreType.DMA((2,))],
    )
    return pl.pallas_call(
        functools.partial(_dispatch_kernel, n_exp=n_exp),
        grid_spec=grid_spec,
        out_shape=jax.ShapeDtypeStruct((n_rows, d), F32),
        compiler_params=_cparams("arbitrary"),
        name="moe_dispatch",
    )(*tabs, x2, pos)


def _expert_kernel(be_ref, nu_ref, xs_ref, w1_ref, b1_ref, w2_ref, b2_ref, ys_ref):
    @pl.when(pl.program_id(0) < nu_ref[0])
    def _():
        f = w2_ref.shape[1]
        h = _dot(xs_ref[...].astype(BF16), w1_ref[0]) + b1_ref[0]
        glu = jnp.minimum(h[:, :f], SWIGLU_LIMIT)
        lin = jnp.clip(h[:, f:], -SWIGLU_LIMIT, SWIGLU_LIMIT)
        a = glu * _sigmoid(SWIGLU_ALPHA * glu) * (lin + 1.0)
        ys_ref[...] = _dot(a.astype(BF16), w2_ref[0]) + b2_ref[0]


def _experts(xs, block_e, n_used, w1, b1, w2, b2):
    n_exp, f, d = w2.shape
    blk = MOE_BLOCK
    n_blocks = xs.shape[0] // blk
    row_map = lambda i, be, nu: (jnp.minimum(i, nu[0] - 1), 0)
    exp_map = lambda i, be, nu: (be[i], 0, 0)
    grid_spec = pltpu.PrefetchScalarGridSpec(
        num_scalar_prefetch=2,
        grid=(n_blocks,),
        in_specs=[
            pl.BlockSpec((blk, d), row_map),
            pl.BlockSpec((1, d, 2 * f), exp_map),
            pl.BlockSpec((1, 1, 2 * f), exp_map),
            pl.BlockSpec((1, f, d), exp_map),
            pl.BlockSpec((1, 1, d), exp_map),
        ],
        out_specs=pl.BlockSpec((blk, d), row_map),
    )
    return pl.pallas_call(
        _expert_kernel,
        grid_spec=grid_spec,
        out_shape=jax.ShapeDtypeStruct(xs.shape, F32),
        compiler_params=_cparams("arbitrary"),
        name="moe_experts",
    )(block_e, n_used, xs, w1, b1.reshape(n_exp, 1, 2 * f), w2, b2.reshape(n_exp, 1, d))


def _combine_kernel(seg8_ref, start8_ref, gstart8_ref, tot8_ref, ys_hbm, x_ref, pos_ref, gate_ref, lg_ref,
                    lb_ref, out_ref, buf, sems, *, n_exp, dn_alpha):
    td = x_ref.shape[0]
    seg_rows = buf.shape[1]
    tabs = (seg8_ref, start8_ref, gstart8_ref, tot8_ref)
    step = pl.program_id(0)
    slot = step % 2

    def start(tile, sl):
        _segment_starts(tabs, tile, n_exp, buf.at[sl], ys_hbm, sems.at[sl], False)

    @pl.when(step == 0)
    def _():
        buf[...] = jnp.zeros_like(buf)
        start(step, slot)

    @pl.when(step + 1 < pl.num_programs(0))
    def _():
        start(step + 1, 1 - slot)

    _segment_wait(tabs, step, buf.at[slot], ys_hbm, sems.at[slot], False)

    pos = pos_ref[...]
    gates = gate_ref[...]
    acc = dn_alpha * x_ref[...]
    for r0 in range(0, seg_rows, PERM_CHUNK):
        cc = (lax.broadcasted_iota(jnp.int32, (td, PERM_CHUNK), 1) + r0).astype(F32)
        g = jnp.where(pos[:, 0:1] == cc, gates[:, 0:1], 0.0)
        for j in range(1, TOP_K):
            g = g + jnp.where(pos[:, j:j + 1] == cc, gates[:, j:j + 1], 0.0)
        acc = acc + _dot(g.astype(BF16), buf[slot, r0:r0 + PERM_CHUNK, :].astype(BF16))
    out_ref[...] = _layer_norm(acc, lg_ref[...], lb_ref[...])


def _combine_ln(ys, pos, gates, tabs, x2, ln_g, ln_b, n_exp, dn_alpha):
    n_tok, d = x2.shape
    td = SEG_TILE
    const = lambda i, *_: (0, 0)
    tile = lambda i, *_: (i, 0)
    grid_spec = pltpu.PrefetchScalarGridSpec(
        num_scalar_prefetch=len(tabs),
        grid=(n_tok // td,),
        in_specs=[
            pl.BlockSpec(memory_space=pl.ANY),
            pl.BlockSpec((td, d), tile),
            pl.BlockSpec((td, LANES), tile),
            pl.BlockSpec((td, LANES), tile),
            pl.BlockSpec((1, d), const),
            pl.BlockSpec((1, d), const),
        ],
        out_specs=pl.BlockSpec((td, d), tile),
        scratch_shapes=[pltpu.VMEM((2, _seg_rows(td, n_exp), d), F32), pltpu.SemaphoreType.DMA((2,))],
    )
    return pl.pallas_call(
        functools.partial(_combine_kernel, n_exp=n_exp, dn_alpha=dn_alpha),
        grid_spec=grid_spec,
        out_shape=jax.ShapeDtypeStruct((n_tok, d), F32),
        compiler_params=_cparams("arbitrary"),
        name="moe_combine_ln",
    )(*tabs, ys, x2, pos, gates, ln_g.reshape(1, d), ln_b.reshape(1, d))


def _w1_prep_kernel(w_ref, p_ref, out_ref):
    f = w_ref.shape[3] // 2
    width = p_ref.shape[0]
    for c in range(2 * f // width):
        t = _dot(w_ref[0, 0, :, c * width:(c + 1) * width].astype(BF16), p_ref[...])
        out_ref[0, :, c * (width // 2):(c + 1) * (width // 2)] = t[:, :width // 2].astype(BF16)
        out_ref[0, :, f + c * (width // 2):f + (c + 1) * (width // 2)] = t[:, width // 2:].astype(BF16)


def _w1_prep(w1, layer):
    _, n_exp, d, f2 = w1.shape
    width = 2 * LANES
    rows = ROW_TILE
    r = lax.broadcasted_iota(jnp.int32, (width, width), 0)
    c = lax.broadcasted_iota(jnp.int32, (width, width), 1)
    perm = (c == (r % 2) * (width // 2) + r // 2).astype(BF16)
    return pl.pallas_call(
        _w1_prep_kernel,
        grid=(n_exp, d // rows),
        in_specs=[
            pl.BlockSpec((1, 1, rows, f2), lambda e, i: (layer, e, i, 0)),
            pl.BlockSpec((width, width), lambda e, i: (0, 0)),
        ],
        out_specs=pl.BlockSpec((1, rows, f2), lambda e, i: (e, i, 0)),
        out_shape=jax.ShapeDtypeStruct((n_exp, d, f2), BF16),
        compiler_params=_cparams("parallel", "parallel"),
        name="moe_w1_prep",
    )(w1, perm)


def _moe_layer(x2, router_w, router_b, w1_all, layer, b1, w2, b2, ln_g, ln_b, dn_alpha):
    n_tok, d = x2.shape
    n_exp = w2.shape[0]
    n_tiles = n_tok // SEG_TILE
    blk8 = MOE_BLOCK // SEG_ALIGN
    n_blocks = -(-(n_tok * TOP_K + n_tiles * n_exp * (SEG_ALIGN - 1)) // MOE_BLOCK) + n_exp
    n_rows = n_blocks * MOE_BLOCK

    pos, gates, meta = _router(x2, router_w, router_b)

    meta = meta.reshape(n_tiles, ROW_SUBLANES, LANES)
    seg8 = meta[:, 0, :n_exp].astype(jnp.int32)
    start8 = meta[:, 1, :n_exp].astype(jnp.int32)
    padded8 = (jnp.sum(seg8, axis=0) + blk8 - 1) // blk8 * blk8
    pends8 = jnp.cumsum(padded8)
    gstart8 = (pends8 - padded8)[None, :] + jnp.cumsum(seg8, axis=0) - seg8
    block_start8 = jnp.arange(n_blocks, dtype=jnp.int32) * blk8
    block_e = jnp.minimum(jnp.sum((pends8[None, :] <= block_start8[:, None]).astype(jnp.int32), axis=1),
                          n_exp - 1).astype(jnp.int32)
    n_used = (pends8[-1:] // blk8).astype(jnp.int32)
    tabs = (seg8.reshape(-1), start8.reshape(-1), gstart8.reshape(-1).astype(jnp.int32),
            jnp.sum(seg8, axis=1).astype(jnp.int32))

    b1_l = jnp.concatenate([b1[:, 0::2], b1[:, 1::2]], axis=-1)

    xs = _dispatch(x2, pos, tabs, n_rows, n_exp)
    ys = _experts(xs, block_e, n_used, _w1_prep(w1_all, layer), b1_l, w2.astype(BF16), b2)
    return _combine_ln(ys, pos, gates, tabs, x2, ln_g, ln_b, n_exp, dn_alpha)


def _rope_lane_tables(positions):
    half = ROPE_DIM // 2
    inv_freq = ROPE_THETA ** (-jnp.arange(0, ROPE_DIM, 2, dtype=F32) / ROPE_DIM)
    ang = positions.astype(F32)[..., None] * inv_freq
    cos, sin = jnp.cos(ang), jnp.sin(ang)
    z = jnp.zeros_like(cos)
    assert 4 * half == LANES
    return (jnp.concatenate([cos, z, cos, z], axis=-1),
            jnp.concatenate([-sin, z, sin, z], axis=-1))


def _rope_lane_columns(w_rope):
    half = ROPE_DIM // 2
    z = jnp.zeros(w_rope.shape[:-1] + (half,), w_rope.dtype)
    return jnp.concatenate([w_rope[..., :half], z, w_rope[..., half:], z], axis=-1)


def kernel(x, positions, ln_g, ln_b, hg_w_in, hg_lb, hg_gnorm, hg_w_o, mla_w_dq, mla_q_norm,
           mla_w_uq, mla_w_o, kv_w_a, kv_norm, kv_w_b, router_w, router_b, moe_w1, moe_b1,
           moe_w2, moe_b2):
    bsz, s_len, d = x.shape
    depth = ln_g.shape[0]
    n_a = hg_w_in.shape[0]
    dn_alpha = (2.0 * depth) ** 0.25
    scale = (NOPE_DIM + ROPE_DIM) ** -0.5

    lb_soft = jax.nn.softmax(hg_lb.astype(F32), axis=0)
    lower_bounds = jnp.cumsum(lb_soft, axis=0) - lb_soft[0]

    q_rank = mla_w_uq.shape[1]
    mla_heads = mla_w_uq.shape[2] // (NOPE_DIM + ROPE_DIM)
    cc = ss = k_full = v_full = None

    for layer in range(depth):
        if layer < n_a:
            x = _hgrn_layer(x, hg_w_in[layer], lower_bounds[layer], hg_gnorm[layer], hg_w_o[layer],
                            ln_g[layer, 0], ln_b[layer, 0], dn_alpha)
        else:
            j = layer - n_a
            if layer == n_a:
                cc, ss = _rope_lane_tables(positions)
                kv_w_a_l = jnp.concatenate(
                    [kv_w_a[:, :KV_RANK], _rope_lane_columns(kv_w_a[:, KV_RANK:])], axis=-1).astype(BF16)
                k_full, v_full = _shared_kv(x, kv_w_a_l, kv_norm, kv_w_b, cc, ss)
            w_uq = mla_w_uq[j].reshape(q_rank, mla_heads, NOPE_DIM + ROPE_DIM)
            w_uq_l = jnp.concatenate(
                [w_uq[..., :NOPE_DIM], _rope_lane_columns(w_uq[..., NOPE_DIM:])],
                axis=-1).reshape(q_rank, mla_heads * QK_DIM).astype(BF16)
            q_full = _mla_queries(x, mla_w_dq[j], mla_q_norm[j], w_uq_l, cc, ss, scale)
            o = _mla_attention(q_full, k_full, v_full)
            x = _proj_residual_ln(o.reshape(bsz * s_len, -1), x.reshape(bsz * s_len, d), mla_w_o[j],
                                  ln_g[layer, 0], ln_b[layer, 0], dn_alpha).reshape(bsz, s_len, d)
        x = _moe_layer(x.reshape(bsz * s_len, d), router_w[layer], router_b[layer], moe_w1, layer,
                       moe_b1[layer], moe_w2[layer], moe_b2[layer], ln_g[layer, 1], ln_b[layer, 1],
                       dn_alpha).reshape(bsz, s_len, d)
    return x
```

```python
import functools

import jax
import jax.numpy as jnp
from jax import lax
from jax.experimental import pallas as pl
from jax.experimental.pallas import tpu as pltpu

F32 = jnp.float32
BF16 = jnp.bfloat16

HG_HEAD_DIM = 128
NOPE_DIM = 128
ROPE_DIM = 64
V_DIM = 128
VO_DIM = 256
LOG2_E = 1.4426950408889634
KV_RANK = 128
ROPE_THETA = 10000.0
TOP_K = 4
SWIGLU_ALPHA = 1.702
SWIGLU_LIMIT = 7.0
LN_EPS = 1e-5
RMS_EPS = 1e-6

LANES = 128
ROW_SUBLANES = 8
QK_DIM = 256
VMEM_LIMIT = 56 * 1024 * 1024

ROW_TILE = 256
HG_GROUP = 128
HG_HEAD_UNROLL = 8
ATT_TQ = 512
MOE_BLOCK = 512
SEG_TILE = 512
SEG_ALIGN = ROW_SUBLANES
SEG_BITS = (SEG_TILE * TOP_K // SEG_ALIGN).bit_length()
SEG_LOW_BITS = 4
PERM_CHUNK = 256


def _cparams(*sem):
    return pltpu.CompilerParams(dimension_semantics=sem, vmem_limit_bytes=VMEM_LIMIT)


def _layer_norm(y, g, b):
    mu = jnp.mean(y, axis=-1, keepdims=True)
    d = y - mu
    var = jnp.mean(d * d, axis=-1, keepdims=True)
    return d * lax.rsqrt(var + LN_EPS) * g + b


def _dot(a, b):
    return jnp.dot(a, b, preferred_element_type=F32)


def _dot_nt(a, b):
    return lax.dot_general(a, b, (((1,), (1,)), ((), ())), preferred_element_type=F32)


def _dot_tn(a, b):
    return lax.dot_general(a, b, (((0,), (0,)), ((), ())), preferred_element_type=F32)


def _sigmoid(x):
    return 1.0 / (1.0 + jnp.exp(-x))


def _hgrn_kernel(x_ref, w_in_ref, lb_ref, gn_ref, w_o_ref, lg_ref, lbias_ref, out_ref,
                 q_s, f_s, i_s, g_s, mix_s, st_s, *, dn_alpha):
    ts, d = x_ref.shape[1], x_ref.shape[2]
    n_heads = d // HG_HEAD_DIM
    grp = HG_GROUP

    @pl.when(pl.program_id(1) == 0)
    def _():
        st_s[...] = jnp.zeros_like(st_s)

    x = x_ref[0]
    xb = x.astype(BF16)
    lb = lb_ref[...]

    for sec, dst in enumerate((q_s, f_s, i_s, g_s)):
        p = _dot(xb, w_in_ref[:, sec * d:(sec + 1) * d])
        if sec == 0 or sec == 3:
            p = p * _sigmoid(p)
        elif sec == 1:
            p = lb + (1.0 - lb) * _sigmoid(p)
        for h in range(n_heads):
            dst[h] = p[:, h * HG_HEAD_DIM:(h + 1) * HG_HEAD_DIM]

    row = lax.broadcasted_iota(jnp.int32, (grp, grp), 0)
    col = lax.broadcasted_iota(jnp.int32, (grp, grp), 1)
    n_levels = grp.bit_length() - 1
    pair_masks = [(((row >> lvl) ^ (col >> lvl)) == 1) & (row > col) for lvl in range(n_levels)]

    def head_body(h, carry):
        for r0 in range(0, ts, grp):
            q = q_s[h, r0:r0 + grp, :]
            fg = f_s[h, r0:r0 + grp, :]
            iv = i_s[h, r0:r0 + grp, :].astype(BF16)
            k = 1.0 - fg
            ep, es, et = fg, None, fg
            a = jnp.where(row == col, _dot_nt(q.astype(BF16), k.astype(BF16)), 0.0)
            for lvl in range(n_levels):
                half = 1 << lvl
                kl = k if es is None else k * es
                a = jnp.where(pair_masks[lvl], _dot_nt((q * ep).astype(BF16), kl.astype(BF16)), a)
                if half < ROW_SUBLANES:
                    odd = (row & half) != 0
                    et3 = et.reshape(grp // ROW_SUBLANES, ROW_SUBLANES, LANES)
                    other = jnp.where(odd, pltpu.roll(et3, half, 1).reshape(et.shape),
                                      pltpu.roll(et3, ROW_SUBLANES - half, 1).reshape(et.shape))
                    ep = jnp.where(odd, ep * other, ep)
                    es = jnp.where(odd, 1.0, other) if es is None else jnp.where(odd, es, es * other)
                    et = et * other
                else:
                    ep_p, es_p, et_p = [], [], []
                    for b0 in range(0, grp, 2 * half):
                        lo, mid, hi = b0, b0 + half, b0 + 2 * half
                        tot = et[lo:mid] * et[mid:hi]
                        ep_p += [ep[lo:mid], ep[mid:hi] * et[lo:mid]]
                        es_p += [es[lo:mid] * et[mid:hi], es[mid:hi]]
                        et_p += [tot, tot]
                    ep = jnp.concatenate(ep_p, axis=0)
                    es = jnp.concatenate(es_p, axis=0)
                    et = jnp.concatenate(et_p, axis=0)
            st = st_s[h]
            o = _dot(a.astype(BF16), iv) + _dot_nt((q * ep).astype(BF16), st.astype(BF16))
            st_s[h] = st * et[0:1, :] + _dot_tn(iv, (k * es).astype(BF16))
            ms = jnp.mean(o * o, axis=-1, keepdims=True)
            y = o * lax.rsqrt(ms + RMS_EPS) * gn_ref[...] * g_s[h, r0:r0 + grp, :]
            mix_s[h, r0:r0 + grp, :] = y.astype(BF16)
        return carry

    lax.fori_loop(0, n_heads, head_body, 0, unroll=HG_HEAD_UNROLL)

    acc = dn_alpha * x
    for h in range(0, n_heads, 2):
        acc = acc + _dot(jnp.concatenate([mix_s[h], mix_s[h + 1]], axis=-1), w_o_ref[h // 2])
    out_ref[0] = _layer_norm(acc, lg_ref[...], lbias_ref[...])


def _hgrn_layer(x, w_in, lb, gnorm, w_o, ln_g, ln_b, dn_alpha):
    bsz, s_len, d = x.shape
    n_heads = d // HG_HEAD_DIM
    ts = ROW_TILE
    const2 = lambda b, s: (0, 0)
    return pl.pallas_call(
        functools.partial(_hgrn_kernel, dn_alpha=dn_alpha),
        grid=(bsz, s_len // ts),
        in_specs=[
            pl.BlockSpec((1, ts, d), lambda b, s: (b, s, 0)),
            pl.BlockSpec((d, 4 * d), const2),
            pl.BlockSpec((1, d), const2),
            pl.BlockSpec((1, HG_HEAD_DIM), const2),
            pl.BlockSpec((n_heads // 2, 2 * HG_HEAD_DIM, d), lambda b, s: (0, 0, 0)),
            pl.BlockSpec((1, d), const2),
            pl.BlockSpec((1, d), const2),
        ],
        out_specs=pl.BlockSpec((1, ts, d), lambda b, s: (b, s, 0)),
        out_shape=jax.ShapeDtypeStruct((bsz, s_len, d), F32),
        scratch_shapes=[
            pltpu.VMEM((n_heads, ts, HG_HEAD_DIM), F32),
            pltpu.VMEM((n_heads, ts, HG_HEAD_DIM), F32),
            pltpu.VMEM((n_heads, ts, HG_HEAD_DIM), F32),
            pltpu.VMEM((n_heads, ts, HG_HEAD_DIM), F32),
            pltpu.VMEM((n_heads, ts, HG_HEAD_DIM), BF16),
            pltpu.VMEM((n_heads, HG_HEAD_DIM, HG_HEAD_DIM), F32),
        ],
        compiler_params=_cparams("parallel", "arbitrary"),
        name="hgrn2_layer",
    )(x, w_in.astype(BF16), lb.reshape(1, d), gnorm.reshape(1, HG_HEAD_DIM),
      w_o.astype(BF16).reshape(n_heads // 2, 2 * HG_HEAD_DIM, d), ln_g.reshape(1, d), ln_b.reshape(1, d))


def _rope_lanes(t, cc, ss):
    return t * cc + pltpu.roll(t, LANES // 2, 1) * ss


def _kv_kernel(x_ref, wa_ref, kvn_ref, wb_ref, cc_ref, ss_ref, k_ref, v_ref):
    n_heads = k_ref.shape[1]
    xb = x_ref[0].astype(BF16)
    ckr = _dot(xb, wa_ref[...])
    c = ckr[:, :KV_RANK]
    c = c * lax.rsqrt(jnp.mean(c * c, axis=-1, keepdims=True) + RMS_EPS) * kvn_ref[...]
    kr = _rope_lanes(ckr[:, KV_RANK:], cc_ref[0], ss_ref[0]).astype(BF16)
    cb = c.astype(BF16)
    for h in range(n_heads):
        kv = _dot(cb, wb_ref[:, h * (NOPE_DIM + V_DIM):(h + 1) * (NOPE_DIM + V_DIM)])
        k_ref[0, h, :, :NOPE_DIM] = kv[:, :NOPE_DIM].astype(BF16)
        k_ref[0, h, :, NOPE_DIM:] = kr
        v_ref[0, h, :, :V_DIM] = kv[:, NOPE_DIM:].astype(BF16)
        v_ref[0, h, :, V_DIM:] = jnp.ones((kv.shape[0], VO_DIM - V_DIM), BF16)


def _shared_kv(x, kv_w_a_l, kv_norm, kv_w_b, cc, ss):
    bsz, s_len, d = x.shape
    n_heads = kv_w_b.shape[1] // (NOPE_DIM + V_DIM)
    ts = ROW_TILE
    const2 = lambda b, s: (0, 0)
    return pl.pallas_call(
        _kv_kernel,
        grid=(bsz, s_len // ts),
        in_specs=[
            pl.BlockSpec((1, ts, d), lambda b, s: (b, s, 0)),
            pl.BlockSpec((d, KV_RANK + LANES), const2),
            pl.BlockSpec((1, KV_RANK), const2),
            pl.BlockSpec((KV_RANK, n_heads * (NOPE_DIM + V_DIM)), const2),
            pl.BlockSpec((1, ts, LANES), lambda b, s: (b, s, 0)),
            pl.BlockSpec((1, ts, LANES), lambda b, s: (b, s, 0)),
        ],
        out_specs=[
            pl.BlockSpec((1, n_heads, ts, QK_DIM), lambda b, s: (b, 0, s, 0)),
            pl.BlockSpec((1, n_heads, ts, VO_DIM), lambda b, s: (b, 0, s, 0)),
        ],
        out_shape=[
            jax.ShapeDtypeStruct((bsz, n_heads, s_len, QK_DIM), BF16),
            jax.ShapeDtypeStruct((bsz, n_heads, s_len, VO_DIM), BF16),
        ],
        compiler_params=_cparams("parallel", "parallel"),
        name="mla_shared_kv",
    )(x, kv_w_a_l, kv_norm.reshape(1, KV_RANK), kv_w_b.astype(BF16), cc, ss)


def _q_kernel(x_ref, wdq_ref, qn_ref, wuq_ref, cc_ref, ss_ref, q_ref, *, scale):
    n_heads = q_ref.shape[1]
    xb = x_ref[0].astype(BF16)
    c = _dot(xb, wdq_ref[...])
    c = c * lax.rsqrt(jnp.mean(c * c, axis=-1, keepdims=True) + RMS_EPS) * qn_ref[...]
    cb = c.astype(BF16)
    cc = cc_ref[0] * scale
    ss = ss_ref[0] * scale
    for h in range(n_heads):
        qh = _dot(cb, wuq_ref[:, h * QK_DIM:(h + 1) * QK_DIM])
        q_ref[0, h, :, :NOPE_DIM] = (qh[:, :NOPE_DIM] * scale).astype(BF16)
        q_ref[0, h, :, NOPE_DIM:] = _rope_lanes(qh[:, NOPE_DIM:], cc, ss).astype(BF16)


def _mla_queries(x, w_dq, q_norm, w_uq_l, cc, ss, scale):
    bsz, s_len, d = x.shape
    q_rank = w_dq.shape[1]
    n_heads = w_uq_l.shape[1] // QK_DIM
    ts = ROW_TILE
    const2 = lambda b, s: (0, 0)
    return pl.pallas_call(
        functools.partial(_q_kernel, scale=scale),
        grid=(bsz, s_len // ts),
        in_specs=[
            pl.BlockSpec((1, ts, d), lambda b, s: (b, s, 0)),
            pl.BlockSpec((d, q_rank), const2),
            pl.BlockSpec((1, q_rank), const2),
            pl.BlockSpec((q_rank, n_heads * QK_DIM), const2),
            pl.BlockSpec((1, ts, LANES), lambda b, s: (b, s, 0)),
            pl.BlockSpec((1, ts, LANES), lambda b, s: (b, s, 0)),
        ],
        out_specs=pl.BlockSpec((1, n_heads, ts, QK_DIM), lambda b, s: (b, 0, s, 0)),
        out_shape=jax.ShapeDtypeStruct((bsz, n_heads, s_len, QK_DIM), BF16),
        compiler_params=_cparams("parallel", "parallel"),
        name="mla_queries",
    )(x, w_dq.astype(BF16), q_norm.reshape(1, q_rank), w_uq_l, cc, ss)


def _attn_kernel(q_ref, k_ref, v_ref, o_ref):
    s_len = q_ref.shape[2]
    tq = min(ATT_TQ, s_len)
    row = lax.broadcasted_iota(jnp.int32, (tq, tq), 0)
    col = lax.broadcasted_iota(jnp.int32, (tq, tq), 1)
    for qi in range(s_len // tq):
        q = q_ref[0, 0, qi * tq:(qi + 1) * tq, :]
        m = jnp.full((tq, 1), -jnp.inf, F32)
        acc = jnp.zeros((tq, VO_DIM), F32)
        for kj in range(qi + 1):
            s = _dot_nt(q, k_ref[0, 0, kj * tq:(kj + 1) * tq, :])
            if kj == qi:
                s = jnp.where(col <= row, s, -jnp.inf)
            m_new = jnp.maximum(m, jnp.max(s, axis=-1, keepdims=True))
            p = jnp.exp2(s - m_new)
            acc = jnp.exp2(m - m_new) * acc + _dot(p.astype(BF16), v_ref[0, 0, kj * tq:(kj + 1) * tq, :])
            m = m_new
        o_ref[0, qi * tq:(qi + 1) * tq, :] = (acc[:, :V_DIM] / acc[:, V_DIM:V_DIM + 1]).astype(o_ref.dtype)


def _mla_attention(q, k, v):
    bsz, n_heads, s_len, _ = q.shape
    return pl.pallas_call(
        _attn_kernel,
        grid=(bsz, n_heads),
        in_specs=[
            pl.BlockSpec((1, 1, s_len, QK_DIM), lambda b, h: (b, h, 0, 0)),
            pl.BlockSpec((1, 1, s_len, QK_DIM), lambda b, h: (b, h, 0, 0)),
            pl.BlockSpec((1, 1, s_len, VO_DIM), lambda b, h: (b, h, 0, 0)),
        ],
        out_specs=pl.BlockSpec((1, s_len, V_DIM), lambda b, h: (b, 0, h)),
        out_shape=jax.ShapeDtypeStruct((bsz, s_len, n_heads * V_DIM), BF16),
        compiler_params=_cparams("parallel", "parallel"),
        name="mla_attention",
    )(q, k, v)


def _proj_ln_kernel(o_ref, x_ref, w_ref, lg_ref, lb_ref, out_ref, *, dn_alpha):
    y = dn_alpha * x_ref[...] + _dot(o_ref[...], w_ref[...])
    out_ref[...] = _layer_norm(y, lg_ref[...], lb_ref[...])


def _proj_residual_ln(o2, x2, w_o, ln_g, ln_b, dn_alpha):
    n_tok, d = x2.shape
    kdim = o2.shape[1]
    ts = ROW_TILE
    const = lambda i: (0, 0)
    return pl.pallas_call(
        functools.partial(_proj_ln_kernel, dn_alpha=dn_alpha),
        grid=(n_tok // ts,),
        in_specs=[
            pl.BlockSpec((ts, kdim), lambda i: (i, 0)),
            pl.BlockSpec((ts, d), lambda i: (i, 0)),
            pl.BlockSpec((kdim, d), const),
            pl.BlockSpec((1, d), const),
            pl.BlockSpec((1, d), const),
        ],
        out_specs=pl.BlockSpec((ts, d), lambda i: (i, 0)),
        out_shape=jax.ShapeDtypeStruct((n_tok, d), F32),
        compiler_params=_cparams("parallel"),
        name="mla_out_proj_ln",
    )(o2, x2, w_o.astype(BF16), ln_g.reshape(1, d), ln_b.reshape(1, d))


def _split_hi_lo(v):
    hi = lax.bitcast_convert_type(lax.bitcast_convert_type(v, jnp.uint32) & jnp.uint32(0xFFFF0000), F32)
    return hi.astype(BF16), (v - hi).astype(BF16)


def _router_kernel(x_ref, whi_ref, wlo_ref, b_ref, pos_ref, gate_ref, meta_ref):
    tr = x_ref.shape[0]

    x_hi, x_lo = _split_hi_lo(x_ref[...])
    logits = (_dot(x_hi, whi_ref[...]) + (_dot(x_lo, whi_ref[...]) + _dot(x_hi, wlo_ref[...]))
              + b_ref[...])
    lane = lax.broadcasted_iota(jnp.int32, (tr, LANES), 1)
    work = logits
    sel = jnp.zeros((tr, LANES), F32)
    ids, vals = [], []
    for _ in range(TOP_K):
        mx = jnp.max(work, axis=-1, keepdims=True)
        idx = jnp.min(jnp.where(work == mx, lane, LANES), axis=-1, keepdims=True)
        hit = lane == idx
        ids.append(idx)
        vals.append(mx)
        sel = jnp.where(hit, 1.0, sel)
        work = jnp.where(hit, -jnp.inf, work)
    exps = [jnp.exp(v - vals[0]) for v in vals]
    denom = exps[0] + exps[1] + exps[2] + exps[3]

    r = lax.broadcasted_iota(jnp.int32, (tr, tr), 0)
    c = lax.broadcasted_iota(jnp.int32, (tr, tr), 1)
    before = _dot(jnp.where(c < r, 1.0, 0.0).astype(BF16), sel.astype(BF16))
    seg8 = jnp.floor((jnp.sum(sel, axis=0, keepdims=True) + (SEG_ALIGN - 1)) * (1.0 / SEG_ALIGN))
    er = lax.broadcasted_iota(jnp.int32, (LANES, LANES), 0)
    ec = lax.broadcasted_iota(jnp.int32, (LANES, LANES), 1)
    start8 = _dot(jnp.broadcast_to(seg8, (ROW_SUBLANES, LANES)).astype(BF16),
                  jnp.where(er < ec, 1.0, 0.0).astype(BF16))[0:1]
    slot = start8 * SEG_ALIGN + before

    pos_out = jnp.zeros((tr, LANES), F32)
    gate_out = jnp.zeros((tr, LANES), F32)
    for j in range(TOP_K):
        pos = jnp.sum(jnp.where(lane == ids[j], slot, 0.0), axis=-1, keepdims=True)
        pos_out = jnp.where(lane == j, pos, pos_out)
        gate_out = jnp.where(lane == j, exps[j] / denom, gate_out)
    pos_ref[...] = pos_out
    gate_ref[...] = gate_out
    row8 = lax.broadcasted_iota(jnp.int32, (ROW_SUBLANES, LANES), 0)
    meta_ref[...] = jnp.where(row8 == 0, seg8, jnp.where(row8 == 1, start8, 0.0))


def _router(x2, router_w, router_b):
    n_tok, d = x2.shape
    n_exp = router_w.shape[1]
    tr = SEG_TILE
    w_hi, w_lo = _split_hi_lo(jnp.zeros((d, LANES), F32).at[:, :n_exp].set(router_w))
    b = jnp.full((1, LANES), -jnp.inf, F32).at[0, :n_exp].set(router_b)
    const = lambda i: (0, 0)
    return pl.pallas_call(
        _router_kernel,
        grid=(n_tok // tr,),
        in_specs=[
            pl.BlockSpec((tr, d), lambda i: (i, 0)),
            pl.BlockSpec((d, LANES), const),
            pl.BlockSpec((d, LANES), const),
            pl.BlockSpec((1, LANES), const),
        ],
        out_specs=[
            pl.BlockSpec((tr, LANES), lambda i: (i, 0)),
            pl.BlockSpec((tr, LANES), lambda i: (i, 0)),
            pl.BlockSpec((ROW_SUBLANES, LANES), lambda i: (i, 0)),
        ],
        out_shape=[
            jax.ShapeDtypeStruct((n_tok, LANES), F32),
            jax.ShapeDtypeStruct((n_tok, LANES), F32),
            jax.ShapeDtypeStruct((n_tok // tr * ROW_SUBLANES, LANES), F32),
        ],
        compiler_params=_cparams("parallel"),
        name="moe_router",
    )(x2, w_hi, w_lo, b)


def _segment_copy(local_ref, hbm_ref, sem, to_hbm, lo8, go8, bit):
    rows = SEG_ALIGN << bit
    aligned = lambda v8: v8 * SEG_ALIGN if isinstance(v8, int) else pl.multiple_of(v8 * SEG_ALIGN, SEG_ALIGN)
    loc = local_ref.at[pl.ds(aligned(lo8), rows)]
    hbm = hbm_ref.at[pl.ds(aligned(go8), rows)]
    return pltpu.make_async_copy(loc, hbm, sem) if to_hbm else pltpu.make_async_copy(hbm, loc, sem)


def _segment_starts(tabs, tile, n_exp, local_ref, hbm_ref, sem, to_hbm):
    seg8_ref, start8_ref, gstart8_ref, _ = tabs

    def body(e, c):
        n = seg8_ref[tile * n_exp + e]
        lo = start8_ref[tile * n_exp + e]
        go = gstart8_ref[tile * n_exp + e]

        def bits(lo_bit, hi_bit):
            for bit in range(lo_bit, hi_bit):
                @pl.when(((n >> bit) & 1) == 1)
                def _():
                    off = (n >> (bit + 1)) << (bit + 1)
                    _segment_copy(local_ref, hbm_ref, sem, to_hbm, lo + off, go + off, bit).start()

        bits(0, SEG_LOW_BITS)

        @pl.when(n >= (1 << SEG_LOW_BITS))
        def _():
            bits(SEG_LOW_BITS, SEG_BITS)
        return c

    lax.fori_loop(0, n_exp, body, 0)


def _segment_wait(tabs, tile, local_ref, hbm_ref, sem, to_hbm):
    total = tabs[3][tile]
    for bit in range(SEG_BITS):
        @pl.when(((total >> bit) & 1) == 1)
        def _():
            _segment_copy(local_ref, hbm_ref, sem, to_hbm, 0, 0, bit).wait()


def _dispatch_kernel(seg8_ref, start8_ref, gstart8_ref, tot8_ref, x_ref, pos_ref, xs_hbm, buf, sems, *, n_exp):
    td = x_ref.shape[0]
    seg_rows = buf.shape[1]
    tabs = (seg8_ref, start8_ref, gstart8_ref, tot8_ref)
    step = pl.program_id(0)
    slot = step % 2

    def wait(tile, sl):
        _segment_wait(tabs, tile, buf.at[sl], xs_hbm, sems.at[sl], True)

    @pl.when(step >= 2)
    def _():
        wait(step - 2, slot)

    xb = x_ref[...].astype(BF16)
    pos_t = pos_ref[...].T
    for r0 in range(0, seg_rows, PERM_CHUNK):
        rr = (lax.broadcasted_iota(jnp.int32, (PERM_CHUNK, td), 0) + r0).astype(F32)
        hit = pos_t[0:1] == rr
        for j in range(1, TOP_K):
            hit = hit | (pos_t[j:j + 1] == rr)
        buf[slot, r0:r0 + PERM_CHUNK, :] = _dot(jnp.where(hit, 1.0, 0.0).astype(BF16), xb)

    _segment_starts(tabs, step, n_exp, buf.at[slot], xs_hbm, sems.at[slot], True)

    @pl.when(step == pl.num_programs(0) - 1)
    def _():
        wait(step, slot)

        @pl.when(step >= 1)
        def _():
            wait(step - 1, 1 - slot)


def _seg_rows(td, n_exp):
    return -(-(td * TOP_K + n_exp * (SEG_ALIGN - 1)) // PERM_CHUNK) * PERM_CHUNK


def _dispatch(x2, pos, tabs, n_rows, n_exp):
    n_tok, d = x2.shape
    td = SEG_TILE
    grid_spec = pltpu.PrefetchScalarGridSpec(
        num_scalar_prefetch=len(tabs),
        grid=(n_tok // td,),
        in_specs=[
            pl.BlockSpec((td, d), lambda i, *_: (i, 0)),
            pl.BlockSpec((td, LANES), lambda i, *_: (i, 0)),
        ],
        out_specs=pl.BlockSpec(memory_space=pl.ANY),
        scratch_shapes=[pltpu.VMEM((2, _seg_rows(td, n_exp), d), F32), pltpu.SemaphoreType.DMA((2,))],
    )
    return pl.pallas_call(
        functools.partial(_dispatch_kernel, n_exp=n_exp),
        grid_spec=grid_spec,
        out_shape=jax.ShapeDtypeStruct((n_rows, d), F32),
        compiler_params=_cparams("arbitrary"),
        name="moe_dispatch",
    )(*tabs, x2, pos)


def _expert_kernel(be_ref, nu_ref, xs_ref, w1_ref, b1_ref, w2_ref, b2_ref, ys_ref):
    @pl.when(pl.program_id(0) < nu_ref[0])
    def _():
        f = w2_ref.shape[1]
        h = _dot(xs_ref[...].astype(BF16), w1_ref[0]) + b1_ref[0]
        glu = jnp.minimum(h[:, :f], SWIGLU_LIMIT)
        lin = jnp.clip(h[:, f:], -SWIGLU_LIMIT, SWIGLU_LIMIT)
        a = glu * _sigmoid(SWIGLU_ALPHA * glu) * (lin + 1.0)
        ys_ref[...] = _dot(a.astype(BF16), w2_ref[0]) + b2_ref[0]


def _experts(xs, block_e, n_used, w1, b1, w2, b2):
    n_exp, f, d = w2.shape
    blk = MOE_BLOCK
    n_blocks = xs.shape[0] // blk
    row_map = lambda i, be, nu: (jnp.minimum(i, nu[0] - 1), 0)
    exp_map = lambda i, be, nu: (be[i], 0, 0)
    grid_spec = pltpu.PrefetchScalarGridSpec(
        num_scalar_prefetch=2,
        grid=(n_blocks,),
        in_specs=[
            pl.BlockSpec((blk, d), row_map),
            pl.BlockSpec((1, d, 2 * f), exp_map),
            pl.BlockSpec((1, 1, 2 * f), exp_map),
            pl.BlockSpec((1, f, d), exp_map),
            pl.BlockSpec((1, 1, d), exp_map),
        ],
        out_specs=pl.BlockSpec((blk, d), row_map),
    )
    return pl.pallas_call(
        _expert_kernel,
        grid_spec=grid_spec,
        out_shape=jax.ShapeDtypeStruct(xs.shape, F32),
        compiler_params=_cparams("arbitrary"),
        name="moe_experts",
    )(block_e, n_used, xs, w1, b1.reshape(n_exp, 1, 2 * f), w2, b2.reshape(n_exp, 1, d))


def _combine_kernel(seg8_ref, start8_ref, gstart8_ref, tot8_ref, ys_hbm, x_ref, pos_ref, gate_ref, lg_ref,
                    lb_ref, out_ref, buf, sems, *, n_exp, dn_alpha):
    td = x_ref.shape[0]
    seg_rows = buf.shape[1]
    tabs = (seg8_ref, start8_ref, gstart8_ref, tot8_ref)
    step = pl.program_id(0)
    slot = step % 2

    def start(tile, sl):
        _segment_starts(tabs, tile, n_exp, buf.at[sl], ys_hbm, sems.at[sl], False)

    @pl.when(step == 0)
    def _():
        buf[...] = jnp.zeros_like(buf)
        start(step, slot)

    @pl.when(step + 1 < pl.num_programs(0))
    def _():
        start(step + 1, 1 - slot)

    _segment_wait(tabs, step, buf.at[slot], ys_hbm, sems.at[slot], False)

    pos = pos_ref[...]
    gates = gate_ref[...]
    acc = dn_alpha * x_ref[...]
    for r0 in range(0, seg_rows, PERM_CHUNK):
        cc = (lax.broadcasted_iota(jnp.int32, (td, PERM_CHUNK), 1) + r0).astype(F32)
        g = jnp.zeros((td, PERM_CHUNK), F32)
        for j in range(TOP_K):
            g = jnp.where(pos[:, j:j + 1] == cc, gates[:, j:j + 1], g)
        acc = acc + _dot(g.astype(BF16), buf[slot, r0:r0 + PERM_CHUNK, :].astype(BF16))
    out_ref[...] = _layer_norm(acc, lg_ref[...], lb_ref[...])


def _combine_ln(ys, pos, gates, tabs, x2, ln_g, ln_b, n_exp, dn_alpha):
    n_tok, d = x2.shape
    td = SEG_TILE
    const = lambda i, *_: (0, 0)
    tile = lambda i, *_: (i, 0)
    grid_spec = pltpu.PrefetchScalarGridSpec(
        num_scalar_prefetch=len(tabs),
        grid=(n_tok // td,),
        in_specs=[
            pl.BlockSpec(memory_space=pl.ANY),
            pl.BlockSpec((td, d), tile),
            pl.BlockSpec((td, LANES), tile),
            pl.BlockSpec((td, LANES), tile),
            pl.BlockSpec((1, d), const),
            pl.BlockSpec((1, d), const),
        ],
        out_specs=pl.BlockSpec((td, d), tile),
        scratch_shapes=[pltpu.VMEM((2, _seg_rows(td, n_exp), d), F32), pltpu.SemaphoreType.DMA((2,))],
    )
    return pl.pallas_call(
        functools.partial(_combine_kernel, n_exp=n_exp, dn_alpha=dn_alpha),
        grid_spec=grid_spec,
        out_shape=jax.ShapeDtypeStruct((n_tok, d), F32),
        compiler_params=_cparams("arbitrary"),
        name="moe_combine_ln",
    )(*tabs, ys, x2, pos, gates, ln_g.reshape(1, d), ln_b.reshape(1, d))


def _w1_prep_kernel(w_ref, p_ref, out_ref):
    f = w_ref.shape[3] // 2
    width = p_ref.shape[0]
    for c in range(2 * f // width):
        t = _dot(w_ref[0, 0, :, c * width:(c + 1) * width].astype(BF16), p_ref[...])
        out_ref[0, :, c * (width // 2):(c + 1) * (width // 2)] = t[:, :width // 2].astype(BF16)
        out_ref[0, :, f + c * (width // 2):f + (c + 1) * (width // 2)] = t[:, width // 2:].astype(BF16)


def _w1_prep(w1, layer):
    _, n_exp, d, f2 = w1.shape
    width = 2 * LANES
    rows = ROW_TILE
    r = lax.broadcasted_iota(jnp.int32, (width, width), 0)
    c = lax.broadcasted_iota(jnp.int32, (width, width), 1)
    perm = (c == (r % 2) * (width // 2) + r // 2).astype(BF16)
    return pl.pallas_call(
        _w1_prep_kernel,
        grid=(n_exp, d // rows),
        in_specs=[
            pl.BlockSpec((1, 1, rows, f2), lambda e, i: (layer, e, i, 0)),
            pl.BlockSpec((width, width), lambda e, i: (0, 0)),
        ],
        out_specs=pl.BlockSpec((1, rows, f2), lambda e, i: (e, i, 0)),
        out_shape=jax.ShapeDtypeStruct((n_exp, d, f2), BF16),
        compiler_params=_cparams("parallel", "parallel"),
        name="moe_w1_prep",
    )(w1, perm)


def _moe_layer(x2, router_w, router_b, w1_all, layer, b1, w2, b2, ln_g, ln_b, dn_alpha):
    n_tok, d = x2.shape
    n_exp = w2.shape[0]
    n_tiles = n_tok // SEG_TILE
    blk8 = MOE_BLOCK // SEG_ALIGN
    n_blocks = -(-(n_tok * TOP_K + n_tiles * n_exp * (SEG_ALIGN - 1)) // MOE_BLOCK) + n_exp
    n_rows = n_blocks * MOE_BLOCK

    pos, gates, meta = _router(x2, router_w, router_b)

    meta = meta.reshape(n_tiles, ROW_SUBLANES, LANES)
    seg8 = meta[:, 0, :n_exp].astype(jnp.int32)
    start8 = meta[:, 1, :n_exp].astype(jnp.int32)
    padded8 = (jnp.sum(seg8, axis=0) + blk8 - 1) // blk8 * blk8
    pends8 = jnp.cumsum(padded8)
    gstart8 = (pends8 - padded8)[None, :] + jnp.cumsum(seg8, axis=0) - seg8
    block_start8 = jnp.arange(n_blocks, dtype=jnp.int32) * blk8
    block_e = jnp.minimum(jnp.sum((pends8[None, :] <= block_start8[:, None]).astype(jnp.int32), axis=1),
                          n_exp - 1).astype(jnp.int32)
    n_used = (pends8[-1:] // blk8).astype(jnp.int32)
    tabs = (seg8.reshape(-1), start8.reshape(-1), gstart8.reshape(-1).astype(jnp.int32),
            jnp.sum(seg8, axis=1).astype(jnp.int32))

    b1_l = jnp.concatenate([b1[:, 0::2], b1[:, 1::2]], axis=-1)

    xs = _dispatch(x2, pos, tabs, n_rows, n_exp)
    ys = _experts(xs, block_e, n_used, _w1_prep(w1_all, layer), b1_l, w2.astype(BF16), b2)
    return _combine_ln(ys, pos, gates, tabs, x2, ln_g, ln_b, n_exp, dn_alpha)


def _rope_lane_tables(positions):
    half = ROPE_DIM // 2
    inv_freq = ROPE_THETA ** (-jnp.arange(0, ROPE_DIM, 2, dtype=F32) / ROPE_DIM)
    ang = positions.astype(F32)[..., None] * inv_freq
    cos, sin = jnp.cos(ang), jnp.sin(ang)
    z = jnp.zeros_like(cos)
    assert 4 * half == LANES
    return (jnp.concatenate([cos, z, cos, z], axis=-1),
            jnp.concatenate([-sin, z, sin, z], axis=-1))


def _rope_lane_columns(w_rope):
    half = ROPE_DIM // 2
    z = jnp.zeros(w_rope.shape[:-1] + (half,), w_rope.dtype)
    return jnp.concatenate([w_rope[..., :half], z, w_rope[..., half:], z], axis=-1)


def kernel(x, positions, ln_g, ln_b, hg_w_in, hg_lb, hg_gnorm, hg_w_o, mla_w_dq, mla_q_norm,
           mla_w_uq, mla_w_o, kv_w_a, kv_norm, kv_w_b, router_w, router_b, moe_w1, moe_b1,
           moe_w2, moe_b2):
    bsz, s_len, d = x.shape
    depth = ln_g.shape[0]
    n_a = hg_w_in.shape[0]
    dn_alpha = (2.0 * depth) ** 0.25
    scale = (NOPE_DIM + ROPE_DIM) ** -0.5 * LOG2_E

    lb_soft = jax.nn.softmax(hg_lb.astype(F32), axis=0)
    lower_bounds = jnp.cumsum(lb_soft, axis=0) - lb_soft[0]

    q_rank = mla_w_uq.shape[1]
    mla_heads = mla_w_uq.shape[2] // (NOPE_DIM + ROPE_DIM)
    cc = ss = k_full = v_full = None

    for layer in range(depth):
        if layer < n_a:
            x = _hgrn_layer(x, hg_w_in[layer], lower_bounds[layer], hg_gnorm[layer], hg_w_o[layer],
                            ln_g[layer, 0], ln_b[layer, 0], dn_alpha)
        else:
            j = layer - n_a
            if layer == n_a:
                cc, ss = _rope_lane_tables(positions)
                kv_w_a_l = jnp.concatenate(
                    [kv_w_a[:, :KV_RANK], _rope_lane_columns(kv_w_a[:, KV_RANK:])], axis=-1).astype(BF16)
                k_full, v_full = _shared_kv(x, kv_w_a_l, kv_norm, kv_w_b, cc, ss)
            w_uq = mla_w_uq[j].reshape(q_rank, mla_heads, NOPE_DIM + ROPE_DIM)
            w_uq_l = jnp.concatenate(
                [w_uq[..., :NOPE_DIM], _rope_lane_columns(w_uq[..., NOPE_DIM:])],
                axis=-1).reshape(q_rank, mla_heads * QK_DIM).astype(BF16)
            q_full = _mla_queries(x, mla_w_dq[j], mla_q_norm[j], w_uq_l, cc, ss, scale)
            o = _mla_attention(q_full, k_full, v_full)
            x = _proj_residual_ln(o.reshape(bsz * s_len, -1), x.reshape(bsz * s_len, d), mla_w_o[j],
                                  ln_g[layer, 0], ln_b[layer, 0], dn_alpha).reshape(bsz, s_len, d)
        x = _moe_layer(x.reshape(bsz * s_len, d), router_w[layer], router_b[layer], moe_w1, layer,
                       moe_b1[layer], moe_w2[layer], moe_b2[layer], ln_g[layer, 1], ln_b[layer, 1],
                       dn_alpha).reshape(bsz, s_len, d)
    return x
```

```python
import functools

import jax
import jax.numpy as jnp
from jax import lax
from jax.experimental import pallas as pl
from jax.experimental.pallas import tpu as pltpu

F32 = jnp.float32
BF16 = jnp.bfloat16

HG_HEAD_DIM = 128
NOPE_DIM = 128
ROPE_DIM = 64
V_DIM = 128
VO_DIM = 256
LOG2_E = 1.4426950408889634
KV_RANK = 128
ROPE_THETA = 10000.0
TOP_K = 4
SWIGLU_ALPHA = 1.702
SWIGLU_LIMIT = 7.0
LN_EPS = 1e-5
RMS_EPS = 1e-6

LANES = 128
ROW_SUBLANES = 8
QK_DIM = 256
VMEM_LIMIT = 56 * 1024 * 1024

ROW_TILE = 256
MLA_TILE = 512
HG_GROUP = 128
HG_HEAD_UNROLL = 8
ATT_TQ = 512
MOE_BLOCK = 512
SEG_TILE = 512
SEG_ALIGN = ROW_SUBLANES
SEG_BITS = (SEG_TILE * TOP_K // SEG_ALIGN).bit_length()
SEG_LOW_BITS = 4
PERM_CHUNK = 256


def _cparams(*sem):
    return pltpu.CompilerParams(dimension_semantics=sem, vmem_limit_bytes=VMEM_LIMIT)


def _layer_norm(y, g, b):
    mu = jnp.mean(y, axis=-1, keepdims=True)
    d = y - mu
    var = jnp.mean(d * d, axis=-1, keepdims=True)
    return d * lax.rsqrt(var + LN_EPS) * g + b


def _dot(a, b):
    return jnp.dot(a, b, preferred_element_type=F32)


def _dot_nt(a, b):
    return lax.dot_general(a, b, (((1,), (1,)), ((), ())), preferred_element_type=F32)


def _dot_tn(a, b):
    return lax.dot_general(a, b, (((0,), (0,)), ((), ())), preferred_element_type=F32)


def _sigmoid(x):
    return 0.5 * jnp.tanh(0.5 * x) + 0.5


def _hgrn_kernel(x_ref, w_in_ref, lb_ref, gn_ref, w_o_ref, lg_ref, lbias_ref, out_ref,
                 q_s, f_s, i_s, g_s, mix_s, st_s, *, dn_alpha):
    ts, d = x_ref.shape[1], x_ref.shape[2]
    n_heads = d // HG_HEAD_DIM
    grp = HG_GROUP

    @pl.when(pl.program_id(1) == 0)
    def _():
        st_s[...] = jnp.zeros_like(st_s)

    x = x_ref[0]
    xb = x.astype(BF16)
    lb = lb_ref[...]

    for sec, dst in enumerate((q_s, f_s, i_s, g_s)):
        p = _dot(xb, w_in_ref[:, sec * d:(sec + 1) * d])
        if sec == 0 or sec == 3:
            p = p * _sigmoid(p)
        elif sec == 1:
            p = lb + (1.0 - lb) * _sigmoid(p)
        for h in range(n_heads):
            dst[h] = p[:, h * HG_HEAD_DIM:(h + 1) * HG_HEAD_DIM]

    row = lax.broadcasted_iota(jnp.int32, (grp, grp), 0)
    col = lax.broadcasted_iota(jnp.int32, (grp, grp), 1)
    n_levels = grp.bit_length() - 1
    pair_masks = [(((row >> lvl) ^ (col >> lvl)) == 1) & (row > col) for lvl in range(n_levels)]

    def head_body(h, carry):
        for r0 in range(0, ts, grp):
            q = q_s[h, r0:r0 + grp, :]
            fg = f_s[h, r0:r0 + grp, :]
            iv = i_s[h, r0:r0 + grp, :].astype(BF16)
            k = 1.0 - fg
            ep, es, et = fg, None, fg
            a = jnp.where(row == col, _dot_nt(q.astype(BF16), k.astype(BF16)), 0.0)
            for lvl in range(n_levels):
                half = 1 << lvl
                kl = k if es is None else k * es
                a = jnp.where(pair_masks[lvl], _dot_nt((q * ep).astype(BF16), kl.astype(BF16)), a)
                if half < ROW_SUBLANES:
                    odd = (row & half) != 0
                    et3 = et.reshape(grp // ROW_SUBLANES, ROW_SUBLANES, LANES)
                    other = jnp.where(odd, pltpu.roll(et3, half, 1).reshape(et.shape),
                                      pltpu.roll(et3, ROW_SUBLANES - half, 1).reshape(et.shape))
                    ep = jnp.where(odd, ep * other, ep)
                    es = jnp.where(odd, 1.0, other) if es is None else jnp.where(odd, es, es * other)
                    et = et * other
                else:
                    ep_p, es_p, et_p = [], [], []
                    for b0 in range(0, grp, 2 * half):
                        lo, mid, hi = b0, b0 + half, b0 + 2 * half
                        tot = et[lo:mid] * et[mid:hi]
                        ep_p += [ep[lo:mid], ep[mid:hi] * et[lo:mid]]
                        es_p += [es[lo:mid] * et[mid:hi], es[mid:hi]]
                        et_p += [tot, tot]
                    ep = jnp.concatenate(ep_p, axis=0)
                    es = jnp.concatenate(es_p, axis=0)
                    et = jnp.concatenate(et_p, axis=0)
            st = st_s[h]
            o = _dot(a.astype(BF16), iv) + _dot_nt((q * ep).astype(BF16), st.astype(BF16))
            st_s[h] = st * et[0:1, :] + _dot_tn(iv, (k * es).astype(BF16))
            ms = jnp.mean(o * o, axis=-1, keepdims=True)
            y = o * lax.rsqrt(ms + RMS_EPS) * gn_ref[...] * g_s[h, r0:r0 + grp, :]
            mix_s[h, r0:r0 + grp, :] = y.astype(BF16)
        return carry

    lax.fori_loop(0, n_heads, head_body, 0, unroll=HG_HEAD_UNROLL)

    acc = dn_alpha * x
    for h in range(0, n_heads, 2):
        acc = acc + _dot(jnp.concatenate([mix_s[h], mix_s[h + 1]], axis=-1), w_o_ref[h // 2])
    out_ref[0] = _layer_norm(acc, lg_ref[...], lbias_ref[...])


def _hgrn_layer(x, w_in, lb, gnorm, w_o, ln_g, ln_b, dn_alpha):
    bsz, s_len, d = x.shape
    n_heads = d // HG_HEAD_DIM
    ts = ROW_TILE
    const2 = lambda b, s: (0, 0)
    gate_scratch = pltpu.VMEM((n_heads, ts, HG_HEAD_DIM), F32)
    return pl.pallas_call(
        functools.partial(_hgrn_kernel, dn_alpha=dn_alpha),
        grid=(bsz, s_len // ts),
        in_specs=[
            pl.BlockSpec((1, ts, d), lambda b, s: (b, s, 0)),
            pl.BlockSpec((d, 4 * d), const2),
            pl.BlockSpec((1, d), const2),
            pl.BlockSpec((1, HG_HEAD_DIM), const2),
            pl.BlockSpec((n_heads // 2, 2 * HG_HEAD_DIM, d), lambda b, s: (0, 0, 0)),
            pl.BlockSpec((1, d), const2),
            pl.BlockSpec((1, d), const2),
        ],
        out_specs=pl.BlockSpec((1, ts, d), lambda b, s: (b, s, 0)),
        out_shape=jax.ShapeDtypeStruct((bsz, s_len, d), F32),
        scratch_shapes=[
            gate_scratch, gate_scratch, gate_scratch, gate_scratch,
            pltpu.VMEM((n_heads, ts, HG_HEAD_DIM), BF16),
            pltpu.VMEM((n_heads, HG_HEAD_DIM, HG_HEAD_DIM), F32),
        ],
        compiler_params=_cparams("parallel", "arbitrary"),
        name="hgrn2_layer",
    )(x, w_in.astype(BF16), lb.reshape(1, d), gnorm.reshape(1, HG_HEAD_DIM),
      w_o.astype(BF16).reshape(n_heads // 2, 2 * HG_HEAD_DIM, d), ln_g.reshape(1, d), ln_b.reshape(1, d))


def _rope_lanes(t, cc, ss):
    return t * cc + pltpu.roll(t, LANES // 2, 1) * ss


def _kv_kernel(x_ref, wa_ref, kvn_ref, wb_ref, cc_ref, ss_ref, k_ref, v_ref):
    n_heads = k_ref.shape[1]
    xb = x_ref[0].astype(BF16)
    ckr = _dot(xb, wa_ref[...])
    c = ckr[:, :KV_RANK]
    c = c * lax.rsqrt(jnp.mean(c * c, axis=-1, keepdims=True) + RMS_EPS) * kvn_ref[...]
    kr = _rope_lanes(ckr[:, KV_RANK:], cc_ref[0], ss_ref[0]).astype(BF16)
    cb = c.astype(BF16)
    for h in range(n_heads):
        kv = _dot(cb, wb_ref[:, h * (NOPE_DIM + V_DIM):(h + 1) * (NOPE_DIM + V_DIM)])
        k_ref[0, h, :, :NOPE_DIM] = kv[:, :NOPE_DIM].astype(BF16)
        k_ref[0, h, :, NOPE_DIM:] = kr
        v_ref[0, h, :, :V_DIM] = kv[:, NOPE_DIM:].astype(BF16)
        v_ref[0, h, :, V_DIM:] = jnp.ones((kv.shape[0], VO_DIM - V_DIM), BF16)


def _shared_kv(x, kv_w_a_l, kv_norm, kv_w_b, cc, ss):
    bsz, s_len, d = x.shape
    n_heads = kv_w_b.shape[1] // (NOPE_DIM + V_DIM)
    ts = MLA_TILE
    const2 = lambda b, s: (0, 0)
    return pl.pallas_call(
        _kv_kernel,
        grid=(bsz, s_len // ts),
        in_specs=[
            pl.BlockSpec((1, ts, d), lambda b, s: (b, s, 0)),
            pl.BlockSpec((d, KV_RANK + LANES), const2),
            pl.BlockSpec((1, KV_RANK), const2),
            pl.BlockSpec((KV_RANK, n_heads * (NOPE_DIM + V_DIM)), const2),
            pl.BlockSpec((1, ts, LANES), lambda b, s: (b, s, 0)),
            pl.BlockSpec((1, ts, LANES), lambda b, s: (b, s, 0)),
        ],
        out_specs=[
            pl.BlockSpec((1, n_heads, ts, QK_DIM), lambda b, s: (b, 0, s, 0)),
            pl.BlockSpec((1, n_heads, ts, VO_DIM), lambda b, s: (b, 0, s, 0)),
        ],
        out_shape=[
            jax.ShapeDtypeStruct((bsz, n_heads, s_len, QK_DIM), BF16),
            jax.ShapeDtypeStruct((bsz, n_heads, s_len, VO_DIM), BF16),
        ],
        compiler_params=_cparams("parallel", "parallel"),
        name="mla_shared_kv",
    )(x, kv_w_a_l, kv_norm.reshape(1, KV_RANK), kv_w_b.astype(BF16), cc, ss)


def _q_kernel(x_ref, wdq_ref, qn_ref, wuq_ref, cc_ref, ss_ref, q_ref, *, scale):
    n_heads = q_ref.shape[1]
    xb = x_ref[0].astype(BF16)
    c = _dot(xb, wdq_ref[...])
    c = c * lax.rsqrt(jnp.mean(c * c, axis=-1, keepdims=True) + RMS_EPS) * qn_ref[...]
    cb = c.astype(BF16)
    cc = cc_ref[0] * scale
    ss = ss_ref[0] * scale
    for h in range(n_heads):
        qh = _dot(cb, wuq_ref[:, h * QK_DIM:(h + 1) * QK_DIM])
        q_ref[0, h, :, :NOPE_DIM] = (qh[:, :NOPE_DIM] * scale).astype(BF16)
        q_ref[0, h, :, NOPE_DIM:] = _rope_lanes(qh[:, NOPE_DIM:], cc, ss).astype(BF16)


def _mla_queries(x, w_dq, q_norm, w_uq_l, cc, ss, scale):
    bsz, s_len, d = x.shape
    q_rank = w_dq.shape[1]
    n_heads = w_uq_l.shape[1] // QK_DIM
    ts = MLA_TILE
    const2 = lambda b, s: (0, 0)
    return pl.pallas_call(
        functools.partial(_q_kernel, scale=scale),
        grid=(bsz, s_len // ts),
        in_specs=[
            pl.BlockSpec((1, ts, d), lambda b, s: (b, s, 0)),
            pl.BlockSpec((d, q_rank), const2),
            pl.BlockSpec((1, q_rank), const2),
            pl.BlockSpec((q_rank, n_heads * QK_DIM), const2),
            pl.BlockSpec((1, ts, LANES), lambda b, s: (b, s, 0)),
            pl.BlockSpec((1, ts, LANES), lambda b, s: (b, s, 0)),
        ],
        out_specs=pl.BlockSpec((1, n_heads, ts, QK_DIM), lambda b, s: (b, 0, s, 0)),
        out_shape=jax.ShapeDtypeStruct((bsz, n_heads, s_len, QK_DIM), BF16),
        compiler_params=_cparams("parallel", "parallel"),
        name="mla_queries",
    )(x, w_dq.astype(BF16), q_norm.reshape(1, q_rank), w_uq_l, cc, ss)


def _attn_kernel(q_ref, k_ref, v_ref, o_ref):
    s_len = q_ref.shape[2]
    tq = min(ATT_TQ, s_len)
    row = lax.broadcasted_iota(jnp.int32, (tq, tq), 0)
    col = lax.broadcasted_iota(jnp.int32, (tq, tq), 1)
    for qi in range(s_len // tq):
        q = q_ref[0, 0, qi * tq:(qi + 1) * tq, :]
        m = jnp.full((tq, 1), -jnp.inf, F32)
        acc = jnp.zeros((tq, VO_DIM), F32)
        for kj in range(qi + 1):
            s = _dot_nt(q, k_ref[0, 0, kj * tq:(kj + 1) * tq, :])
            if kj == qi:
                s = jnp.where(col <= row, s, -jnp.inf)
            m_new = jnp.maximum(m, jnp.max(s, axis=-1, keepdims=True))
            p = jnp.exp2(s - m_new)
            acc = jnp.exp2(m - m_new) * acc + _dot(p.astype(BF16), v_ref[0, 0, kj * tq:(kj + 1) * tq, :])
            m = m_new
        o_ref[0, qi * tq:(qi + 1) * tq, :] = (acc[:, :V_DIM] / acc[:, V_DIM:V_DIM + 1]).astype(o_ref.dtype)


def _mla_attention(q, k, v):
    bsz, n_heads, s_len, _ = q.shape
    return pl.pallas_call(
        _attn_kernel,
        grid=(bsz, n_heads),
        in_specs=[
            pl.BlockSpec((1, 1, s_len, QK_DIM), lambda b, h: (b, h, 0, 0)),
            pl.BlockSpec((1, 1, s_len, QK_DIM), lambda b, h: (b, h, 0, 0)),
            pl.BlockSpec((1, 1, s_len, VO_DIM), lambda b, h: (b, h, 0, 0)),
        ],
        out_specs=pl.BlockSpec((1, s_len, V_DIM), lambda b, h: (b, 0, h)),
        out_shape=jax.ShapeDtypeStruct((bsz, s_len, n_heads * V_DIM), BF16),
        compiler_params=_cparams("parallel", "parallel"),
        name="mla_attention",
    )(q, k, v)


def _split_hi_lo(v):
    hi = lax.bitcast_convert_type(lax.bitcast_convert_type(v, jnp.uint32) & jnp.uint32(0xFFFF0000), F32)
    return hi.astype(BF16), (v - hi).astype(BF16)


def _route_tile(x, whi_ref, wlo_ref, b_ref, pos_ref, gate_ref, meta_ref):
    tr = x.shape[0]

    x_hi, x_lo = _split_hi_lo(x)
    logits = (_dot(x_hi, whi_ref[...]) + (_dot(x_lo, whi_ref[...]) + _dot(x_hi, wlo_ref[...]))
              + b_ref[...])
    lane = lax.broadcasted_iota(jnp.int32, (tr, LANES), 1)
    work = logits
    sel = jnp.zeros((tr, LANES), F32)
    ids, vals = [], []
    for _ in range(TOP_K):
        mx = jnp.max(work, axis=-1, keepdims=True)
        idx = jnp.min(jnp.where(work == mx, lane, LANES), axis=-1, keepdims=True)
        hit = lane == idx
        ids.append(idx)
        vals.append(mx)
        sel = jnp.where(hit, 1.0, sel)
        work = jnp.where(hit, -jnp.inf, work)
    exps = [jnp.exp(v - vals[0]) for v in vals]
    denom = exps[0] + exps[1] + exps[2] + exps[3]

    r = lax.broadcasted_iota(jnp.int32, (tr, tr), 0)
    c = lax.broadcasted_iota(jnp.int32, (tr, tr), 1)
    before = _dot(jnp.where(c < r, 1.0, 0.0).astype(BF16), sel.astype(BF16))
    seg8 = jnp.floor((jnp.sum(sel, axis=0, keepdims=True) + (SEG_ALIGN - 1)) * (1.0 / SEG_ALIGN))
    er = lax.broadcasted_iota(jnp.int32, (LANES, LANES), 0)
    ec = lax.broadcasted_iota(jnp.int32, (LANES, LANES), 1)
    start8 = _dot(jnp.broadcast_to(seg8, (ROW_SUBLANES, LANES)).astype(BF16),
                  jnp.where(er < ec, 1.0, 0.0).astype(BF16))[0:1]
    slot = start8 * SEG_ALIGN + before

    pos_out = jnp.zeros((tr, LANES), F32)
    gate_out = jnp.zeros((tr, LANES), F32)
    for j in range(TOP_K):
        pos = jnp.sum(jnp.where(lane == ids[j], slot, 0.0), axis=-1, keepdims=True)
        pos_out = jnp.where(lane == j, pos, pos_out)
        gate_out = jnp.where(lane == j, exps[j] / denom, gate_out)
    pos_ref[...] = pos_out
    gate_ref[...] = gate_out
    row8 = lax.broadcasted_iota(jnp.int32, (ROW_SUBLANES, LANES), 0)
    meta_ref[...] = jnp.where(row8 == 0, seg8, jnp.where(row8 == 1, start8, 0.0))


def _router_kernel(x_ref, whi_ref, wlo_ref, b_ref, pos_ref, gate_ref, meta_ref):
    _route_tile(x_ref[...], whi_ref, wlo_ref, b_ref, pos_ref, gate_ref, meta_ref)


def _router_operands(router_w, router_b):
    d, n_exp = router_w.shape
    w_hi, w_lo = _split_hi_lo(jnp.zeros((d, LANES), F32).at[:, :n_exp].set(router_w))
    return w_hi, w_lo, jnp.full((1, LANES), -jnp.inf, F32).at[0, :n_exp].set(router_b)


def _router_specs(d, n_tok):
    const = lambda i: (0, 0)
    tile = lambda i: (i, 0)
    in_specs = [pl.BlockSpec((d, LANES), const), pl.BlockSpec((d, LANES), const), pl.BlockSpec((1, LANES), const)]
    out_specs = [pl.BlockSpec((SEG_TILE, LANES), tile), pl.BlockSpec((SEG_TILE, LANES), tile),
                 pl.BlockSpec((ROW_SUBLANES, LANES), tile)]
    out_shape = [
        jax.ShapeDtypeStruct((n_tok, LANES), F32),
        jax.ShapeDtypeStruct((n_tok, LANES), F32),
        jax.ShapeDtypeStruct((n_tok // SEG_TILE * ROW_SUBLANES, LANES), F32),
    ]
    return in_specs, out_specs, out_shape


def _router(x2, router_w, router_b):
    n_tok, d = x2.shape
    in_specs, out_specs, out_shape = _router_specs(d, n_tok)
    return pl.pallas_call(
        _router_kernel,
        grid=(n_tok // SEG_TILE,),
        in_specs=[pl.BlockSpec((SEG_TILE, d), lambda i: (i, 0))] + in_specs,
        out_specs=out_specs,
        out_shape=out_shape,
        compiler_params=_cparams("parallel"),
        name="moe_router",
    )(x2, *_router_operands(router_w, router_b))


def _proj_ln_route_kernel(o_ref, x_ref, w_ref, lg_ref, lb_ref, whi_ref, wlo_ref, rb_ref,
                          out_ref, pos_ref, gate_ref, meta_ref, *, dn_alpha):
    y = _layer_norm(dn_alpha * x_ref[...] + _dot(o_ref[...], w_ref[...]), lg_ref[...], lb_ref[...])
    out_ref[...] = y
    _route_tile(y, whi_ref, wlo_ref, rb_ref, pos_ref, gate_ref, meta_ref)


def _proj_residual_ln_route(o2, x2, w_o, ln_g, ln_b, router_w, router_b, dn_alpha):
    n_tok, d = x2.shape
    kdim = o2.shape[1]
    ts = SEG_TILE
    const = lambda i: (0, 0)
    r_in, r_out, r_shape = _router_specs(d, n_tok)
    res = pl.pallas_call(
        functools.partial(_proj_ln_route_kernel, dn_alpha=dn_alpha),
        grid=(n_tok // ts,),
        in_specs=[
            pl.BlockSpec((ts, kdim), lambda i: (i, 0)),
            pl.BlockSpec((ts, d), lambda i: (i, 0)),
            pl.BlockSpec((kdim, d), const),
            pl.BlockSpec((1, d), const),
            pl.BlockSpec((1, d), const),
        ] + r_in,
        out_specs=[pl.BlockSpec((ts, d), lambda i: (i, 0))] + r_out,
        out_shape=[jax.ShapeDtypeStruct((n_tok, d), F32)] + r_shape,
        compiler_params=_cparams("parallel"),
        name="mla_out_proj_ln_route",
    )(o2, x2, w_o.astype(BF16), ln_g.reshape(1, d), ln_b.reshape(1, d), *_router_operands(router_w, router_b))
    return res[0], tuple(res[1:])


def _segment_copy(local_ref, hbm_ref, sem, to_hbm, lo8, go8, bit):
    rows = SEG_ALIGN << bit
    aligned = lambda v8: v8 * SEG_ALIGN if isinstance(v8, int) else pl.multiple_of(v8 * SEG_ALIGN, SEG_ALIGN)
    loc = local_ref.at[pl.ds(aligned(lo8), rows)]
    hbm = hbm_ref.at[pl.ds(aligned(go8), rows)]
    return pltpu.make_async_copy(loc, hbm, sem) if to_hbm else pltpu.make_async_copy(hbm, loc, sem)


def _segment_starts(tabs, tile, n_exp, local_ref, hbm_ref, sem, to_hbm):
    seg8_ref, start8_ref, gstart8_ref, _ = tabs

    def body(e, c):
        n = seg8_ref[tile * n_exp + e]
        lo = start8_ref[tile * n_exp + e]
        go = gstart8_ref[tile * n_exp + e]

        def bits(lo_bit, hi_bit):
            for bit in range(lo_bit, hi_bit):
                @pl.when(((n >> bit) & 1) == 1)
                def _():
                    off = (n >> (bit + 1)) << (bit + 1)
                    _segment_copy(local_ref, hbm_ref, sem, to_hbm, lo + off, go + off, bit).start()

        bits(0, SEG_LOW_BITS)

        @pl.when(n >= (1 << SEG_LOW_BITS))
        def _():
            bits(SEG_LOW_BITS, SEG_BITS)
        return c

    lax.fori_loop(0, n_exp, body, 0)


def _segment_wait(tabs, tile, local_ref, hbm_ref, sem, to_hbm):
    total = tabs[3][tile]
    for bit in range(SEG_BITS):
        @pl.when(((total >> bit) & 1) == 1)
        def _():
            _segment_copy(local_ref, hbm_ref, sem, to_hbm, 0, 0, bit).wait()


def _dispatch_kernel(seg8_ref, start8_ref, gstart8_ref, tot8_ref, x_ref, pos_ref, xs_hbm, buf, sems, *, n_exp):
    td = x_ref.shape[0]
    seg_rows = buf.shape[1]
    tabs = (seg8_ref, start8_ref, gstart8_ref, tot8_ref)
    step = pl.program_id(0)
    slot = step % 2

    def wait(tile, sl):
        _segment_wait(tabs, tile, buf.at[sl], xs_hbm, sems.at[sl], True)

    @pl.when(step >= 2)
    def _():
        wait(step - 2, slot)

    xb = x_ref[...].astype(BF16)
    pos_t = pos_ref[...].T
    for r0 in range(0, seg_rows, PERM_CHUNK):
        rr = (lax.broadcasted_iota(jnp.int32, (PERM_CHUNK, td), 0) + r0).astype(F32)
        hit = pos_t[0:1] == rr
        for j in range(1, TOP_K):
            hit = hit | (pos_t[j:j + 1] == rr)
        buf[slot, r0:r0 + PERM_CHUNK, :] = _dot(jnp.where(hit, 1.0, 0.0).astype(BF16), xb)

    _segment_starts(tabs, step, n_exp, buf.at[slot], xs_hbm, sems.at[slot], True)

    @pl.when(step == pl.num_programs(0) - 1)
    def _():
        wait(step, slot)

        @pl.when(step >= 1)
        def _():
            wait(step - 1, 1 - slot)


def _seg_rows(td, n_exp):
    return -(-(td * TOP_K + n_exp * (SEG_ALIGN - 1)) // PERM_CHUNK) * PERM_CHUNK


def _dispatch(x2, pos, tabs, n_rows, n_exp):
    n_tok, d = x2.shape
    td = SEG_TILE
    grid_spec = pltpu.PrefetchScalarGridSpec(
        num_scalar_prefetch=len(tabs),
        grid=(n_tok // td,),
        in_specs=[
            pl.BlockSpec((td, d), lambda i, *_: (i, 0)),
            pl.BlockSpec((td, LANES), lambda i, *_: (i, 0)),
        ],
        out_specs=pl.BlockSpec(memory_space=pl.ANY),
        scratch_shapes=[pltpu.VMEM((2, _seg_rows(td, n_exp), d), F32), pltpu.SemaphoreType.DMA((2,))],
    )
    return pl.pallas_call(
        functools.partial(_dispatch_kernel, n_exp=n_exp),
        grid_spec=grid_spec,
        out_shape=jax.ShapeDtypeStruct((n_rows, d), F32),
        compiler_params=_cparams("arbitrary"),
        name="moe_dispatch",
    )(*tabs, x2, pos)


def _expert_kernel(be_ref, nu_ref, xs_ref, w1_ref, b1_ref, w2_ref, b2_ref, ys_ref):
    @pl.when(pl.program_id(0) < nu_ref[0])
    def _():
        f = w2_ref.shape[1]
        h = _dot(xs_ref[...].astype(BF16), w1_ref[0]) + b1_ref[0]
        glu = jnp.minimum(h[:, :f], SWIGLU_LIMIT)
        lin = jnp.clip(h[:, f:], -SWIGLU_LIMIT, SWIGLU_LIMIT)
        a = glu * _sigmoid(SWIGLU_ALPHA * glu) * (lin + 1.0)
        ys_ref[...] = _dot(a.astype(BF16), w2_ref[0]) + b2_ref[0]


def _experts(xs, block_e, n_used, w1, b1, w2, b2):
    n_exp, f, d = w2.shape
    blk = MOE_BLOCK
    n_blocks = xs.shape[0] // blk
    row_map = lambda i, be, nu: (jnp.minimum(i, nu[0] - 1), 0)
    exp_map = lambda i, be, nu: (be[i], 0, 0)
    grid_spec = pltpu.PrefetchScalarGridSpec(
        num_scalar_prefetch=2,
        grid=(n_blocks,),
        in_specs=[
            pl.BlockSpec((blk, d), row_map),
            pl.BlockSpec((1, d, 2 * f), exp_map),
            pl.BlockSpec((1, 1, 2 * f), exp_map),
            pl.BlockSpec((1, f, d), exp_map),
            pl.BlockSpec((1, 1, d), exp_map),
        ],
        out_specs=pl.BlockSpec((blk, d), row_map),
    )
    return pl.pallas_call(
        _expert_kernel,
        grid_spec=grid_spec,
        out_shape=jax.ShapeDtypeStruct(xs.shape, F32),
        compiler_params=_cparams("arbitrary"),
        name="moe_experts",
    )(block_e, n_used, xs, w1, b1.reshape(n_exp, 1, 2 * f), w2, b2.reshape(n_exp, 1, d))


def _combine_kernel(seg8_ref, start8_ref, gstart8_ref, tot8_ref, ys_hbm, x_ref, pos_ref, gate_ref, lg_ref,
                    lb_ref, out_ref, buf, sems, *, n_exp, dn_alpha):
    td = x_ref.shape[0]
    seg_rows = buf.shape[1]
    tabs = (seg8_ref, start8_ref, gstart8_ref, tot8_ref)
    step = pl.program_id(0)
    slot = step % 2

    def start(tile, sl):
        _segment_starts(tabs, tile, n_exp, buf.at[sl], ys_hbm, sems.at[sl], False)

    @pl.when(step == 0)
    def _():
        buf[...] = jnp.zeros_like(buf)
        start(step, slot)

    @pl.when(step + 1 < pl.num_programs(0))
    def _():
        start(step + 1, 1 - slot)

    _segment_wait(tabs, step, buf.at[slot], ys_hbm, sems.at[slot], False)

    pos = pos_ref[...]
    gates = gate_ref[...]
    acc = dn_alpha * x_ref[...]
    for r0 in range(0, seg_rows, PERM_CHUNK):
        cc = (lax.broadcasted_iota(jnp.int32, (td, PERM_CHUNK), 1) + r0).astype(F32)
        g = jnp.zeros((td, PERM_CHUNK), F32)
        for j in range(TOP_K):
            g = jnp.where(pos[:, j:j + 1] == cc, gates[:, j:j + 1], g)
        acc = acc + _dot(g.astype(BF16), buf[slot, r0:r0 + PERM_CHUNK, :].astype(BF16))
    out_ref[...] = _layer_norm(acc, lg_ref[...], lb_ref[...])


def _combine_ln(ys, pos, gates, tabs, x2, ln_g, ln_b, n_exp, dn_alpha):
    n_tok, d = x2.shape
    td = SEG_TILE
    const = lambda i, *_: (0, 0)
    tile = lambda i, *_: (i, 0)
    grid_spec = pltpu.PrefetchScalarGridSpec(
        num_scalar_prefetch=len(tabs),
        grid=(n_tok // td,),
        in_specs=[
            pl.BlockSpec(memory_space=pl.ANY),
            pl.BlockSpec((td, d), tile),
            pl.BlockSpec((td, LANES), tile),
            pl.BlockSpec((td, LANES), tile),
            pl.BlockSpec((1, d), const),
            pl.BlockSpec((1, d), const),
        ],
        out_specs=pl.BlockSpec((td, d), tile),
        scratch_shapes=[pltpu.VMEM((2, _seg_rows(td, n_exp), d), F32), pltpu.SemaphoreType.DMA((2,))],
    )
    return pl.pallas_call(
        functools.partial(_combine_kernel, n_exp=n_exp, dn_alpha=dn_alpha),
        grid_spec=grid_spec,
        out_shape=jax.ShapeDtypeStruct((n_tok, d), F32),
        compiler_params=_cparams("arbitrary"),
        name="moe_combine_ln",
    )(*tabs, ys, x2, pos, gates, ln_g.reshape(1, d), ln_b.reshape(1, d))


def _w1_prep_kernel(w_ref, p_ref, out_ref):
    f = w_ref.shape[3] // 2
    width = p_ref.shape[0]
    for c in range(2 * f // width):
        t = _dot(w_ref[0, 0, :, c * width:(c + 1) * width].astype(BF16), p_ref[...])
        out_ref[0, :, c * (width // 2):(c + 1) * (width // 2)] = t[:, :width // 2].astype(BF16)
        out_ref[0, :, f + c * (width // 2):f + (c + 1) * (width // 2)] = t[:, width // 2:].astype(BF16)


def _w1_prep(w1, layer):
    _, n_exp, d, f2 = w1.shape
    width = 2 * LANES
    rows = ROW_TILE
    r = lax.broadcasted_iota(jnp.int32, (width, width), 0)
    c = lax.broadcasted_iota(jnp.int32, (width, width), 1)
    perm = (c == (r % 2) * (width // 2) + r // 2).astype(BF16)
    return pl.pallas_call(
        _w1_prep_kernel,
        grid=(n_exp, d // rows),
        in_specs=[
            pl.BlockSpec((1, 1, rows, f2), lambda e, i: (layer, e, i, 0)),
            pl.BlockSpec((width, width), lambda e, i: (0, 0)),
        ],
        out_specs=pl.BlockSpec((1, rows, f2), lambda e, i: (e, i, 0)),
        out_shape=jax.ShapeDtypeStruct((n_exp, d, f2), BF16),
        compiler_params=_cparams("parallel", "parallel"),
        name="moe_w1_prep",
    )(w1, perm)


def _moe_layer(x2, router_w, router_b, w1_all, layer, b1, w2, b2, ln_g, ln_b, dn_alpha, routing=None):
    n_tok, d = x2.shape
    n_exp = w2.shape[0]
    n_tiles = n_tok // SEG_TILE
    blk8 = MOE_BLOCK // SEG_ALIGN
    n_blocks = -(-(n_tok * TOP_K + n_tiles * n_exp * (SEG_ALIGN - 1)) // MOE_BLOCK) + n_exp
    n_rows = n_blocks * MOE_BLOCK

    pos, gates, meta = _router(x2, router_w, router_b) if routing is None else routing

    meta = meta.reshape(n_tiles, ROW_SUBLANES, LANES)
    seg8 = meta[:, 0, :n_exp].astype(jnp.int32)
    start8 = meta[:, 1, :n_exp].astype(jnp.int32)
    padded8 = (jnp.sum(seg8, axis=0) + blk8 - 1) // blk8 * blk8
    pends8 = jnp.cumsum(padded8)
    gstart8 = (pends8 - padded8)[None, :] + jnp.cumsum(seg8, axis=0) - seg8
    block_start8 = jnp.arange(n_blocks, dtype=jnp.int32) * blk8
    block_e = jnp.minimum(jnp.sum((pends8[None, :] <= block_start8[:, None]).astype(jnp.int32), axis=1),
                          n_exp - 1).astype(jnp.int32)
    n_used = (pends8[-1:] // blk8).astype(jnp.int32)
    tabs = (seg8.reshape(-1), start8.reshape(-1), gstart8.reshape(-1).astype(jnp.int32),
            jnp.sum(seg8, axis=1).astype(jnp.int32))

    b1_l = jnp.concatenate([b1[:, 0::2], b1[:, 1::2]], axis=-1)

    xs = _dispatch(x2, pos, tabs, n_rows, n_exp)
    ys = _experts(xs, block_e, n_used, _w1_prep(w1_all, layer), b1_l, w2.astype(BF16), b2)
    return _combine_ln(ys, pos, gates, tabs, x2, ln_g, ln_b, n_exp, dn_alpha)


def _rope_lane_tables(positions):
    half = ROPE_DIM // 2
    inv_freq = ROPE_THETA ** (-jnp.arange(0, ROPE_DIM, 2, dtype=F32) / ROPE_DIM)
    ang = positions.astype(F32)[..., None] * inv_freq
    cos, sin = jnp.cos(ang), jnp.sin(ang)
    z = jnp.zeros_like(cos)
    assert 4 * half == LANES
    return (jnp.concatenate([cos, z, cos, z], axis=-1),
            jnp.concatenate([-sin, z, sin, z], axis=-1))


def _rope_lane_columns(w_rope):
    half = ROPE_DIM // 2
    z = jnp.zeros(w_rope.shape[:-1] + (half,), w_rope.dtype)
    return jnp.concatenate([w_rope[..., :half], z, w_rope[..., half:], z], axis=-1)


def kernel(x, positions, ln_g, ln_b, hg_w_in, hg_lb, hg_gnorm, hg_w_o, mla_w_dq, mla_q_norm,
           mla_w_uq, mla_w_o, kv_w_a, kv_norm, kv_w_b, router_w, router_b, moe_w1, moe_b1,
           moe_w2, moe_b2):
    bsz, s_len, d = x.shape
    depth = ln_g.shape[0]
    n_a = hg_w_in.shape[0]
    dn_alpha = (2.0 * depth) ** 0.25
    scale = (NOPE_DIM + ROPE_DIM) ** -0.5 * LOG2_E

    lb_soft = jax.nn.softmax(hg_lb.astype(F32), axis=0)
    lower_bounds = jnp.cumsum(lb_soft, axis=0) - lb_soft[0]

    q_rank = mla_w_uq.shape[1]
    mla_heads = mla_w_uq.shape[2] // (NOPE_DIM + ROPE_DIM)
    cc = ss = k_full = v_full = None

    for layer in range(depth):
        routing = None
        if layer < n_a:
            x = _hgrn_layer(x, hg_w_in[layer], lower_bounds[layer], hg_gnorm[layer], hg_w_o[layer],
                            ln_g[layer, 0], ln_b[layer, 0], dn_alpha)
        else:
            j = layer - n_a
            if layer == n_a:
                cc, ss = _rope_lane_tables(positions)
                kv_w_a_l = jnp.concatenate(
                    [kv_w_a[:, :KV_RANK], _rope_lane_columns(kv_w_a[:, KV_RANK:])], axis=-1).astype(BF16)
                k_full, v_full = _shared_kv(x, kv_w_a_l, kv_norm, kv_w_b, cc, ss)
            w_uq = mla_w_uq[j].reshape(q_rank, mla_heads, NOPE_DIM + ROPE_DIM)
            w_uq_l = jnp.concatenate(
                [w_uq[..., :NOPE_DIM], _rope_lane_columns(w_uq[..., NOPE_DIM:])],
                axis=-1).reshape(q_rank, mla_heads * QK_DIM).astype(BF16)
            q_full = _mla_queries(x, mla_w_dq[j], mla_q_norm[j], w_uq_l, cc, ss, scale)
            o = _mla_attention(q_full, k_full, v_full)
            x, routing = _proj_residual_ln_route(
                o.reshape(bsz * s_len, -1), x.reshape(bsz * s_len, d), mla_w_o[j], ln_g[layer, 0],
                ln_b[layer, 0], router_w[layer], router_b[layer], dn_alpha)
        x = _moe_layer(x.reshape(bsz * s_len, d), router_w[layer], router_b[layer], moe_w1, layer,
                       moe_b1[layer], moe_w2[layer], moe_b2[layer], ln_g[layer, 1], ln_b[layer, 1],
                       dn_alpha, routing).reshape(bsz, s_len, d)
    return x
```

```python
import functools

import jax
import jax.numpy as jnp
from jax import lax
from jax.experimental import pallas as pl
from jax.experimental.pallas import tpu as pltpu

F32 = jnp.float32
BF16 = jnp.bfloat16

HG_HEAD_DIM = 128
NOPE_DIM = 128
ROPE_DIM = 64
V_DIM = 128
VO_DIM = 256
LOG2_E = 1.4426950408889634
KV_RANK = 128
ROPE_THETA = 10000.0
TOP_K = 4
SWIGLU_ALPHA = 1.702
SWIGLU_LIMIT = 7.0
LN_EPS = 1e-5
RMS_EPS = 1e-6

LANES = 128
ROW_SUBLANES = 8
QK_DIM = 256
VMEM_LIMIT = 56 * 1024 * 1024

ROW_TILE = 256
MLA_TILE = 512
HG_GROUP = 128
HG_HEAD_UNROLL = 8
ATT_TQ = 512
MOE_BLOCK = 512
SEG_TILE = 512
SEG_ALIGN = ROW_SUBLANES
SEG_BITS = (SEG_TILE * TOP_K // SEG_ALIGN).bit_length()
SEG_LOW_BITS = 4
PERM_CHUNK = 256


def _cparams(*sem):
    return pltpu.CompilerParams(dimension_semantics=sem, vmem_limit_bytes=VMEM_LIMIT)


def _layer_norm(y, g, b):
    mu = jnp.mean(y, axis=-1, keepdims=True)
    d = y - mu
    var = jnp.mean(d * d, axis=-1, keepdims=True)
    return d * lax.rsqrt(var + LN_EPS) * g + b


def _dot(a, b):
    return jnp.dot(a, b, preferred_element_type=F32)


def _dot_nt(a, b):
    return lax.dot_general(a, b, (((1,), (1,)), ((), ())), preferred_element_type=F32)


def _dot_tn(a, b):
    return lax.dot_general(a, b, (((0,), (0,)), ((), ())), preferred_element_type=F32)


def _sigmoid(x):
    return 0.5 * jnp.tanh(0.5 * x) + 0.5


def _hgrn_kernel(x_ref, w_in_ref, lb_ref, gn_ref, w_o_ref, lg_ref, lbias_ref, out_ref,
                 q_s, f_s, i_s, g_s, mix_s, st_s, *, dn_alpha):
    ts, d = x_ref.shape[1], x_ref.shape[2]
    n_heads = d // HG_HEAD_DIM
    grp = HG_GROUP

    @pl.when(pl.program_id(1) == 0)
    def _():
        st_s[...] = jnp.zeros_like(st_s)

    x = x_ref[0]
    xb = x.astype(BF16)
    lb = lb_ref[...]

    for sec, dst in enumerate((q_s, f_s, i_s, g_s)):
        p = _dot(xb, w_in_ref[:, sec * d:(sec + 1) * d])
        if sec == 0 or sec == 3:
            p = p * _sigmoid(p)
        elif sec == 1:
            p = lb + (1.0 - lb) * _sigmoid(p)
        for h in range(n_heads):
            dst[h] = p[:, h * HG_HEAD_DIM:(h + 1) * HG_HEAD_DIM]

    row = lax.broadcasted_iota(jnp.int32, (grp, grp), 0)
    col = lax.broadcasted_iota(jnp.int32, (grp, grp), 1)
    n_levels = grp.bit_length() - 1
    pair_masks = [(((row >> lvl) ^ (col >> lvl)) == 1) & (row > col) for lvl in range(n_levels)]

    def head_body(h, carry):
        for r0 in range(0, ts, grp):
            q = q_s[h, r0:r0 + grp, :]
            fg = f_s[h, r0:r0 + grp, :]
            iv = i_s[h, r0:r0 + grp, :].astype(BF16)
            k = 1.0 - fg
            ep, es, et = fg, None, fg
            a = jnp.where(row == col, _dot_nt(q.astype(BF16), k.astype(BF16)), 0.0)
            for lvl in range(n_levels):
                half = 1 << lvl
                kl = k if es is None else k * es
                a = jnp.where(pair_masks[lvl], _dot_nt((q * ep).astype(BF16), kl.astype(BF16)), a)
                if half < ROW_SUBLANES:
                    odd = (row & half) != 0
                    et3 = et.reshape(grp // ROW_SUBLANES, ROW_SUBLANES, LANES)
                    other = jnp.where(odd, pltpu.roll(et3, half, 1).reshape(et.shape),
                                      pltpu.roll(et3, ROW_SUBLANES - half, 1).reshape(et.shape))
                    ep = jnp.where(odd, ep * other, ep)
                    es = jnp.where(odd, 1.0, other) if es is None else jnp.where(odd, es, es * other)
                    et = et * other
                else:
                    ep_p, es_p, et_p = [], [], []
                    for b0 in range(0, grp, 2 * half):
                        lo, mid, hi = b0, b0 + half, b0 + 2 * half
                        tot = et[lo:mid] * et[mid:hi]
                        ep_p += [ep[lo:mid], ep[mid:hi] * et[lo:mid]]
                        es_p += [es[lo:mid] * et[mid:hi], es[mid:hi]]
                        et_p += [tot, tot]
                    ep = jnp.concatenate(ep_p, axis=0)
                    es = jnp.concatenate(es_p, axis=0)
                    et = jnp.concatenate(et_p, axis=0)
            st = st_s[h]
            o = _dot(a.astype(BF16), iv) + _dot_nt((q * ep).astype(BF16), st.astype(BF16))
            st_s[h] = st * et[0:1, :] + _dot_tn(iv, (k * es).astype(BF16))
            ms = jnp.mean(o * o, axis=-1, keepdims=True)
            y = o * lax.rsqrt(ms + RMS_EPS) * gn_ref[...] * g_s[h, r0:r0 + grp, :]
            mix_s[h, r0:r0 + grp, :] = y.astype(BF16)
        return carry

    lax.fori_loop(0, n_heads, head_body, 0, unroll=HG_HEAD_UNROLL)

    acc = dn_alpha * x
    for h in range(0, n_heads, 2):
        acc = acc + _dot(jnp.concatenate([mix_s[h], mix_s[h + 1]], axis=-1), w_o_ref[h // 2])
    out_ref[0] = _layer_norm(acc, lg_ref[...], lbias_ref[...])


def _hgrn_layer(x, w_in, lb, gnorm, w_o, ln_g, ln_b, dn_alpha):
    bsz, s_len, d = x.shape
    n_heads = d // HG_HEAD_DIM
    ts = ROW_TILE
    const2 = lambda b, s: (0, 0)
    gate_scratch = pltpu.VMEM((n_heads, ts, HG_HEAD_DIM), F32)
    return pl.pallas_call(
        functools.partial(_hgrn_kernel, dn_alpha=dn_alpha),
        grid=(bsz, s_len // ts),
        in_specs=[
            pl.BlockSpec((1, ts, d), lambda b, s: (b, s, 0)),
            pl.BlockSpec((d, 4 * d), const2),
            pl.BlockSpec((1, d), const2),
            pl.BlockSpec((1, HG_HEAD_DIM), const2),
            pl.BlockSpec((n_heads // 2, 2 * HG_HEAD_DIM, d), lambda b, s: (0, 0, 0)),
            pl.BlockSpec((1, d), const2),
            pl.BlockSpec((1, d), const2),
        ],
        out_specs=pl.BlockSpec((1, ts, d), lambda b, s: (b, s, 0)),
        out_shape=jax.ShapeDtypeStruct((bsz, s_len, d), F32),
        scratch_shapes=[
            gate_scratch, gate_scratch, gate_scratch, gate_scratch,
            pltpu.VMEM((n_heads, ts, HG_HEAD_DIM), BF16),
            pltpu.VMEM((n_heads, HG_HEAD_DIM, HG_HEAD_DIM), F32),
        ],
        compiler_params=_cparams("parallel", "arbitrary"),
        name="hgrn2_layer",
    )(x, w_in.astype(BF16), lb.reshape(1, d), gnorm.reshape(1, HG_HEAD_DIM),
      w_o.astype(BF16).reshape(n_heads // 2, 2 * HG_HEAD_DIM, d), ln_g.reshape(1, d), ln_b.reshape(1, d))


def _rope_lanes(t, cc, ss):
    return t * cc + pltpu.roll(t, LANES // 2, 1) * ss


def _kv_kernel(x_ref, wa_ref, kvn_ref, wb_ref, cc_ref, ss_ref, k_ref, v_ref):
    n_heads = k_ref.shape[1]
    xb = x_ref[0].astype(BF16)
    ckr = _dot(xb, wa_ref[...])
    c = ckr[:, :KV_RANK]
    c = c * lax.rsqrt(jnp.mean(c * c, axis=-1, keepdims=True) + RMS_EPS) * kvn_ref[...]
    kr = _rope_lanes(ckr[:, KV_RANK:], cc_ref[0], ss_ref[0]).astype(BF16)
    cb = c.astype(BF16)
    for h in range(n_heads):
        kv = _dot(cb, wb_ref[:, h * (NOPE_DIM + V_DIM):(h + 1) * (NOPE_DIM + V_DIM)])
        k_ref[0, h, :, :NOPE_DIM] = kv[:, :NOPE_DIM].astype(BF16)
        k_ref[0, h, :, NOPE_DIM:] = kr
        v_ref[0, h, :, :V_DIM] = kv[:, NOPE_DIM:].astype(BF16)
        v_ref[0, h, :, V_DIM:] = jnp.ones((kv.shape[0], VO_DIM - V_DIM), BF16)


def _shared_kv(x, kv_w_a_l, kv_norm, kv_w_b, cc, ss):
    bsz, s_len, d = x.shape
    n_heads = kv_w_b.shape[1] // (NOPE_DIM + V_DIM)
    ts = MLA_TILE
    const2 = lambda b, s: (0, 0)
    return pl.pallas_call(
        _kv_kernel,
        grid=(bsz, s_len // ts),
        in_specs=[
            pl.BlockSpec((1, ts, d), lambda b, s: (b, s, 0)),
            pl.BlockSpec((d, KV_RANK + LANES), const2),
            pl.BlockSpec((1, KV_RANK), const2),
            pl.BlockSpec((KV_RANK, n_heads * (NOPE_DIM + V_DIM)), const2),
            pl.BlockSpec((1, ts, LANES), lambda b, s: (b, s, 0)),
            pl.BlockSpec((1, ts, LANES), lambda b, s: (b, s, 0)),
        ],
        out_specs=[
            pl.BlockSpec((1, n_heads, ts, QK_DIM), lambda b, s: (b, 0, s, 0)),
            pl.BlockSpec((1, n_heads, ts, VO_DIM), lambda b, s: (b, 0, s, 0)),
        ],
        out_shape=[
            jax.ShapeDtypeStruct((bsz, n_heads, s_len, QK_DIM), BF16),
            jax.ShapeDtypeStruct((bsz, n_heads, s_len, VO_DIM), BF16),
        ],
        compiler_params=_cparams("parallel", "parallel"),
        name="mla_shared_kv",
    )(x, kv_w_a_l, kv_norm.reshape(1, KV_RANK), kv_w_b.astype(BF16), cc, ss)


def _q_kernel(x_ref, wdq_ref, qn_ref, wuq_ref, cc_ref, ss_ref, q_ref, *, scale):
    n_heads = q_ref.shape[1]
    xb = x_ref[0].astype(BF16)
    c = _dot(xb, wdq_ref[...])
    c = c * lax.rsqrt(jnp.mean(c * c, axis=-1, keepdims=True) + RMS_EPS) * qn_ref[...]
    cb = c.astype(BF16)
    cc = cc_ref[0] * scale
    ss = ss_ref[0] * scale
    for h in range(n_heads):
        qh = _dot(cb, wuq_ref[:, h * QK_DIM:(h + 1) * QK_DIM])
        q_ref[0, h, :, :NOPE_DIM] = (qh[:, :NOPE_DIM] * scale).astype(BF16)
        q_ref[0, h, :, NOPE_DIM:] = _rope_lanes(qh[:, NOPE_DIM:], cc, ss).astype(BF16)


def _mla_queries(x, w_dq, q_norm, w_uq_l, cc, ss, scale):
    bsz, s_len, d = x.shape
    q_rank = w_dq.shape[1]
    n_heads = w_uq_l.shape[1] // QK_DIM
    ts = MLA_TILE
    const2 = lambda b, s: (0, 0)
    return pl.pallas_call(
        functools.partial(_q_kernel, scale=scale),
        grid=(bsz, s_len // ts),
        in_specs=[
            pl.BlockSpec((1, ts, d), lambda b, s: (b, s, 0)),
            pl.BlockSpec((d, q_rank), const2),
            pl.BlockSpec((1, q_rank), const2),
            pl.BlockSpec((q_rank, n_heads * QK_DIM), const2),
            pl.BlockSpec((1, ts, LANES), lambda b, s: (b, s, 0)),
            pl.BlockSpec((1, ts, LANES), lambda b, s: (b, s, 0)),
        ],
        out_specs=pl.BlockSpec((1, n_heads, ts, QK_DIM), lambda b, s: (b, 0, s, 0)),
        out_shape=jax.ShapeDtypeStruct((bsz, n_heads, s_len, QK_DIM), BF16),
        compiler_params=_cparams("parallel", "parallel"),
        name="mla_queries",
    )(x, w_dq.astype(BF16), q_norm.reshape(1, q_rank), w_uq_l, cc, ss)


def _attn_kernel(q_ref, k_ref, v_ref, o_ref):
    s_len = q_ref.shape[2]
    tq = min(ATT_TQ, s_len)
    row = lax.broadcasted_iota(jnp.int32, (tq, tq), 0)
    col = lax.broadcasted_iota(jnp.int32, (tq, tq), 1)
    for qi in range(s_len // tq):
        q = q_ref[0, 0, qi * tq:(qi + 1) * tq, :]
        m = jnp.full((tq, 1), -jnp.inf, F32)
        acc = jnp.zeros((tq, VO_DIM), F32)
        for kj in range(qi + 1):
            s = _dot_nt(q, k_ref[0, 0, kj * tq:(kj + 1) * tq, :])
            if kj == qi:
                s = jnp.where(col <= row, s, -jnp.inf)
            m_new = jnp.maximum(m, jnp.max(s, axis=-1, keepdims=True))
            p = jnp.exp2(s - m_new)
            acc = jnp.exp2(m - m_new) * acc + _dot(p.astype(BF16), v_ref[0, 0, kj * tq:(kj + 1) * tq, :])
            m = m_new
        o_ref[0, qi * tq:(qi + 1) * tq, :] = (acc[:, :V_DIM] / acc[:, V_DIM:V_DIM + 1]).astype(o_ref.dtype)


def _mla_attention(q, k, v):
    bsz, n_heads, s_len, _ = q.shape
    return pl.pallas_call(
        _attn_kernel,
        grid=(bsz, n_heads),
        in_specs=[
            pl.BlockSpec((1, 1, s_len, QK_DIM), lambda b, h: (b, h, 0, 0)),
            pl.BlockSpec((1, 1, s_len, QK_DIM), lambda b, h: (b, h, 0, 0)),
            pl.BlockSpec((1, 1, s_len, VO_DIM), lambda b, h: (b, h, 0, 0)),
        ],
        out_specs=pl.BlockSpec((1, s_len, V_DIM), lambda b, h: (b, 0, h)),
        out_shape=jax.ShapeDtypeStruct((bsz, s_len, n_heads * V_DIM), BF16),
        compiler_params=_cparams("parallel", "parallel"),
        name="mla_attention",
    )(q, k, v)


def _split_hi_lo(v):
    hi = lax.bitcast_convert_type(lax.bitcast_convert_type(v, jnp.uint32) & jnp.uint32(0xFFFF0000), F32)
    return hi.astype(BF16), (v - hi).astype(BF16)


def _route_tile(x, whi_ref, wlo_ref, b_ref, pos_ref, gate_ref, meta_ref):
    tr = x.shape[0]

    x_hi, x_lo = _split_hi_lo(x)
    logits = (_dot(x_hi, whi_ref[...]) + (_dot(x_lo, whi_ref[...]) + _dot(x_hi, wlo_ref[...]))
              + b_ref[...])
    lane = lax.broadcasted_iota(jnp.int32, (tr, LANES), 1)
    work = logits
    sel = jnp.zeros((tr, LANES), F32)
    ids, vals = [], []
    for _ in range(TOP_K):
        mx = jnp.max(work, axis=-1, keepdims=True)
        idx = jnp.min(jnp.where(work == mx, lane, LANES), axis=-1, keepdims=True)
        hit = lane == idx
        ids.append(idx)
        vals.append(mx)
        sel = jnp.where(hit, 1.0, sel)
        work = jnp.where(hit, -jnp.inf, work)
    exps = [jnp.exp(v - vals[0]) for v in vals]
    denom = exps[0] + exps[1] + exps[2] + exps[3]

    r = lax.broadcasted_iota(jnp.int32, (tr, tr), 0)
    c = lax.broadcasted_iota(jnp.int32, (tr, tr), 1)
    before = _dot(jnp.where(c < r, 1.0, 0.0).astype(BF16), sel.astype(BF16))
    seg8 = jnp.floor((jnp.sum(sel, axis=0, keepdims=True) + (SEG_ALIGN - 1)) * (1.0 / SEG_ALIGN))
    er = lax.broadcasted_iota(jnp.int32, (LANES, LANES), 0)
    ec = lax.broadcasted_iota(jnp.int32, (LANES, LANES), 1)
    start8 = _dot(jnp.broadcast_to(seg8, (ROW_SUBLANES, LANES)).astype(BF16),
                  jnp.where(er < ec, 1.0, 0.0).astype(BF16))[0:1]
    slot = start8 * SEG_ALIGN + before

    pos_out = jnp.zeros((tr, LANES), F32)
    gate_out = jnp.zeros((tr, LANES), F32)
    for j in range(TOP_K):
        pos = jnp.sum(jnp.where(lane == ids[j], slot, 0.0), axis=-1, keepdims=True)
        pos_out = jnp.where(lane == j, pos, pos_out)
        gate_out = jnp.where(lane == j, exps[j] / denom, gate_out)
    pos_ref[...] = pos_out
    gate_ref[...] = gate_out
    row8 = lax.broadcasted_iota(jnp.int32, (ROW_SUBLANES, LANES), 0)
    meta_ref[...] = jnp.where(row8 == 0, seg8, jnp.where(row8 == 1, start8, 0.0))


def _router_kernel(x_ref, whi_ref, wlo_ref, b_ref, pos_ref, gate_ref, meta_ref):
    _route_tile(x_ref[...], whi_ref, wlo_ref, b_ref, pos_ref, gate_ref, meta_ref)


def _router_operands(router_w, router_b):
    d, n_exp = router_w.shape
    w_hi, w_lo = _split_hi_lo(jnp.zeros((d, LANES), F32).at[:, :n_exp].set(router_w))
    return w_hi, w_lo, jnp.full((1, LANES), -jnp.inf, F32).at[0, :n_exp].set(router_b)


def _router_specs(d, n_tok):
    const = lambda i: (0, 0)
    tile = lambda i: (i, 0)
    in_specs = [pl.BlockSpec((d, LANES), const), pl.BlockSpec((d, LANES), const), pl.BlockSpec((1, LANES), const)]
    out_specs = [pl.BlockSpec((SEG_TILE, LANES), tile), pl.BlockSpec((SEG_TILE, LANES), tile),
                 pl.BlockSpec((ROW_SUBLANES, LANES), tile)]
    out_shape = [
        jax.ShapeDtypeStruct((n_tok, LANES), F32),
        jax.ShapeDtypeStruct((n_tok, LANES), F32),
        jax.ShapeDtypeStruct((n_tok // SEG_TILE * ROW_SUBLANES, LANES), F32),
    ]
    return in_specs, out_specs, out_shape


def _router(x2, router_w, router_b):
    n_tok, d = x2.shape
    in_specs, out_specs, out_shape = _router_specs(d, n_tok)
    return pl.pallas_call(
        _router_kernel,
        grid=(n_tok // SEG_TILE,),
        in_specs=[pl.BlockSpec((SEG_TILE, d), lambda i: (i, 0))] + in_specs,
        out_specs=out_specs,
        out_shape=out_shape,
        compiler_params=_cparams("parallel"),
        name="moe_router",
    )(x2, *_router_operands(router_w, router_b))


def _proj_ln_route_kernel(o_ref, x_ref, w_ref, lg_ref, lb_ref, whi_ref, wlo_ref, rb_ref,
                          out_ref, pos_ref, gate_ref, meta_ref, *, dn_alpha):
    y = _layer_norm(dn_alpha * x_ref[...] + _dot(o_ref[...], w_ref[...]), lg_ref[...], lb_ref[...])
    out_ref[...] = y
    _route_tile(y, whi_ref, wlo_ref, rb_ref, pos_ref, gate_ref, meta_ref)


def _proj_residual_ln_route(o2, x2, w_o, ln_g, ln_b, router_w, router_b, dn_alpha):
    n_tok, d = x2.shape
    kdim = o2.shape[1]
    ts = SEG_TILE
    const = lambda i: (0, 0)
    r_in, r_out, r_shape = _router_specs(d, n_tok)
    res = pl.pallas_call(
        functools.partial(_proj_ln_route_kernel, dn_alpha=dn_alpha),
        grid=(n_tok // ts,),
        in_specs=[
            pl.BlockSpec((ts, kdim), lambda i: (i, 0)),
            pl.BlockSpec((ts, d), lambda i: (i, 0)),
            pl.BlockSpec((kdim, d), const),
            pl.BlockSpec((1, d), const),
            pl.BlockSpec((1, d), const),
        ] + r_in,
        out_specs=[pl.BlockSpec((ts, d), lambda i: (i, 0))] + r_out,
        out_shape=[jax.ShapeDtypeStruct((n_tok, d), F32)] + r_shape,
        compiler_params=_cparams("parallel"),
        name="mla_out_proj_ln_route",
    )(o2, x2, w_o.astype(BF16), ln_g.reshape(1, d), ln_b.reshape(1, d), *_router_operands(router_w, router_b))
    return res[0], tuple(res[1:])


def _segment_copy(local_ref, hbm_ref, sem, to_hbm, lo8, go8, bit):
    rows = SEG_ALIGN << bit
    aligned = lambda v8: v8 * SEG_ALIGN if isinstance(v8, int) else pl.multiple_of(v8 * SEG_ALIGN, SEG_ALIGN)
    loc = local_ref.at[pl.ds(aligned(lo8), rows)]
    hbm = hbm_ref.at[pl.ds(aligned(go8), rows)]
    return pltpu.make_async_copy(loc, hbm, sem) if to_hbm else pltpu.make_async_copy(hbm, loc, sem)


def _segment_starts(tabs, tile, n_exp, local_ref, hbm_ref, sem, to_hbm):
    seg8_ref, start8_ref, gstart8_ref, _ = tabs

    def body(e, c):
        n = seg8_ref[tile * n_exp + e]
        lo = start8_ref[tile * n_exp + e]
        go = gstart8_ref[tile * n_exp + e]

        def bits(lo_bit, hi_bit):
            for bit in range(lo_bit, hi_bit):
                @pl.when(((n >> bit) & 1) == 1)
                def _():
                    off = (n >> (bit + 1)) << (bit + 1)
                    _segment_copy(local_ref, hbm_ref, sem, to_hbm, lo + off, go + off, bit).start()

        bits(0, SEG_LOW_BITS)

        @pl.when(n >= (1 << SEG_LOW_BITS))
        def _():
            bits(SEG_LOW_BITS, SEG_BITS)
        return c

    lax.fori_loop(0, n_exp, body, 0)


def _segment_wait(tabs, tile, local_ref, hbm_ref, sem, to_hbm):
    total = tabs[3][tile]
    for bit in range(SEG_BITS):
        @pl.when(((total >> bit) & 1) == 1)
        def _():
            _segment_copy(local_ref, hbm_ref, sem, to_hbm, 0, 0, bit).wait()


def _dispatch_kernel(seg8_ref, start8_ref, gstart8_ref, tot8_ref, x_ref, pos_ref, xs_hbm, buf, sems, *, n_exp):
    td = x_ref.shape[0]
    seg_rows = buf.shape[1]
    tabs = (seg8_ref, start8_ref, gstart8_ref, tot8_ref)
    step = pl.program_id(0)
    slot = step % 2

    def wait(tile, sl):
        _segment_wait(tabs, tile, buf.at[sl], xs_hbm, sems.at[sl], True)

    @pl.when(step >= 2)
    def _():
        wait(step - 2, slot)

    xb = x_ref[...].astype(BF16)
    pos_t = pos_ref[...].T
    for r0 in range(0, seg_rows, PERM_CHUNK):
        rr = (lax.broadcasted_iota(jnp.int32, (PERM_CHUNK, td), 0) + r0).astype(F32)
        hit = pos_t[0:1] == rr
        for j in range(1, TOP_K):
            hit = hit | (pos_t[j:j + 1] == rr)
        buf[slot, r0:r0 + PERM_CHUNK, :] = _dot(jnp.where(hit, 1.0, 0.0).astype(BF16), xb)

    _segment_starts(tabs, step, n_exp, buf.at[slot], xs_hbm, sems.at[slot], True)

    @pl.when(step == pl.num_programs(0) - 1)
    def _():
        wait(step, slot)

        @pl.when(step >= 1)
        def _():
            wait(step - 1, 1 - slot)


def _seg_rows(td, n_exp):
    return -(-(td * TOP_K + n_exp * (SEG_ALIGN - 1)) // PERM_CHUNK) * PERM_CHUNK


def _dispatch(x2, pos, tabs, n_rows, n_exp):
    n_tok, d = x2.shape
    td = SEG_TILE
    grid_spec = pltpu.PrefetchScalarGridSpec(
        num_scalar_prefetch=len(tabs),
        grid=(n_tok // td,),
        in_specs=[
            pl.BlockSpec((td, d), lambda i, *_: (i, 0)),
            pl.BlockSpec((td, LANES), lambda i, *_: (i, 0)),
        ],
        out_specs=pl.BlockSpec(memory_space=pl.ANY),
        scratch_shapes=[pltpu.VMEM((2, _seg_rows(td, n_exp), d), F32), pltpu.SemaphoreType.DMA((2,))],
    )
    return pl.pallas_call(
        functools.partial(_dispatch_kernel, n_exp=n_exp),
        grid_spec=grid_spec,
        out_shape=jax.ShapeDtypeStruct((n_rows, d), F32),
        compiler_params=_cparams("arbitrary"),
        name="moe_dispatch",
    )(*tabs, x2, pos)


def _expert_kernel(be_ref, nu_ref, xs_ref, w1_ref, b1_ref, w2_ref, b2_ref, perm_ref, ys_ref, w1b_s, w2b_s):
    step = pl.program_id(0)
    d, f = w2b_s.shape[1], w2b_s.shape[0]
    active = step < nu_ref[0]
    new_expert = jnp.logical_or(step == 0, be_ref[step] != be_ref[jnp.maximum(step - 1, 0)])

    @pl.when(jnp.logical_and(active, new_expert))
    def _():
        width = perm_ref.shape[0]
        for r0 in range(0, d, ROW_TILE):
            for c in range(2 * f // width):
                t = _dot(w1_ref[0, 0, r0:r0 + ROW_TILE, c * width:(c + 1) * width].astype(BF16), perm_ref[...])
                w1b_s[r0:r0 + ROW_TILE, c * (width // 2):(c + 1) * (width // 2)] = t[:, :width // 2].astype(BF16)
                w1b_s[r0:r0 + ROW_TILE, f + c * (width // 2):f + (c + 1) * (width // 2)] = (
                    t[:, width // 2:].astype(BF16))
        w2b_s[...] = w2_ref[0, 0].astype(BF16)

    @pl.when(active)
    def _():
        h = _dot(xs_ref[...].astype(BF16), w1b_s[...]) + b1_ref[0]
        glu = jnp.minimum(h[:, :f], SWIGLU_LIMIT)
        lin = jnp.clip(h[:, f:], -SWIGLU_LIMIT, SWIGLU_LIMIT)
        a = glu * _sigmoid(SWIGLU_ALPHA * glu) * (lin + 1.0)
        ys_ref[...] = _dot(a.astype(BF16), w2b_s[...]) + b2_ref[0]


def _experts(xs, block_e, n_used, w1_all, w2_all, layer, b1, b2):
    _, n_exp, f, d = w2_all.shape
    blk = MOE_BLOCK
    n_blocks = xs.shape[0] // blk
    width = 2 * LANES
    r = lax.broadcasted_iota(jnp.int32, (width, width), 0)
    c = lax.broadcasted_iota(jnp.int32, (width, width), 1)
    perm = (c == (r % 2) * (width // 2) + r // 2).astype(BF16)
    row_map = lambda i, be, nu: (jnp.minimum(i, nu[0] - 1), 0)
    exp_map = lambda i, be, nu: (be[i], 0, 0)
    wgt_map = lambda i, be, nu: (layer, be[i], 0, 0)
    grid_spec = pltpu.PrefetchScalarGridSpec(
        num_scalar_prefetch=2,
        grid=(n_blocks,),
        in_specs=[
            pl.BlockSpec((blk, d), row_map),
            pl.BlockSpec((1, 1, d, 2 * f), wgt_map),
            pl.BlockSpec((1, 1, 2 * f), exp_map),
            pl.BlockSpec((1, 1, f, d), wgt_map),
            pl.BlockSpec((1, 1, d), exp_map),
            pl.BlockSpec((width, width), lambda i, be, nu: (0, 0)),
        ],
        out_specs=pl.BlockSpec((blk, d), row_map),
        scratch_shapes=[pltpu.VMEM((d, 2 * f), BF16), pltpu.VMEM((f, d), BF16)],
    )
    return pl.pallas_call(
        _expert_kernel,
        grid_spec=grid_spec,
        out_shape=jax.ShapeDtypeStruct(xs.shape, F32),
        compiler_params=_cparams("arbitrary"),
        name="moe_experts",
    )(block_e, n_used, xs, w1_all, b1.reshape(n_exp, 1, 2 * f), w2_all, b2.reshape(n_exp, 1, d), perm)


def _combine_kernel(seg8_ref, start8_ref, gstart8_ref, tot8_ref, ys_hbm, x_ref, pos_ref, gate_ref, lg_ref,
                    lb_ref, out_ref, buf, sems, *, n_exp, dn_alpha):
    td = x_ref.shape[0]
    seg_rows = buf.shape[1]
    tabs = (seg8_ref, start8_ref, gstart8_ref, tot8_ref)
    step = pl.program_id(0)
    slot = step % 2

    def start(tile, sl):
        _segment_starts(tabs, tile, n_exp, buf.at[sl], ys_hbm, sems.at[sl], False)

    @pl.when(step == 0)
    def _():
        buf[...] = jnp.zeros_like(buf)
        start(step, slot)

    @pl.when(step + 1 < pl.num_programs(0))
    def _():
        start(step + 1, 1 - slot)

    _segment_wait(tabs, step, buf.at[slot], ys_hbm, sems.at[slot], False)

    pos = pos_ref[...]
    gates = gate_ref[...]
    acc = dn_alpha * x_ref[...]
    for r0 in range(0, seg_rows, PERM_CHUNK):
        cc = (lax.broadcasted_iota(jnp.int32, (td, PERM_CHUNK), 1) + r0).astype(F32)
        g = jnp.zeros((td, PERM_CHUNK), F32)
        for j in range(TOP_K):
            g = jnp.where(pos[:, j:j + 1] == cc, gates[:, j:j + 1], g)
        acc = acc + _dot(g.astype(BF16), buf[slot, r0:r0 + PERM_CHUNK, :].astype(BF16))
    out_ref[...] = _layer_norm(acc, lg_ref[...], lb_ref[...])


def _combine_ln(ys, pos, gates, tabs, x2, ln_g, ln_b, n_exp, dn_alpha):
    n_tok, d = x2.shape
    td = SEG_TILE
    const = lambda i, *_: (0, 0)
    tile = lambda i, *_: (i, 0)
    grid_spec = pltpu.PrefetchScalarGridSpec(
        num_scalar_prefetch=len(tabs),
        grid=(n_tok // td,),
        in_specs=[
            pl.BlockSpec(memory_space=pl.ANY),
            pl.BlockSpec((td, d), tile),
            pl.BlockSpec((td, LANES), tile),
            pl.BlockSpec((td, LANES), tile),
            pl.BlockSpec((1, d), const),
            pl.BlockSpec((1, d), const),
        ],
        out_specs=pl.BlockSpec((td, d), tile),
        scratch_shapes=[pltpu.VMEM((2, _seg_rows(td, n_exp), d), F32), pltpu.SemaphoreType.DMA((2,))],
    )
    return pl.pallas_call(
        functools.partial(_combine_kernel, n_exp=n_exp, dn_alpha=dn_alpha),
        grid_spec=grid_spec,
        out_shape=jax.ShapeDtypeStruct((n_tok, d), F32),
        compiler_params=_cparams("arbitrary"),
        name="moe_combine_ln",
    )(*tabs, ys, x2, pos, gates, ln_g.reshape(1, d), ln_b.reshape(1, d))


def _moe_layer(x2, router_w, router_b, w1_all, w2_all, layer, b1, b2, ln_g, ln_b, dn_alpha, routing=None):
    n_tok, d = x2.shape
    n_exp = w2_all.shape[1]
    n_tiles = n_tok // SEG_TILE
    blk8 = MOE_BLOCK // SEG_ALIGN
    n_blocks = -(-(n_tok * TOP_K + n_tiles * n_exp * (SEG_ALIGN - 1)) // MOE_BLOCK) + n_exp
    n_rows = n_blocks * MOE_BLOCK

    pos, gates, meta = _router(x2, router_w, router_b) if routing is None else routing

    meta = meta.reshape(n_tiles, ROW_SUBLANES, LANES)
    seg8 = meta[:, 0, :n_exp].astype(jnp.int32)
    start8 = meta[:, 1, :n_exp].astype(jnp.int32)
    padded8 = (jnp.sum(seg8, axis=0) + blk8 - 1) // blk8 * blk8
    pends8 = jnp.cumsum(padded8)
    gstart8 = (pends8 - padded8)[None, :] + jnp.cumsum(seg8, axis=0) - seg8
    block_start8 = jnp.arange(n_blocks, dtype=jnp.int32) * blk8
    block_e = jnp.minimum(jnp.sum((pends8[None, :] <= block_start8[:, None]).astype(jnp.int32), axis=1),
                          n_exp - 1).astype(jnp.int32)
    n_used = (pends8[-1:] // blk8).astype(jnp.int32)
    tabs = (seg8.reshape(-1), start8.reshape(-1), gstart8.reshape(-1).astype(jnp.int32),
            jnp.sum(seg8, axis=1).astype(jnp.int32))

    b1_l = jnp.concatenate([b1[:, 0::2], b1[:, 1::2]], axis=-1)

    xs = _dispatch(x2, pos, tabs, n_rows, n_exp)
    ys = _experts(xs, block_e, n_used, w1_all, w2_all, layer, b1_l, b2)
    return _combine_ln(ys, pos, gates, tabs, x2, ln_g, ln_b, n_exp, dn_alpha)


def _rope_lane_tables(positions):
    half = ROPE_DIM // 2
    inv_freq = ROPE_THETA ** (-jnp.arange(0, ROPE_DIM, 2, dtype=F32) / ROPE_DIM)
    ang = positions.astype(F32)[..., None] * inv_freq
    cos, sin = jnp.cos(ang), jnp.sin(ang)
    z = jnp.zeros_like(cos)
    assert 4 * half == LANES
    return (jnp.concatenate([cos, z, cos, z], axis=-1),
            jnp.concatenate([-sin, z, sin, z], axis=-1))


def _rope_lane_columns(w_rope):
    half = ROPE_DIM // 2
    z = jnp.zeros(w_rope.shape[:-1] + (half,), w_rope.dtype)
    return jnp.concatenate([w_rope[..., :half], z, w_rope[..., half:], z], axis=-1)


def kernel(x, positions, ln_g, ln_b, hg_w_in, hg_lb, hg_gnorm, hg_w_o, mla_w_dq, mla_q_norm,
           mla_w_uq, mla_w_o, kv_w_a, kv_norm, kv_w_b, router_w, router_b, moe_w1, moe_b1,
           moe_w2, moe_b2):
    bsz, s_len, d = x.shape
    depth = ln_g.shape[0]
    n_a = hg_w_in.shape[0]
    dn_alpha = (2.0 * depth) ** 0.25
    scale = (NOPE_DIM + ROPE_DIM) ** -0.5 * LOG2_E

    lb_soft = jax.nn.softmax(hg_lb.astype(F32), axis=0)
    lower_bounds = jnp.cumsum(lb_soft, axis=0) - lb_soft[0]

    q_rank = mla_w_uq.shape[1]
    mla_heads = mla_w_uq.shape[2] // (NOPE_DIM + ROPE_DIM)
    cc = ss = k_full = v_full = None

    for layer in range(depth):
        routing = None
        if layer < n_a:
            x = _hgrn_layer(x, hg_w_in[layer], lower_bounds[layer], hg_gnorm[layer], hg_w_o[layer],
                            ln_g[layer, 0], ln_b[layer, 0], dn_alpha)
        else:
            j = layer - n_a
            if layer == n_a:
                cc, ss = _rope_lane_tables(positions)
                kv_w_a_l = jnp.concatenate(
                    [kv_w_a[:, :KV_RANK], _rope_lane_columns(kv_w_a[:, KV_RANK:])], axis=-1).astype(BF16)
                k_full, v_full = _shared_kv(x, kv_w_a_l, kv_norm, kv_w_b, cc, ss)
            w_uq = mla_w_uq[j].reshape(q_rank, mla_heads, NOPE_DIM + ROPE_DIM)
            w_uq_l = jnp.concatenate(
                [w_uq[..., :NOPE_DIM], _rope_lane_columns(w_uq[..., NOPE_DIM:])],
                axis=-1).reshape(q_rank, mla_heads * QK_DIM).astype(BF16)
            q_full = _mla_queries(x, mla_w_dq[j], mla_q_norm[j], w_uq_l, cc, ss, scale)
            o = _mla_attention(q_full, k_full, v_full)
            x, routing = _proj_residual_ln_route(
                o.reshape(bsz * s_len, -1), x.reshape(bsz * s_len, d), mla_w_o[j], ln_g[layer, 0],
                ln_b[layer, 0], router_w[layer], router_b[layer], dn_alpha)
        x = _moe_layer(x.reshape(bsz * s_len, d), router_w[layer], router_b[layer], moe_w1, moe_w2, layer,
                       moe_b1[layer], moe_b2[layer], ln_g[layer, 1], ln_b[layer, 1],
                       dn_alpha, routing).reshape(bsz, s_len, d)
    return x
```

```python
import functools

import jax
import jax.numpy as jnp
from jax import lax
from jax.experimental import pallas as pl
from jax.experimental.pallas import tpu as pltpu

F32 = jnp.float32
BF16 = jnp.bfloat16

HG_HEAD_DIM = 128
NOPE_DIM = 128
ROPE_DIM = 64
V_DIM = 128
VO_DIM = 256
LOG2_E = 1.4426950408889634
KV_RANK = 128
ROPE_THETA = 10000.0
TOP_K = 4
SWIGLU_ALPHA = 1.702
SWIGLU_LIMIT = 7.0
LN_EPS = 1e-5
RMS_EPS = 1e-6

LANES = 128
ROW_SUBLANES = 8
QK_DIM = 256
VMEM_LIMIT = 56 * 1024 * 1024

ROW_TILE = 256
MLA_TILE = 512
HG_GROUP = 128
HG_HEAD_UNROLL = 8
ATT_TQ = 512
MOE_BLOCK = 512
SEG_TILE = 512
SEG_ALIGN = ROW_SUBLANES
SEG_BITS = (SEG_TILE * TOP_K // SEG_ALIGN).bit_length()
SEG_LOW_BITS = 4
PERM_CHUNK = 256


def _cparams(*sem):
    return pltpu.CompilerParams(dimension_semantics=sem, vmem_limit_bytes=VMEM_LIMIT)


def _layer_norm(y, g, b):
    mu = jnp.mean(y, axis=-1, keepdims=True)
    d = y - mu
    var = jnp.mean(d * d, axis=-1, keepdims=True)
    return d * lax.rsqrt(var + LN_EPS) * g + b


def _dot(a, b):
    return jnp.dot(a, b, preferred_element_type=F32)


def _dot_nt(a, b):
    return lax.dot_general(a, b, (((1,), (1,)), ((), ())), preferred_element_type=F32)


def _dot_tn(a, b):
    return lax.dot_general(a, b, (((0,), (0,)), ((), ())), preferred_element_type=F32)


def _sigmoid(x):
    return 0.5 * jnp.tanh(0.5 * x) + 0.5


def _hgrn_kernel(x_ref, w_in_ref, lb_ref, gn_ref, w_o_ref, lg_ref, lbias_ref, out_ref,
                 q_s, f_s, i_s, g_s, mix_s, st_s, *, dn_alpha):
    ts, d = x_ref.shape[1], x_ref.shape[2]
    n_heads = d // HG_HEAD_DIM
    grp = HG_GROUP

    @pl.when(pl.program_id(1) == 0)
    def _():
        st_s[...] = jnp.zeros_like(st_s)

    x = x_ref[0]
    xb = x.astype(BF16)
    lb = lb_ref[...]

    for sec, dst in enumerate((q_s, f_s, i_s, g_s)):
        p = _dot(xb, w_in_ref[:, sec * d:(sec + 1) * d])
        if sec == 0 or sec == 3:
            p = p * _sigmoid(p)
        elif sec == 1:
            p = lb + (1.0 - lb) * _sigmoid(p)
        for h in range(n_heads):
            dst[h] = p[:, h * HG_HEAD_DIM:(h + 1) * HG_HEAD_DIM]

    row = lax.broadcasted_iota(jnp.int32, (grp, grp), 0)
    col = lax.broadcasted_iota(jnp.int32, (grp, grp), 1)
    n_levels = grp.bit_length() - 1
    pair_masks = [(((row >> lvl) ^ (col >> lvl)) == 1) & (row > col) for lvl in range(n_levels)]

    def head_body(h, carry):
        for r0 in range(0, ts, grp):
            q = q_s[h, r0:r0 + grp, :]
            fg = f_s[h, r0:r0 + grp, :]
            iv = i_s[h, r0:r0 + grp, :].astype(BF16)
            k = 1.0 - fg
            ep, es, et = fg, None, fg
            a = jnp.where(row == col, _dot_nt(q.astype(BF16), k.astype(BF16)), 0.0)
            for lvl in range(n_levels):
                half = 1 << lvl
                kl = k if es is None else k * es
                a = jnp.where(pair_masks[lvl], _dot_nt((q * ep).astype(BF16), kl.astype(BF16)), a)
                if half < ROW_SUBLANES:
                    odd = (row & half) != 0
                    et3 = et.reshape(grp // ROW_SUBLANES, ROW_SUBLANES, LANES)
                    other = jnp.where(odd, pltpu.roll(et3, half, 1).reshape(et.shape),
                                      pltpu.roll(et3, ROW_SUBLANES - half, 1).reshape(et.shape))
                    ep = jnp.where(odd, ep * other, ep)
                    es = jnp.where(odd, 1.0, other) if es is None else jnp.where(odd, es, es * other)
                    et = et * other
                else:
                    ep_p, es_p, et_p = [], [], []
                    for b0 in range(0, grp, 2 * half):
                        lo, mid, hi = b0, b0 + half, b0 + 2 * half
                        tot = et[lo:mid] * et[mid:hi]
                        ep_p += [ep[lo:mid], ep[mid:hi] * et[lo:mid]]
                        es_p += [es[lo:mid] * et[mid:hi], es[mid:hi]]
                        et_p += [tot, tot]
                    ep = jnp.concatenate(ep_p, axis=0)
                    es = jnp.concatenate(es_p, axis=0)
                    et = jnp.concatenate(et_p, axis=0)
            st = st_s[h]
            o = _dot(a.astype(BF16), iv) + _dot_nt((q * ep).astype(BF16), st.astype(BF16))
            st_s[h] = st * et[0:1, :] + _dot_tn(iv, (k * es).astype(BF16))
            ms = jnp.mean(o * o, axis=-1, keepdims=True)
            y = o * lax.rsqrt(ms + RMS_EPS) * gn_ref[...] * g_s[h, r0:r0 + grp, :]
            mix_s[h, r0:r0 + grp, :] = y.astype(BF16)
        return carry

    lax.fori_loop(0, n_heads, head_body, 0, unroll=HG_HEAD_UNROLL)

    acc = dn_alpha * x
    for h in range(0, n_heads, 2):
        acc = acc + _dot(jnp.concatenate([mix_s[h], mix_s[h + 1]], axis=-1), w_o_ref[h // 2])
    out_ref[0] = _layer_norm(acc, lg_ref[...], lbias_ref[...])


def _hgrn_layer(x, w_in, lb, gnorm, w_o, ln_g, ln_b, dn_alpha):
    bsz, s_len, d = x.shape
    n_heads = d // HG_HEAD_DIM
    ts = ROW_TILE
    const2 = lambda b, s: (0, 0)
    gate_scratch = pltpu.VMEM((n_heads, ts, HG_HEAD_DIM), F32)
    return pl.pallas_call(
        functools.partial(_hgrn_kernel, dn_alpha=dn_alpha),
        grid=(bsz, s_len // ts),
        in_specs=[
            pl.BlockSpec((1, ts, d), lambda b, s: (b, s, 0)),
            pl.BlockSpec((d, 4 * d), const2),
            pl.BlockSpec((1, d), const2),
            pl.BlockSpec((1, HG_HEAD_DIM), const2),
            pl.BlockSpec((n_heads // 2, 2 * HG_HEAD_DIM, d), lambda b, s: (0, 0, 0)),
            pl.BlockSpec((1, d), const2),
            pl.BlockSpec((1, d), const2),
        ],
        out_specs=pl.BlockSpec((1, ts, d), lambda b, s: (b, s, 0)),
        out_shape=jax.ShapeDtypeStruct((bsz, s_len, d), F32),
        scratch_shapes=[
            gate_scratch, gate_scratch, gate_scratch, gate_scratch,
            pltpu.VMEM((n_heads, ts, HG_HEAD_DIM), BF16),
            pltpu.VMEM((n_heads, HG_HEAD_DIM, HG_HEAD_DIM), F32),
        ],
        compiler_params=_cparams("parallel", "arbitrary"),
        name="hgrn2_layer",
    )(x, w_in.astype(BF16), lb.reshape(1, d), gnorm.reshape(1, HG_HEAD_DIM),
      w_o.astype(BF16).reshape(n_heads // 2, 2 * HG_HEAD_DIM, d), ln_g.reshape(1, d), ln_b.reshape(1, d))


def _rope_lanes(t, cc, ss):
    return t * cc + pltpu.roll(t, LANES // 2, 1) * ss


def _kv_kernel(x_ref, wa_ref, kvn_ref, wb_ref, cc_ref, ss_ref, k_ref, v_ref):
    n_heads = k_ref.shape[1]
    xb = x_ref[0].astype(BF16)
    ckr = _dot(xb, wa_ref[...])
    c = ckr[:, :KV_RANK]
    c = c * lax.rsqrt(jnp.mean(c * c, axis=-1, keepdims=True) + RMS_EPS) * kvn_ref[...]
    kr = _rope_lanes(ckr[:, KV_RANK:], cc_ref[0], ss_ref[0]).astype(BF16)
    cb = c.astype(BF16)
    for h in range(n_heads):
        kv = _dot(cb, wb_ref[:, h * (NOPE_DIM + V_DIM):(h + 1) * (NOPE_DIM + V_DIM)])
        k_ref[0, h, :, :NOPE_DIM] = kv[:, :NOPE_DIM].astype(BF16)
        k_ref[0, h, :, NOPE_DIM:] = kr
        v_ref[0, h, :, :V_DIM] = kv[:, NOPE_DIM:].astype(BF16)
        v_ref[0, h, :, V_DIM:] = jnp.ones((kv.shape[0], VO_DIM - V_DIM), BF16)


def _shared_kv(x, kv_w_a_l, kv_norm, kv_w_b, cc, ss):
    bsz, s_len, d = x.shape
    n_heads = kv_w_b.shape[1] // (NOPE_DIM + V_DIM)
    ts = MLA_TILE
    const2 = lambda b, s: (0, 0)
    return pl.pallas_call(
        _kv_kernel,
        grid=(bsz, s_len // ts),
        in_specs=[
            pl.BlockSpec((1, ts, d), lambda b, s: (b, s, 0)),
            pl.BlockSpec((d, KV_RANK + LANES), const2),
            pl.BlockSpec((1, KV_RANK), const2),
            pl.BlockSpec((KV_RANK, n_heads * (NOPE_DIM + V_DIM)), const2),
            pl.BlockSpec((1, ts, LANES), lambda b, s: (b, s, 0)),
            pl.BlockSpec((1, ts, LANES), lambda b, s: (b, s, 0)),
        ],
        out_specs=[
            pl.BlockSpec((1, n_heads, ts, QK_DIM), lambda b, s: (b, 0, s, 0)),
            pl.BlockSpec((1, n_heads, ts, VO_DIM), lambda b, s: (b, 0, s, 0)),
        ],
        out_shape=[
            jax.ShapeDtypeStruct((bsz, n_heads, s_len, QK_DIM), BF16),
            jax.ShapeDtypeStruct((bsz, n_heads, s_len, VO_DIM), BF16),
        ],
        compiler_params=_cparams("parallel", "parallel"),
        name="mla_shared_kv",
    )(x, kv_w_a_l, kv_norm.reshape(1, KV_RANK), kv_w_b.astype(BF16), cc, ss)


def _q_kernel(x_ref, wdq_ref, qn_ref, wuq_ref, cc_ref, ss_ref, q_ref, *, scale):
    n_heads = q_ref.shape[1]
    xb = x_ref[0].astype(BF16)
    c = _dot(xb, wdq_ref[...])
    c = c * lax.rsqrt(jnp.mean(c * c, axis=-1, keepdims=True) + RMS_EPS) * qn_ref[...]
    cb = c.astype(BF16)
    cc = cc_ref[0] * scale
    ss = ss_ref[0] * scale
    for h in range(n_heads):
        qh = _dot(cb, wuq_ref[:, h * QK_DIM:(h + 1) * QK_DIM])
        q_ref[0, h, :, :NOPE_DIM] = (qh[:, :NOPE_DIM] * scale).astype(BF16)
        q_ref[0, h, :, NOPE_DIM:] = _rope_lanes(qh[:, NOPE_DIM:], cc, ss).astype(BF16)


def _mla_queries(x, w_dq, q_norm, w_uq_l, cc, ss, scale):
    bsz, s_len, d = x.shape
    q_rank = w_dq.shape[1]
    n_heads = w_uq_l.shape[1] // QK_DIM
    ts = MLA_TILE
    const2 = lambda b, s: (0, 0)
    return pl.pallas_call(
        functools.partial(_q_kernel, scale=scale),
        grid=(bsz, s_len // ts),
        in_specs=[
            pl.BlockSpec((1, ts, d), lambda b, s: (b, s, 0)),
            pl.BlockSpec((d, q_rank), const2),
            pl.BlockSpec((1, q_rank), const2),
            pl.BlockSpec((q_rank, n_heads * QK_DIM), const2),
            pl.BlockSpec((1, ts, LANES), lambda b, s: (b, s, 0)),
            pl.BlockSpec((1, ts, LANES), lambda b, s: (b, s, 0)),
        ],
        out_specs=pl.BlockSpec((1, n_heads, ts, QK_DIM), lambda b, s: (b, 0, s, 0)),
        out_shape=jax.ShapeDtypeStruct((bsz, n_heads, s_len, QK_DIM), BF16),
        compiler_params=_cparams("parallel", "parallel"),
        name="mla_queries",
    )(x, w_dq.astype(BF16), q_norm.reshape(1, q_rank), w_uq_l, cc, ss)


def _attn_kernel(q_ref, k_ref, v_ref, o_ref):
    s_len = q_ref.shape[2]
    tq = min(ATT_TQ, s_len)
    row = lax.broadcasted_iota(jnp.int32, (tq, tq), 0)
    col = lax.broadcasted_iota(jnp.int32, (tq, tq), 1)
    for qi in range(s_len // tq):
        q = q_ref[0, 0, qi * tq:(qi + 1) * tq, :]
        m = jnp.full((tq, 1), -jnp.inf, F32)
        acc = jnp.zeros((tq, VO_DIM), F32)
        for kj in range(qi + 1):
            s = _dot_nt(q, k_ref[0, 0, kj * tq:(kj + 1) * tq, :])
            if kj == qi:
                s = jnp.where(col <= row, s, -jnp.inf)
            m_new = jnp.maximum(m, jnp.max(s, axis=-1, keepdims=True))
            p = jnp.exp2(s - m_new)
            acc = jnp.exp2(m - m_new) * acc + _dot(p.astype(BF16), v_ref[0, 0, kj * tq:(kj + 1) * tq, :])
            m = m_new
        o_ref[0, qi * tq:(qi + 1) * tq, :] = (acc[:, :V_DIM] / acc[:, V_DIM:V_DIM + 1]).astype(o_ref.dtype)


def _mla_attention(q, k, v):
    bsz, n_heads, s_len, _ = q.shape
    return pl.pallas_call(
        _attn_kernel,
        grid=(bsz, n_heads),
        in_specs=[
            pl.BlockSpec((1, 1, s_len, QK_DIM), lambda b, h: (b, h, 0, 0)),
            pl.BlockSpec((1, 1, s_len, QK_DIM), lambda b, h: (b, h, 0, 0)),
            pl.BlockSpec((1, 1, s_len, VO_DIM), lambda b, h: (b, h, 0, 0)),
        ],
        out_specs=pl.BlockSpec((1, s_len, V_DIM), lambda b, h: (b, 0, h)),
        out_shape=jax.ShapeDtypeStruct((bsz, s_len, n_heads * V_DIM), BF16),
        compiler_params=_cparams("parallel", "parallel"),
        name="mla_attention",
    )(q, k, v)


def _split_hi_lo(v):
    hi = lax.bitcast_convert_type(lax.bitcast_convert_type(v, jnp.uint32) & jnp.uint32(0xFFFF0000), F32)
    return hi.astype(BF16), (v - hi).astype(BF16)


def _route_tile(x, whi_ref, wlo_ref, b_ref, pos_ref, gate_ref, meta_ref):
    tr = x.shape[0]

    x_hi, x_lo = _split_hi_lo(x)
    logits = (_dot(x_hi, whi_ref[...]) + (_dot(x_lo, whi_ref[...]) + _dot(x_hi, wlo_ref[...]))
              + b_ref[...])
    lane = lax.broadcasted_iota(jnp.int32, (tr, LANES), 1)
    work = logits
    sel = jnp.zeros((tr, LANES), F32)
    ids, vals = [], []
    for _ in range(TOP_K):
        mx = jnp.max(work, axis=-1, keepdims=True)
        idx = jnp.min(jnp.where(work == mx, lane, LANES), axis=-1, keepdims=True)
        hit = lane == idx
        ids.append(idx)
        vals.append(mx)
        sel = jnp.where(hit, 1.0, sel)
        work = jnp.where(hit, -jnp.inf, work)
    exps = [jnp.exp(v - vals[0]) for v in vals]
    denom = exps[0] + exps[1] + exps[2] + exps[3]

    r = lax.broadcasted_iota(jnp.int32, (tr, tr), 0)
    c = lax.broadcasted_iota(jnp.int32, (tr, tr), 1)
    before = _dot(jnp.where(c < r, 1.0, 0.0).astype(BF16), sel.astype(BF16))
    seg8 = jnp.floor((jnp.sum(sel, axis=0, keepdims=True) + (SEG_ALIGN - 1)) * (1.0 / SEG_ALIGN))
    er = lax.broadcasted_iota(jnp.int32, (LANES, LANES), 0)
    ec = lax.broadcasted_iota(jnp.int32, (LANES, LANES), 1)
    start8 = _dot(jnp.broadcast_to(seg8, (ROW_SUBLANES, LANES)).astype(BF16),
                  jnp.where(er < ec, 1.0, 0.0).astype(BF16))[0:1]
    slot = start8 * SEG_ALIGN + before

    pos_out = jnp.zeros((tr, LANES), F32)
    gate_out = jnp.zeros((tr, LANES), F32)
    for j in range(TOP_K):
        pos = jnp.sum(jnp.where(lane == ids[j], slot, 0.0), axis=-1, keepdims=True)
        pos_out = jnp.where(lane == j, pos, pos_out)
        gate_out = jnp.where(lane == j, exps[j] / denom, gate_out)
    pos_ref[...] = pos_out
    gate_ref[...] = gate_out
    row8 = lax.broadcasted_iota(jnp.int32, (ROW_SUBLANES, LANES), 0)
    meta_ref[...] = jnp.where(row8 == 0, seg8, jnp.where(row8 == 1, start8, 0.0))


def _router_kernel(x_ref, whi_ref, wlo_ref, b_ref, pos_ref, gate_ref, meta_ref):
    _route_tile(x_ref[...], whi_ref, wlo_ref, b_ref, pos_ref, gate_ref, meta_ref)


def _router_operands(router_w, router_b):
    d, n_exp = router_w.shape
    w_hi, w_lo = _split_hi_lo(jnp.zeros((d, LANES), F32).at[:, :n_exp].set(router_w))
    return w_hi, w_lo, jnp.full((1, LANES), -jnp.inf, F32).at[0, :n_exp].set(router_b)


def _router_specs(d, n_tok):
    const = lambda i: (0, 0)
    tile = lambda i: (i, 0)
    in_specs = [pl.BlockSpec((d, LANES), const), pl.BlockSpec((d, LANES), const), pl.BlockSpec((1, LANES), const)]
    out_specs = [pl.BlockSpec((SEG_TILE, LANES), tile), pl.BlockSpec((SEG_TILE, LANES), tile),
                 pl.BlockSpec((ROW_SUBLANES, LANES), tile)]
    out_shape = [
        jax.ShapeDtypeStruct((n_tok, LANES), F32),
        jax.ShapeDtypeStruct((n_tok, LANES), F32),
        jax.ShapeDtypeStruct((n_tok // SEG_TILE * ROW_SUBLANES, LANES), F32),
    ]
    return in_specs, out_specs, out_shape


def _router(x2, router_w, router_b):
    n_tok, d = x2.shape
    in_specs, out_specs, out_shape = _router_specs(d, n_tok)
    return pl.pallas_call(
        _router_kernel,
        grid=(n_tok // SEG_TILE,),
        in_specs=[pl.BlockSpec((SEG_TILE, d), lambda i: (i, 0))] + in_specs,
        out_specs=out_specs,
        out_shape=out_shape,
        compiler_params=_cparams("parallel"),
        name="moe_router",
    )(x2, *_router_operands(router_w, router_b))


def _proj_ln_route_kernel(o_ref, x_ref, w_ref, lg_ref, lb_ref, whi_ref, wlo_ref, rb_ref,
                          out_ref, pos_ref, gate_ref, meta_ref, *, dn_alpha):
    y = _layer_norm(dn_alpha * x_ref[...] + _dot(o_ref[...], w_ref[...]), lg_ref[...], lb_ref[...])
    out_ref[...] = y
    _route_tile(y, whi_ref, wlo_ref, rb_ref, pos_ref, gate_ref, meta_ref)


def _proj_residual_ln_route(o2, x2, w_o, ln_g, ln_b, router_w, router_b, dn_alpha):
    n_tok, d = x2.shape
    kdim = o2.shape[1]
    ts = SEG_TILE
    const = lambda i: (0, 0)
    r_in, r_out, r_shape = _router_specs(d, n_tok)
    res = pl.pallas_call(
        functools.partial(_proj_ln_route_kernel, dn_alpha=dn_alpha),
        grid=(n_tok // ts,),
        in_specs=[
            pl.BlockSpec((ts, kdim), lambda i: (i, 0)),
            pl.BlockSpec((ts, d), lambda i: (i, 0)),
            pl.BlockSpec((kdim, d), const),
            pl.BlockSpec((1, d), const),
            pl.BlockSpec((1, d), const),
        ] + r_in,
        out_specs=[pl.BlockSpec((ts, d), lambda i: (i, 0))] + r_out,
        out_shape=[jax.ShapeDtypeStruct((n_tok, d), F32)] + r_shape,
        compiler_params=_cparams("parallel"),
        name="mla_out_proj_ln_route",
    )(o2, x2, w_o.astype(BF16), ln_g.reshape(1, d), ln_b.reshape(1, d), *_router_operands(router_w, router_b))
    return res[0], tuple(res[1:])


def _segment_copy(local_ref, hbm_ref, sem, to_hbm, lo8, go8, bit):
    rows = SEG_ALIGN << bit
    aligned = lambda v8: v8 * SEG_ALIGN if isinstance(v8, int) else pl.multiple_of(v8 * SEG_ALIGN, SEG_ALIGN)
    loc = local_ref.at[pl.ds(aligned(lo8), rows)]
    hbm = hbm_ref.at[pl.ds(aligned(go8), rows)]
    return pltpu.make_async_copy(loc, hbm, sem) if to_hbm else pltpu.make_async_copy(hbm, loc, sem)


def _segment_starts(tabs, tile, n_exp, local_ref, hbm_ref, sem, to_hbm):
    seg8_ref, start8_ref, gstart8_ref, _ = tabs

    def body(e, c):
        n = seg8_ref[tile * n_exp + e]
        lo = start8_ref[tile * n_exp + e]
        go = gstart8_ref[tile * n_exp + e]

        def bits(lo_bit, hi_bit):
            for bit in range(lo_bit, hi_bit):
                @pl.when(((n >> bit) & 1) == 1)
                def _():
                    off = (n >> (bit + 1)) << (bit + 1)
                    _segment_copy(local_ref, hbm_ref, sem, to_hbm, lo + off, go + off, bit).start()

        bits(0, SEG_LOW_BITS)

        @pl.when(n >= (1 << SEG_LOW_BITS))
        def _():
            bits(SEG_LOW_BITS, SEG_BITS)
        return c

    lax.fori_loop(0, n_exp, body, 0)


def _segment_wait(tabs, tile, local_ref, hbm_ref, sem, to_hbm):
    total = tabs[3][tile]
    for bit in range(SEG_BITS):
        @pl.when(((total >> bit) & 1) == 1)
        def _():
            _segment_copy(local_ref, hbm_ref, sem, to_hbm, 0, 0, bit).wait()


def _dispatch_kernel(seg8_ref, start8_ref, gstart8_ref, tot8_ref, x_ref, pos_ref, xs_hbm, buf, sems, *, n_exp):
    td = x_ref.shape[0]
    seg_rows = buf.shape[1]
    tabs = (seg8_ref, start8_ref, gstart8_ref, tot8_ref)
    step = pl.program_id(0)
    slot = step % 2

    def wait(tile, sl):
        _segment_wait(tabs, tile, buf.at[sl], xs_hbm, sems.at[sl], True)

    @pl.when(step >= 2)
    def _():
        wait(step - 2, slot)

    xb = x_ref[...].astype(BF16)
    pos_t = pos_ref[...].T
    for r0 in range(0, seg_rows, PERM_CHUNK):
        rr = (lax.broadcasted_iota(jnp.int32, (PERM_CHUNK, td), 0) + r0).astype(F32)
        hit = pos_t[0:1] == rr
        for j in range(1, TOP_K):
            hit = hit | (pos_t[j:j + 1] == rr)
        buf[slot, r0:r0 + PERM_CHUNK, :] = _dot(jnp.where(hit, 1.0, 0.0).astype(BF16), xb)

    _segment_starts(tabs, step, n_exp, buf.at[slot], xs_hbm, sems.at[slot], True)

    @pl.when(step == pl.num_programs(0) - 1)
    def _():
        wait(step, slot)

        @pl.when(step >= 1)
        def _():
            wait(step - 1, 1 - slot)


def _seg_rows(td, n_exp):
    return -(-(td * TOP_K + n_exp * (SEG_ALIGN - 1)) // PERM_CHUNK) * PERM_CHUNK


def _dispatch(x2, pos, tabs, n_rows, n_exp):
    n_tok, d = x2.shape
    td = SEG_TILE
    grid_spec = pltpu.PrefetchScalarGridSpec(
        num_scalar_prefetch=len(tabs),
        grid=(n_tok // td,),
        in_specs=[
            pl.BlockSpec((td, d), lambda i, *_: (i, 0)),
            pl.BlockSpec((td, LANES), lambda i, *_: (i, 0)),
        ],
        out_specs=pl.BlockSpec(memory_space=pl.ANY),
        scratch_shapes=[pltpu.VMEM((2, _seg_rows(td, n_exp), d), F32), pltpu.SemaphoreType.DMA((2,))],
    )
    return pl.pallas_call(
        functools.partial(_dispatch_kernel, n_exp=n_exp),
        grid_spec=grid_spec,
        out_shape=jax.ShapeDtypeStruct((n_rows, d), F32),
        compiler_params=_cparams("arbitrary"),
        name="moe_dispatch",
    )(*tabs, x2, pos)


def _expert_kernel(be_ref, nu_ref, nxt_ref, xs_ref, w1_hbm, b1_ref, w2_hbm, b2_ref, perm_ref, ys_ref,
                   w1f_s, w2f_s, w1b_s, w2b_s, sems, *, layer):
    step = pl.program_id(0)
    d, f = w2b_s.shape[1], w2b_s.shape[0]
    active = step < nu_ref[0]
    new_expert = jnp.logical_or(step == 0, be_ref[step] != be_ref[jnp.maximum(step - 1, 0)])

    def fetch(e):
        return (pltpu.make_async_copy(w1_hbm.at[layer, e], w1f_s, sems.at[0]),
                pltpu.make_async_copy(w2_hbm.at[layer, e], w2f_s, sems.at[1]))

    @pl.when(jnp.logical_and(active, step == 0))
    def _():
        for cp in fetch(be_ref[0]):
            cp.start()

    @pl.when(jnp.logical_and(active, new_expert))
    def _():
        for cp in fetch(be_ref[step]):
            cp.wait()
        width = perm_ref.shape[0]
        for r0 in range(0, d, ROW_TILE):
            for c in range(2 * f // width):
                t = _dot(w1f_s[r0:r0 + ROW_TILE, c * width:(c + 1) * width].astype(BF16), perm_ref[...])
                w1b_s[r0:r0 + ROW_TILE, c * (width // 2):(c + 1) * (width // 2)] = t[:, :width // 2].astype(BF16)
                w1b_s[r0:r0 + ROW_TILE, f + c * (width // 2):f + (c + 1) * (width // 2)] = (
                    t[:, width // 2:].astype(BF16))
        w2b_s[...] = w2f_s[...].astype(BF16)

        @pl.when(nxt_ref[step] >= 0)
        def _():
            for cp in fetch(nxt_ref[step]):
                cp.start()

    @pl.when(active)
    def _():
        h = _dot(xs_ref[...].astype(BF16), w1b_s[...]) + b1_ref[0]
        glu = jnp.minimum(h[:, :f], SWIGLU_LIMIT)
        lin = jnp.clip(h[:, f:], -SWIGLU_LIMIT, SWIGLU_LIMIT)
        a = glu * _sigmoid(SWIGLU_ALPHA * glu) * (lin + 1.0)
        ys_ref[...] = _dot(a.astype(BF16), w2b_s[...]) + b2_ref[0]


def _experts(xs, block_e, n_used, next_e, w1_all, w2_all, layer, b1, b2):
    _, n_exp, f, d = w2_all.shape
    blk = MOE_BLOCK
    n_blocks = xs.shape[0] // blk
    width = 2 * LANES
    r = lax.broadcasted_iota(jnp.int32, (width, width), 0)
    c = lax.broadcasted_iota(jnp.int32, (width, width), 1)
    perm = (c == (r % 2) * (width // 2) + r // 2).astype(BF16)
    row_map = lambda i, be, nu, nx: (jnp.minimum(i, nu[0] - 1), 0)
    exp_map = lambda i, be, nu, nx: (be[i], 0, 0)
    grid_spec = pltpu.PrefetchScalarGridSpec(
        num_scalar_prefetch=3,
        grid=(n_blocks,),
        in_specs=[
            pl.BlockSpec((blk, d), row_map),
            pl.BlockSpec(memory_space=pl.ANY),
            pl.BlockSpec((1, 1, 2 * f), exp_map),
            pl.BlockSpec(memory_space=pl.ANY),
            pl.BlockSpec((1, 1, d), exp_map),
            pl.BlockSpec((width, width), lambda i, be, nu, nx: (0, 0)),
        ],
        out_specs=pl.BlockSpec((blk, d), row_map),
        scratch_shapes=[pltpu.VMEM((d, 2 * f), F32), pltpu.VMEM((f, d), F32),
                        pltpu.VMEM((d, 2 * f), BF16), pltpu.VMEM((f, d), BF16),
                        pltpu.SemaphoreType.DMA((2,))],
    )
    return pl.pallas_call(
        functools.partial(_expert_kernel, layer=layer),
        grid_spec=grid_spec,
        out_shape=jax.ShapeDtypeStruct(xs.shape, F32),
        compiler_params=_cparams("arbitrary"),
        name="moe_experts",
    )(block_e, n_used, next_e, xs, w1_all, b1.reshape(n_exp, 1, 2 * f), w2_all, b2.reshape(n_exp, 1, d), perm)


def _combine_kernel(seg8_ref, start8_ref, gstart8_ref, tot8_ref, ys_hbm, x_ref, pos_ref, gate_ref, lg_ref,
                    lb_ref, out_ref, buf, sems, *, n_exp, dn_alpha):
    td = x_ref.shape[0]
    seg_rows = buf.shape[1]
    tabs = (seg8_ref, start8_ref, gstart8_ref, tot8_ref)
    step = pl.program_id(0)
    slot = step % 2

    def start(tile, sl):
        _segment_starts(tabs, tile, n_exp, buf.at[sl], ys_hbm, sems.at[sl], False)

    @pl.when(step == 0)
    def _():
        buf[...] = jnp.zeros_like(buf)
        start(step, slot)

    @pl.when(step + 1 < pl.num_programs(0))
    def _():
        start(step + 1, 1 - slot)

    _segment_wait(tabs, step, buf.at[slot], ys_hbm, sems.at[slot], False)

    pos = pos_ref[...]
    gates = gate_ref[...]
    acc = dn_alpha * x_ref[...]
    for r0 in range(0, seg_rows, PERM_CHUNK):
        cc = (lax.broadcasted_iota(jnp.int32, (td, PERM_CHUNK), 1) + r0).astype(F32)
        g = jnp.zeros((td, PERM_CHUNK), F32)
        for j in range(TOP_K):
            g = jnp.where(pos[:, j:j + 1] == cc, gates[:, j:j + 1], g)
        acc = acc + _dot(g.astype(BF16), buf[slot, r0:r0 + PERM_CHUNK, :].astype(BF16))
    out_ref[...] = _layer_norm(acc, lg_ref[...], lb_ref[...])


def _combine_ln(ys, pos, gates, tabs, x2, ln_g, ln_b, n_exp, dn_alpha):
    n_tok, d = x2.shape
    td = SEG_TILE
    const = lambda i, *_: (0, 0)
    tile = lambda i, *_: (i, 0)
    grid_spec = pltpu.PrefetchScalarGridSpec(
        num_scalar_prefetch=len(tabs),
        grid=(n_tok // td,),
        in_specs=[
            pl.BlockSpec(memory_space=pl.ANY),
            pl.BlockSpec((td, d), tile),
            pl.BlockSpec((td, LANES), tile),
            pl.BlockSpec((td, LANES), tile),
            pl.BlockSpec((1, d), const),
            pl.BlockSpec((1, d), const),
        ],
        out_specs=pl.BlockSpec((td, d), tile),
        scratch_shapes=[pltpu.VMEM((2, _seg_rows(td, n_exp), d), F32), pltpu.SemaphoreType.DMA((2,))],
    )
    return pl.pallas_call(
        functools.partial(_combine_kernel, n_exp=n_exp, dn_alpha=dn_alpha),
        grid_spec=grid_spec,
        out_shape=jax.ShapeDtypeStruct((n_tok, d), F32),
        compiler_params=_cparams("arbitrary"),
        name="moe_combine_ln",
    )(*tabs, ys, x2, pos, gates, ln_g.reshape(1, d), ln_b.reshape(1, d))


def _moe_layer(x2, router_w, router_b, w1_all, w2_all, layer, b1, b2, ln_g, ln_b, dn_alpha, routing=None):
    n_tok, d = x2.shape
    n_exp = w2_all.shape[1]
    n_tiles = n_tok // SEG_TILE
    blk8 = MOE_BLOCK // SEG_ALIGN
    n_blocks = -(-(n_tok * TOP_K + n_tiles * n_exp * (SEG_ALIGN - 1)) // MOE_BLOCK) + n_exp
    n_rows = n_blocks * MOE_BLOCK

    pos, gates, meta = _router(x2, router_w, router_b) if routing is None else routing

    meta = meta.reshape(n_tiles, ROW_SUBLANES, LANES)
    seg8 = meta[:, 0, :n_exp].astype(jnp.int32)
    start8 = meta[:, 1, :n_exp].astype(jnp.int32)
    padded8 = (jnp.sum(seg8, axis=0) + blk8 - 1) // blk8 * blk8
    pends8 = jnp.cumsum(padded8)
    gstart8 = (pends8 - padded8)[None, :] + jnp.cumsum(seg8, axis=0) - seg8
    block_start8 = jnp.arange(n_blocks, dtype=jnp.int32) * blk8
    block_e = jnp.minimum(jnp.sum((pends8[None, :] <= block_start8[:, None]).astype(jnp.int32), axis=1),
                          n_exp - 1).astype(jnp.int32)
    n_used = (pends8[-1:] // blk8).astype(jnp.int32)
    ids = jnp.arange(n_exp, dtype=jnp.int32)
    later = jnp.where((ids[None, :] > ids[:, None]) & (padded8[None, :] > 0), ids[None, :], n_exp)
    next_present = jnp.min(later, axis=1)
    next_e = jnp.where(next_present < n_exp, next_present, -1)[block_e].astype(jnp.int32)
    tabs = (seg8.reshape(-1), start8.reshape(-1), gstart8.reshape(-1).astype(jnp.int32),
            jnp.sum(seg8, axis=1).astype(jnp.int32))

    b1_l = jnp.concatenate([b1[:, 0::2], b1[:, 1::2]], axis=-1)

    xs = _dispatch(x2, pos, tabs, n_rows, n_exp)
    ys = _experts(xs, block_e, n_used, next_e, w1_all, w2_all, layer, b1_l, b2)
    return _combine_ln(ys, pos, gates, tabs, x2, ln_g, ln_b, n_exp, dn_alpha)


def _rope_lane_tables(positions):
    half = ROPE_DIM // 2
    inv_freq = ROPE_THETA ** (-jnp.arange(0, ROPE_DIM, 2, dtype=F32) / ROPE_DIM)
    ang = positions.astype(F32)[..., None] * inv_freq
    cos, sin = jnp.cos(ang), jnp.sin(ang)
    z = jnp.zeros_like(cos)
    assert 4 * half == LANES
    return (jnp.concatenate([cos, z, cos, z], axis=-1),
            jnp.concatenate([-sin, z, sin, z], axis=-1))


def _rope_lane_columns(w_rope):
    half = ROPE_DIM // 2
    z = jnp.zeros(w_rope.shape[:-1] + (half,), w_rope.dtype)
    return jnp.concatenate([w_rope[..., :half], z, w_rope[..., half:], z], axis=-1)


def kernel(x, positions, ln_g, ln_b, hg_w_in, hg_lb, hg_gnorm, hg_w_o, mla_w_dq, mla_q_norm,
           mla_w_uq, mla_w_o, kv_w_a, kv_norm, kv_w_b, router_w, router_b, moe_w1, moe_b1,
           moe_w2, moe_b2):
    bsz, s_len, d = x.shape
    depth = ln_g.shape[0]
    n_a = hg_w_in.shape[0]
    dn_alpha = (2.0 * depth) ** 0.25
    scale = (NOPE_DIM + ROPE_DIM) ** -0.5 * LOG2_E

    lb_soft = jax.nn.softmax(hg_lb.astype(F32), axis=0)
    lower_bounds = jnp.cumsum(lb_soft, axis=0) - lb_soft[0]

    q_rank = mla_w_uq.shape[1]
    mla_heads = mla_w_uq.shape[2] // (NOPE_DIM + ROPE_DIM)
    cc = ss = k_full = v_full = None

    for layer in range(depth):
        routing = None
        if layer < n_a:
            x = _hgrn_layer(x, hg_w_in[layer], lower_bounds[layer], hg_gnorm[layer], hg_w_o[layer],
                            ln_g[layer, 0], ln_b[layer, 0], dn_alpha)
        else:
            j = layer - n_a
            if layer == n_a:
                cc, ss = _rope_lane_tables(positions)
                kv_w_a_l = jnp.concatenate(
                    [kv_w_a[:, :KV_RANK], _rope_lane_columns(kv_w_a[:, KV_RANK:])], axis=-1).astype(BF16)
                k_full, v_full = _shared_kv(x, kv_w_a_l, kv_norm, kv_w_b, cc, ss)
            w_uq = mla_w_uq[j].reshape(q_rank, mla_heads, NOPE_DIM + ROPE_DIM)
            w_uq_l = jnp.concatenate(
                [w_uq[..., :NOPE_DIM], _rope_lane_columns(w_uq[..., NOPE_DIM:])],
                axis=-1).reshape(q_rank, mla_heads * QK_DIM).astype(BF16)
            q_full = _mla_queries(x, mla_w_dq[j], mla_q_norm[j], w_uq_l, cc, ss, scale)
            o = _mla_attention(q_full, k_full, v_full)
            x, routing = _proj_residual_ln_route(
                o.reshape(bsz * s_len, -1), x.reshape(bsz * s_len, d), mla_w_o[j], ln_g[layer, 0],
                ln_b[layer, 0], router_w[layer], router_b[layer], dn_alpha)
        x = _moe_layer(x.reshape(bsz * s_len, d), router_w[layer], router_b[layer], moe_w1, moe_w2, layer,
                       moe_b1[layer], moe_b2[layer], ln_g[layer, 1], ln_b[layer, 1],
                       dn_alpha, routing).reshape(bsz, s_len, d)
    return x
```

```python
import functools

import jax
import jax.numpy as jnp
from jax import lax
from jax.experimental import pallas as pl
from jax.experimental.pallas import tpu as pltpu

F32 = jnp.float32
BF16 = jnp.bfloat16

HG_HEAD_DIM = 128
NOPE_DIM = 128
ROPE_DIM = 64
V_DIM = 128
VO_DIM = 256
LOG2_E = 1.4426950408889634
KV_RANK = 128
ROPE_THETA = 10000.0
TOP_K = 4
SWIGLU_ALPHA = 1.702
SWIGLU_LIMIT = 7.0
LN_EPS = 1e-5
RMS_EPS = 1e-6

LANES = 128
ROW_SUBLANES = 8
QK_DIM = 256
VMEM_LIMIT = 56 * 1024 * 1024

ROW_TILE = 256
MLA_TILE = 512
HG_GROUP = 128
HG_HEAD_UNROLL = 8
ATT_TQ = 512
ATT_HEADS = 4
MOE_BLOCK = 512
SEG_TILE = 512
SEG_ALIGN = ROW_SUBLANES
SEG_BITS = (SEG_TILE * TOP_K // SEG_ALIGN).bit_length()
SEG_LOW_BITS = 4
PERM_CHUNK = 256


def _cparams(*sem):
    return pltpu.CompilerParams(dimension_semantics=sem, vmem_limit_bytes=VMEM_LIMIT)


def _layer_norm(y, g, b):
    mu = jnp.mean(y, axis=-1, keepdims=True)
    d = y - mu
    var = jnp.mean(d * d, axis=-1, keepdims=True)
    return d * lax.rsqrt(var + LN_EPS) * g + b


def _dot(a, b):
    return jnp.dot(a, b, preferred_element_type=F32)


def _dot_nt(a, b):
    return lax.dot_general(a, b, (((1,), (1,)), ((), ())), preferred_element_type=F32)


def _dot_tn(a, b):
    return lax.dot_general(a, b, (((0,), (0,)), ((), ())), preferred_element_type=F32)


def _sigmoid(x):
    return 0.5 * jnp.tanh(0.5 * x) + 0.5


def _hgrn_kernel(x_ref, w_in_ref, lb_ref, gn_ref, w_o_ref, lg_ref, lbias_ref, out_ref,
                 q_s, f_s, i_s, g_s, mix_s, st_s, *, dn_alpha):
    ts, d = x_ref.shape[1], x_ref.shape[2]
    n_heads = d // HG_HEAD_DIM
    grp = HG_GROUP

    @pl.when(pl.program_id(1) == 0)
    def _():
        st_s[...] = jnp.zeros_like(st_s)

    x = x_ref[0]
    xb = x.astype(BF16)
    lb = lb_ref[...]

    for sec, dst in enumerate((q_s, f_s, i_s, g_s)):
        p = _dot(xb, w_in_ref[:, sec * d:(sec + 1) * d])
        if sec == 0 or sec == 3:
            p = p * _sigmoid(p)
        elif sec == 1:
            p = lb + (1.0 - lb) * _sigmoid(p)
        for h in range(n_heads):
            dst[h] = p[:, h * HG_HEAD_DIM:(h + 1) * HG_HEAD_DIM]

    row = lax.broadcasted_iota(jnp.int32, (grp, grp), 0)
    col = lax.broadcasted_iota(jnp.int32, (grp, grp), 1)
    n_levels = grp.bit_length() - 1
    pair_masks = [(((row >> lvl) ^ (col >> lvl)) == 1) & (row > col) for lvl in range(n_levels)]

    def head_body(h, carry):
        for r0 in range(0, ts, grp):
            q = q_s[h, r0:r0 + grp, :]
            fg = f_s[h, r0:r0 + grp, :]
            iv = i_s[h, r0:r0 + grp, :].astype(BF16)
            k = 1.0 - fg
            ep, es, et = fg, None, fg
            a = jnp.where(row == col, _dot_nt(q.astype(BF16), k.astype(BF16)), 0.0)
            for lvl in range(n_levels):
                half = 1 << lvl
                kl = k if es is None else k * es
                a = jnp.where(pair_masks[lvl], _dot_nt((q * ep).astype(BF16), kl.astype(BF16)), a)
                if half < ROW_SUBLANES:
                    odd = (row & half) != 0
                    et3 = et.reshape(grp // ROW_SUBLANES, ROW_SUBLANES, LANES)
                    other = jnp.where(odd, pltpu.roll(et3, half, 1).reshape(et.shape),
                                      pltpu.roll(et3, ROW_SUBLANES - half, 1).reshape(et.shape))
                    ep = jnp.where(odd, ep * other, ep)
                    es = jnp.where(odd, 1.0, other) if es is None else jnp.where(odd, es, es * other)
                    et = et * other
                else:
                    ep_p, es_p, et_p = [], [], []
                    for b0 in range(0, grp, 2 * half):
                        lo, mid, hi = b0, b0 + half, b0 + 2 * half
                        tot = et[lo:mid] * et[mid:hi]
                        ep_p += [ep[lo:mid], ep[mid:hi] * et[lo:mid]]
                        es_p += [es[lo:mid] * et[mid:hi], es[mid:hi]]
                        et_p += [tot, tot]
                    ep = jnp.concatenate(ep_p, axis=0)
                    es = jnp.concatenate(es_p, axis=0)
                    et = jnp.concatenate(et_p, axis=0)
            st = st_s[h]
            o = _dot(a.astype(BF16), iv) + _dot_nt((q * ep).astype(BF16), st.astype(BF16))
            st_s[h] = st * et[0:1, :] + _dot_tn(iv, (k * es).astype(BF16))
            ms = jnp.mean(o * o, axis=-1, keepdims=True)
            y = o * lax.rsqrt(ms + RMS_EPS) * gn_ref[...] * g_s[h, r0:r0 + grp, :]
            mix_s[h, r0:r0 + grp, :] = y.astype(BF16)
        return carry

    lax.fori_loop(0, n_heads, head_body, 0, unroll=HG_HEAD_UNROLL)

    acc = dn_alpha * x
    for h in range(0, n_heads, 2):
        acc = acc + _dot(jnp.concatenate([mix_s[h], mix_s[h + 1]], axis=-1), w_o_ref[h // 2])
    out_ref[0] = _layer_norm(acc, lg_ref[...], lbias_ref[...])


def _hgrn_layer(x, w_in, lb, gnorm, w_o, ln_g, ln_b, dn_alpha):
    bsz, s_len, d = x.shape
    n_heads = d // HG_HEAD_DIM
    ts = ROW_TILE
    const2 = lambda b, s: (0, 0)
    gate_scratch = pltpu.VMEM((n_heads, ts, HG_HEAD_DIM), F32)
    return pl.pallas_call(
        functools.partial(_hgrn_kernel, dn_alpha=dn_alpha),
        grid=(bsz, s_len // ts),
        in_specs=[
            pl.BlockSpec((1, ts, d), lambda b, s: (b, s, 0)),
            pl.BlockSpec((d, 4 * d), const2),
            pl.BlockSpec((1, d), const2),
            pl.BlockSpec((1, HG_HEAD_DIM), const2),
            pl.BlockSpec((n_heads // 2, 2 * HG_HEAD_DIM, d), lambda b, s: (0, 0, 0)),
            pl.BlockSpec((1, d), const2),
            pl.BlockSpec((1, d), const2),
        ],
        out_specs=pl.BlockSpec((1, ts, d), lambda b, s: (b, s, 0)),
        out_shape=jax.ShapeDtypeStruct((bsz, s_len, d), F32),
        scratch_shapes=[
            gate_scratch, gate_scratch, gate_scratch, gate_scratch,
            pltpu.VMEM((n_heads, ts, HG_HEAD_DIM), BF16),
            pltpu.VMEM((n_heads, HG_HEAD_DIM, HG_HEAD_DIM), F32),
        ],
        compiler_params=_cparams("parallel", "arbitrary"),
        name="hgrn2_layer",
    )(x, w_in.astype(BF16), lb.reshape(1, d), gnorm.reshape(1, HG_HEAD_DIM),
      w_o.astype(BF16).reshape(n_heads // 2, 2 * HG_HEAD_DIM, d), ln_g.reshape(1, d), ln_b.reshape(1, d))


def _rope_lanes(t, cc, ss):
    return t * cc + pltpu.roll(t, LANES // 2, 1) * ss


def _kv_kernel(x_ref, wa_ref, kvn_ref, wb_ref, cc_ref, ss_ref, k_ref, v_ref):
    n_heads = k_ref.shape[1]
    xb = x_ref[0].astype(BF16)
    ckr = _dot(xb, wa_ref[...])
    c = ckr[:, :KV_RANK]
    c = c * lax.rsqrt(jnp.mean(c * c, axis=-1, keepdims=True) + RMS_EPS) * kvn_ref[...]
    kr = _rope_lanes(ckr[:, KV_RANK:], cc_ref[0], ss_ref[0]).astype(BF16)
    cb = c.astype(BF16)
    for h in range(n_heads):
        kv = _dot(cb, wb_ref[:, h * (NOPE_DIM + V_DIM):(h + 1) * (NOPE_DIM + V_DIM)])
        k_ref[0, h, :, :NOPE_DIM] = kv[:, :NOPE_DIM].astype(BF16)
        k_ref[0, h, :, NOPE_DIM:] = kr
        v_ref[0, h, :, :V_DIM] = kv[:, NOPE_DIM:].astype(BF16)
        v_ref[0, h, :, V_DIM:] = jnp.ones((kv.shape[0], VO_DIM - V_DIM), BF16)


def _shared_kv(x, kv_w_a_l, kv_norm, kv_w_b, cc, ss):
    bsz, s_len, d = x.shape
    n_heads = kv_w_b.shape[1] // (NOPE_DIM + V_DIM)
    ts = MLA_TILE
    const2 = lambda b, s: (0, 0)
    return pl.pallas_call(
        _kv_kernel,
        grid=(bsz, s_len // ts),
        in_specs=[
            pl.BlockSpec((1, ts, d), lambda b, s: (b, s, 0)),
            pl.BlockSpec((d, KV_RANK + LANES), const2),
            pl.BlockSpec((1, KV_RANK), const2),
            pl.BlockSpec((KV_RANK, n_heads * (NOPE_DIM + V_DIM)), const2),
            pl.BlockSpec((1, ts, LANES), lambda b, s: (b, s, 0)),
            pl.BlockSpec((1, ts, LANES), lambda b, s: (b, s, 0)),
        ],
        out_specs=[
            pl.BlockSpec((1, n_heads, ts, QK_DIM), lambda b, s: (b, 0, s, 0)),
            pl.BlockSpec((1, n_heads, ts, VO_DIM), lambda b, s: (b, 0, s, 0)),
        ],
        out_shape=[
            jax.ShapeDtypeStruct((bsz, n_heads, s_len, QK_DIM), BF16),
            jax.ShapeDtypeStruct((bsz, n_heads, s_len, VO_DIM), BF16),
        ],
        compiler_params=_cparams("parallel", "parallel"),
        name="mla_shared_kv",
    )(x, kv_w_a_l, kv_norm.reshape(1, KV_RANK), kv_w_b.astype(BF16), cc, ss)


def _q_kernel(x_ref, wdq_ref, qn_ref, wuq_ref, cc_ref, ss_ref, q_ref, *, scale):
    n_heads = q_ref.shape[1]
    xb = x_ref[0].astype(BF16)
    c = _dot(xb, wdq_ref[...])
    c = c * lax.rsqrt(jnp.mean(c * c, axis=-1, keepdims=True) + RMS_EPS) * qn_ref[...]
    cb = c.astype(BF16)
    cc = cc_ref[0] * scale
    ss = ss_ref[0] * scale
    for h in range(n_heads):
        qh = _dot(cb, wuq_ref[:, h * QK_DIM:(h + 1) * QK_DIM])
        q_ref[0, h, :, :NOPE_DIM] = (qh[:, :NOPE_DIM] * scale).astype(BF16)
        q_ref[0, h, :, NOPE_DIM:] = _rope_lanes(qh[:, NOPE_DIM:], cc, ss).astype(BF16)


def _mla_queries(x, w_dq, q_norm, w_uq_l, cc, ss, scale):
    bsz, s_len, d = x.shape
    q_rank = w_dq.shape[1]
    n_heads = w_uq_l.shape[1] // QK_DIM
    ts = MLA_TILE
    const2 = lambda b, s: (0, 0)
    return pl.pallas_call(
        functools.partial(_q_kernel, scale=scale),
        grid=(bsz, s_len // ts),
        in_specs=[
            pl.BlockSpec((1, ts, d), lambda b, s: (b, s, 0)),
            pl.BlockSpec((d, q_rank), const2),
            pl.BlockSpec((1, q_rank), const2),
            pl.BlockSpec((q_rank, n_heads * QK_DIM), const2),
            pl.BlockSpec((1, ts, LANES), lambda b, s: (b, s, 0)),
            pl.BlockSpec((1, ts, LANES), lambda b, s: (b, s, 0)),
        ],
        out_specs=pl.BlockSpec((1, n_heads, ts, QK_DIM), lambda b, s: (b, 0, s, 0)),
        out_shape=jax.ShapeDtypeStruct((bsz, n_heads, s_len, QK_DIM), BF16),
        compiler_params=_cparams("parallel", "parallel"),
        name="mla_queries",
    )(x, w_dq.astype(BF16), q_norm.reshape(1, q_rank), w_uq_l, cc, ss)


def _attn_kernel(q_ref, k_ref, v_ref, o_ref):
    s_len = q_ref.shape[2]
    tq = min(ATT_TQ, s_len)
    row = lax.broadcasted_iota(jnp.int32, (tq, tq), 0)
    col = lax.broadcasted_iota(jnp.int32, (tq, tq), 1)
    for hh in range(q_ref.shape[1]):
        for qi in range(s_len // tq):
            q = q_ref[0, hh, qi * tq:(qi + 1) * tq, :]
            m = jnp.full((tq, 1), -jnp.inf, F32)
            acc = jnp.zeros((tq, VO_DIM), F32)
            for kj in range(qi + 1):
                s = _dot_nt(q, k_ref[0, hh, kj * tq:(kj + 1) * tq, :])
                if kj == qi:
                    s = jnp.where(col <= row, s, -jnp.inf)
                m_new = jnp.maximum(m, jnp.max(s, axis=-1, keepdims=True))
                p = jnp.exp2(s - m_new)
                acc = jnp.exp2(m - m_new) * acc + _dot(p.astype(BF16), v_ref[0, hh, kj * tq:(kj + 1) * tq, :])
                m = m_new
            o_ref[0, qi * tq:(qi + 1) * tq, hh * V_DIM:(hh + 1) * V_DIM] = (
                acc[:, :V_DIM] / acc[:, V_DIM:V_DIM + 1]).astype(o_ref.dtype)


def _mla_attention(q, k, v):
    bsz, n_heads, s_len, _ = q.shape
    return pl.pallas_call(
        _attn_kernel,
        grid=(bsz, n_heads // ATT_HEADS),
        in_specs=[
            pl.BlockSpec((1, ATT_HEADS, s_len, QK_DIM), lambda b, h: (b, h, 0, 0)),
            pl.BlockSpec((1, ATT_HEADS, s_len, QK_DIM), lambda b, h: (b, h, 0, 0)),
            pl.BlockSpec((1, ATT_HEADS, s_len, VO_DIM), lambda b, h: (b, h, 0, 0)),
        ],
        out_specs=pl.BlockSpec((1, s_len, ATT_HEADS * V_DIM), lambda b, h: (b, 0, h)),
        out_shape=jax.ShapeDtypeStruct((bsz, s_len, n_heads * V_DIM), BF16),
        compiler_params=_cparams("parallel", "parallel"),
        name="mla_attention",
    )(q, k, v)


def _split_hi_lo(v):
    hi = lax.bitcast_convert_type(lax.bitcast_convert_type(v, jnp.uint32) & jnp.uint32(0xFFFF0000), F32)
    return hi.astype(BF16), (v - hi).astype(BF16)


def _route_tile(x, whi_ref, wlo_ref, b_ref, pos_ref, gate_ref, meta_ref):
    tr = x.shape[0]

    x_hi, x_lo = _split_hi_lo(x)
    logits = (_dot(x_hi, whi_ref[...]) + (_dot(x_lo, whi_ref[...]) + _dot(x_hi, wlo_ref[...]))
              + b_ref[...])
    lane = lax.broadcasted_iota(jnp.int32, (tr, LANES), 1)
    work = logits
    sel = jnp.zeros((tr, LANES), F32)
    ids, vals = [], []
    for _ in range(TOP_K):
        mx = jnp.max(work, axis=-1, keepdims=True)
        idx = jnp.min(jnp.where(work == mx, lane, LANES), axis=-1, keepdims=True)
        hit = lane == idx
        ids.append(idx)
        vals.append(mx)
        sel = jnp.where(hit, 1.0, sel)
        work = jnp.where(hit, -jnp.inf, work)
    exps = [jnp.exp(v - vals[0]) for v in vals]
    denom = exps[0] + exps[1] + exps[2] + exps[3]

    r = lax.broadcasted_iota(jnp.int32, (tr, tr), 0)
    c = lax.broadcasted_iota(jnp.int32, (tr, tr), 1)
    before = _dot(jnp.where(c < r, 1.0, 0.0).astype(BF16), sel.astype(BF16))
    seg8 = jnp.floor((jnp.sum(sel, axis=0, keepdims=True) + (SEG_ALIGN - 1)) * (1.0 / SEG_ALIGN))
    er = lax.broadcasted_iota(jnp.int32, (LANES, LANES), 0)
    ec = lax.broadcasted_iota(jnp.int32, (LANES, LANES), 1)
    start8 = _dot(jnp.broadcast_to(seg8, (ROW_SUBLANES, LANES)).astype(BF16),
                  jnp.where(er < ec, 1.0, 0.0).astype(BF16))[0:1]
    slot = start8 * SEG_ALIGN + before

    pos_out = jnp.zeros((tr, LANES), F32)
    gate_out = jnp.zeros((tr, LANES), F32)
    for j in range(TOP_K):
        pos = jnp.sum(jnp.where(lane == ids[j], slot, 0.0), axis=-1, keepdims=True)
        pos_out = jnp.where(lane == j, pos, pos_out)
        gate_out = jnp.where(lane == j, exps[j] / denom, gate_out)
    pos_ref[...] = pos_out
    gate_ref[...] = gate_out
    row8 = lax.broadcasted_iota(jnp.int32, (ROW_SUBLANES, LANES), 0)
    meta_ref[...] = jnp.where(row8 == 0, seg8, jnp.where(row8 == 1, start8, 0.0))


def _router_kernel(x_ref, whi_ref, wlo_ref, b_ref, pos_ref, gate_ref, meta_ref):
    _route_tile(x_ref[...], whi_ref, wlo_ref, b_ref, pos_ref, gate_ref, meta_ref)


def _router_operands(router_w, router_b):
    d, n_exp = router_w.shape
    w_hi, w_lo = _split_hi_lo(jnp.zeros((d, LANES), F32).at[:, :n_exp].set(router_w))
    return w_hi, w_lo, jnp.full((1, LANES), -jnp.inf, F32).at[0, :n_exp].set(router_b)


def _router_specs(d, n_tok):
    const = lambda i: (0, 0)
    tile = lambda i: (i, 0)
    in_specs = [pl.BlockSpec((d, LANES), const), pl.BlockSpec((d, LANES), const), pl.BlockSpec((1, LANES), const)]
    out_specs = [pl.BlockSpec((SEG_TILE, LANES), tile), pl.BlockSpec((SEG_TILE, LANES), tile),
                 pl.BlockSpec((ROW_SUBLANES, LANES), tile)]
    out_shape = [
        jax.ShapeDtypeStruct((n_tok, LANES), F32),
        jax.ShapeDtypeStruct((n_tok, LANES), F32),
        jax.ShapeDtypeStruct((n_tok // SEG_TILE * ROW_SUBLANES, LANES), F32),
    ]
    return in_specs, out_specs, out_shape


def _router(x2, router_w, router_b):
    n_tok, d = x2.shape
    in_specs, out_specs, out_shape = _router_specs(d, n_tok)
    return pl.pallas_call(
        _router_kernel,
        grid=(n_tok // SEG_TILE,),
        in_specs=[pl.BlockSpec((SEG_TILE, d), lambda i: (i, 0))] + in_specs,
        out_specs=out_specs,
        out_shape=out_shape,
        compiler_params=_cparams("parallel"),
        name="moe_router",
    )(x2, *_router_operands(router_w, router_b))


def _proj_ln_route_kernel(o_ref, x_ref, w_ref, lg_ref, lb_ref, whi_ref, wlo_ref, rb_ref,
                          out_ref, pos_ref, gate_ref, meta_ref, *, dn_alpha):
    y = _layer_norm(dn_alpha * x_ref[...] + _dot(o_ref[...], w_ref[...]), lg_ref[...], lb_ref[...])
    out_ref[...] = y
    _route_tile(y, whi_ref, wlo_ref, rb_ref, pos_ref, gate_ref, meta_ref)


def _proj_residual_ln_route(o2, x2, w_o, ln_g, ln_b, router_w, router_b, dn_alpha):
    n_tok, d = x2.shape
    kdim = o2.shape[1]
    ts = SEG_TILE
    const = lambda i: (0, 0)
    r_in, r_out, r_shape = _router_specs(d, n_tok)
    res = pl.pallas_call(
        functools.partial(_proj_ln_route_kernel, dn_alpha=dn_alpha),
        grid=(n_tok // ts,),
        in_specs=[
            pl.BlockSpec((ts, kdim), lambda i: (i, 0)),
            pl.BlockSpec((ts, d), lambda i: (i, 0)),
            pl.BlockSpec((kdim, d), const),
            pl.BlockSpec((1, d), const),
            pl.BlockSpec((1, d), const),
        ] + r_in,
        out_specs=[pl.BlockSpec((ts, d), lambda i: (i, 0))] + r_out,
        out_shape=[jax.ShapeDtypeStruct((n_tok, d), F32)] + r_shape,
        compiler_params=_cparams("parallel"),
        name="mla_out_proj_ln_route",
    )(o2, x2, w_o.astype(BF16), ln_g.reshape(1, d), ln_b.reshape(1, d), *_router_operands(router_w, router_b))
    return res[0], tuple(res[1:])


def _segment_copy(local_ref, hbm_ref, sem, to_hbm, lo8, go8, bit):
    rows = SEG_ALIGN << bit
    aligned = lambda v8: v8 * SEG_ALIGN if isinstance(v8, int) else pl.multiple_of(v8 * SEG_ALIGN, SEG_ALIGN)
    loc = local_ref.at[pl.ds(aligned(lo8), rows)]
    hbm = hbm_ref.at[pl.ds(aligned(go8), rows)]
    return pltpu.make_async_copy(loc, hbm, sem) if to_hbm else pltpu.make_async_copy(hbm, loc, sem)


def _segment_starts(tabs, tile, n_exp, local_ref, hbm_ref, sem, to_hbm):
    seg8_ref, start8_ref, gstart8_ref, _ = tabs

    def body(e, c):
        n = seg8_ref[tile * n_exp + e]
        lo = start8_ref[tile * n_exp + e]
        go = gstart8_ref[tile * n_exp + e]

        def bits(lo_bit, hi_bit):
            for bit in range(lo_bit, hi_bit):
                @pl.when(((n >> bit) & 1) == 1)
                def _():
                    off = (n >> (bit + 1)) << (bit + 1)
                    _segment_copy(local_ref, hbm_ref, sem, to_hbm, lo + off, go + off, bit).start()

        bits(0, SEG_LOW_BITS)

        @pl.when(n >= (1 << SEG_LOW_BITS))
        def _():
            bits(SEG_LOW_BITS, SEG_BITS)
        return c

    lax.fori_loop(0, n_exp, body, 0)


def _segment_wait(tabs, tile, local_ref, hbm_ref, sem, to_hbm):
    total = tabs[3][tile]
    for bit in range(SEG_BITS):
        @pl.when(((total >> bit) & 1) == 1)
        def _():
            _segment_copy(local_ref, hbm_ref, sem, to_hbm, 0, 0, bit).wait()


def _dispatch_kernel(seg8_ref, start8_ref, gstart8_ref, tot8_ref, x_ref, pos_ref, xs_hbm, buf, sems, *, n_exp):
    td = x_ref.shape[0]
    seg_rows = buf.shape[1]
    tabs = (seg8_ref, start8_ref, gstart8_ref, tot8_ref)
    step = pl.program_id(0)
    slot = step % 2

    def wait(tile, sl):
        _segment_wait(tabs, tile, buf.at[sl], xs_hbm, sems.at[sl], True)

    @pl.when(step >= 2)
    def _():
        wait(step - 2, slot)

    xb = x_ref[...].astype(BF16)
    pos_t = pos_ref[...].T
    for r0 in range(0, seg_rows, PERM_CHUNK):
        rr = (lax.broadcasted_iota(jnp.int32, (PERM_CHUNK, td), 0) + r0).astype(F32)
        hit = pos_t[0:1] == rr
        for j in range(1, TOP_K):
            hit = hit | (pos_t[j:j + 1] == rr)
        buf[slot, r0:r0 + PERM_CHUNK, :] = _dot(jnp.where(hit, 1.0, 0.0).astype(BF16), xb)

    _segment_starts(tabs, step, n_exp, buf.at[slot], xs_hbm, sems.at[slot], True)

    @pl.when(step == pl.num_programs(0) - 1)
    def _():
        wait(step, slot)

        @pl.when(step >= 1)
        def _():
            wait(step - 1, 1 - slot)


def _seg_rows(td, n_exp):
    return -(-(td * TOP_K + n_exp * (SEG_ALIGN - 1)) // PERM_CHUNK) * PERM_CHUNK


def _dispatch(x2, pos, tabs, n_rows, n_exp):
    n_tok, d = x2.shape
    td = SEG_TILE
    grid_spec = pltpu.PrefetchScalarGridSpec(
        num_scalar_prefetch=len(tabs),
        grid=(n_tok // td,),
        in_specs=[
            pl.BlockSpec((td, d), lambda i, *_: (i, 0)),
            pl.BlockSpec((td, LANES), lambda i, *_: (i, 0)),
        ],
        out_specs=pl.BlockSpec(memory_space=pl.ANY),
        scratch_shapes=[pltpu.VMEM((2, _seg_rows(td, n_exp), d), F32), pltpu.SemaphoreType.DMA((2,))],
    )
    return pl.pallas_call(
        functools.partial(_dispatch_kernel, n_exp=n_exp),
        grid_spec=grid_spec,
        out_shape=jax.ShapeDtypeStruct((n_rows, d), F32),
        compiler_params=_cparams("arbitrary"),
        name="moe_dispatch",
    )(*tabs, x2, pos)


def _expert_kernel(be_ref, nu_ref, nxt_ref, xs_ref, w1_hbm, b1_ref, w2_hbm, b2_ref, perm_ref, ys_ref,
                   w1f_s, w2f_s, w1b_s, w2b_s, sems, *, layer):
    step = pl.program_id(0)
    d, f = w2b_s.shape[1], w2b_s.shape[0]
    active = step < nu_ref[0]
    new_expert = jnp.logical_or(step == 0, be_ref[step] != be_ref[jnp.maximum(step - 1, 0)])

    def fetch(e):
        return (pltpu.make_async_copy(w1_hbm.at[layer, e], w1f_s, sems.at[0]),
                pltpu.make_async_copy(w2_hbm.at[layer, e], w2f_s, sems.at[1]))

    @pl.when(jnp.logical_and(active, step == 0))
    def _():
        for cp in fetch(be_ref[0]):
            cp.start()

    @pl.when(jnp.logical_and(active, new_expert))
    def _():
        for cp in fetch(be_ref[step]):
            cp.wait()
        width = perm_ref.shape[0]
        for r0 in range(0, d, ROW_TILE):
            for c in range(2 * f // width):
                t = _dot(w1f_s[r0:r0 + ROW_TILE, c * width:(c + 1) * width].astype(BF16), perm_ref[...])
                w1b_s[r0:r0 + ROW_TILE, c * (width // 2):(c + 1) * (width // 2)] = t[:, :width // 2].astype(BF16)
                w1b_s[r0:r0 + ROW_TILE, f + c * (width // 2):f + (c + 1) * (width // 2)] = (
                    t[:, width // 2:].astype(BF16))
        w2b_s[...] = w2f_s[...].astype(BF16)

        @pl.when(nxt_ref[step] >= 0)
        def _():
            for cp in fetch(nxt_ref[step]):
                cp.start()

    @pl.when(active)
    def _():
        h = _dot(xs_ref[...].astype(BF16), w1b_s[...]) + b1_ref[0]
        glu = jnp.minimum(h[:, :f], SWIGLU_LIMIT)
        lin = jnp.clip(h[:, f:], -SWIGLU_LIMIT, SWIGLU_LIMIT)
        a = glu * _sigmoid(SWIGLU_ALPHA * glu) * (lin + 1.0)
        ys_ref[...] = _dot(a.astype(BF16), w2b_s[...]) + b2_ref[0]


def _experts(xs, block_e, n_used, next_e, w1_all, w2_all, layer, b1, b2):
    _, n_exp, f, d = w2_all.shape
    blk = MOE_BLOCK
    n_blocks = xs.shape[0] // blk
    width = 2 * LANES
    r = lax.broadcasted_iota(jnp.int32, (width, width), 0)
    c = lax.broadcasted_iota(jnp.int32, (width, width), 1)
    perm = (c == (r % 2) * (width // 2) + r // 2).astype(BF16)
    row_map = lambda i, be, nu, nx: (jnp.minimum(i, nu[0] - 1), 0)
    exp_map = lambda i, be, nu, nx: (be[i], 0, 0)
    grid_spec = pltpu.PrefetchScalarGridSpec(
        num_scalar_prefetch=3,
        grid=(n_blocks,),
        in_specs=[
            pl.BlockSpec((blk, d), row_map),
            pl.BlockSpec(memory_space=pl.ANY),
            pl.BlockSpec((1, 1, 2 * f), exp_map),
            pl.BlockSpec(memory_space=pl.ANY),
            pl.BlockSpec((1, 1, d), exp_map),
            pl.BlockSpec((width, width), lambda i, be, nu, nx: (0, 0)),
        ],
        out_specs=pl.BlockSpec((blk, d), row_map),
        scratch_shapes=[pltpu.VMEM((d, 2 * f), F32), pltpu.VMEM((f, d), F32),
                        pltpu.VMEM((d, 2 * f), BF16), pltpu.VMEM((f, d), BF16),
                        pltpu.SemaphoreType.DMA((2,))],
    )
    return pl.pallas_call(
        functools.partial(_expert_kernel, layer=layer),
        grid_spec=grid_spec,
        out_shape=jax.ShapeDtypeStruct(xs.shape, F32),
        compiler_params=_cparams("arbitrary"),
        name="moe_experts",
    )(block_e, n_used, next_e, xs, w1_all, b1.reshape(n_exp, 1, 2 * f), w2_all, b2.reshape(n_exp, 1, d), perm)


def _combine_kernel(seg8_ref, start8_ref, gstart8_ref, tot8_ref, ys_hbm, x_ref, pos_ref, gate_ref, lg_ref,
                    lb_ref, out_ref, buf, sems, *, n_exp, dn_alpha):
    td = x_ref.shape[0]
    seg_rows = buf.shape[1]
    tabs = (seg8_ref, start8_ref, gstart8_ref, tot8_ref)
    step = pl.program_id(0)
    slot = step % 2

    def start(tile, sl):
        _segment_starts(tabs, tile, n_exp, buf.at[sl], ys_hbm, sems.at[sl], False)

    @pl.when(step == 0)
    def _():
        buf[...] = jnp.zeros_like(buf)
        start(step, slot)

    @pl.when(step + 1 < pl.num_programs(0))
    def _():
        start(step + 1, 1 - slot)

    _segment_wait(tabs, step, buf.at[slot], ys_hbm, sems.at[slot], False)

    pos = pos_ref[...]
    gates = gate_ref[...]
    acc = dn_alpha * x_ref[...]
    for r0 in range(0, seg_rows, PERM_CHUNK):
        cc = (lax.broadcasted_iota(jnp.int32, (td, PERM_CHUNK), 1) + r0).astype(F32)
        g = jnp.zeros((td, PERM_CHUNK), F32)
        for j in range(TOP_K):
            g = jnp.where(pos[:, j:j + 1] == cc, gates[:, j:j + 1], g)
        acc = acc + _dot(g.astype(BF16), buf[slot, r0:r0 + PERM_CHUNK, :].astype(BF16))
    out_ref[...] = _layer_norm(acc, lg_ref[...], lb_ref[...])


def _combine_ln(ys, pos, gates, tabs, x2, ln_g, ln_b, n_exp, dn_alpha):
    n_tok, d = x2.shape
    td = SEG_TILE
    const = lambda i, *_: (0, 0)
    tile = lambda i, *_: (i, 0)
    grid_spec = pltpu.PrefetchScalarGridSpec(
        num_scalar_prefetch=len(tabs),
        grid=(n_tok // td,),
        in_specs=[
            pl.BlockSpec(memory_space=pl.ANY),
            pl.BlockSpec((td, d), tile),
            pl.BlockSpec((td, LANES), tile),
            pl.BlockSpec((td, LANES), tile),
            pl.BlockSpec((1, d), const),
            pl.BlockSpec((1, d), const),
        ],
        out_specs=pl.BlockSpec((td, d), tile),
        scratch_shapes=[pltpu.VMEM((2, _seg_rows(td, n_exp), d), F32), pltpu.SemaphoreType.DMA((2,))],
    )
    return pl.pallas_call(
        functools.partial(_combine_kernel, n_exp=n_exp, dn_alpha=dn_alpha),
        grid_spec=grid_spec,
        out_shape=jax.ShapeDtypeStruct((n_tok, d), F32),
        compiler_params=_cparams("arbitrary"),
        name="moe_combine_ln",
    )(*tabs, ys, x2, pos, gates, ln_g.reshape(1, d), ln_b.reshape(1, d))


def _moe_layer(x2, router_w, router_b, w1_all, w2_all, layer, b1, b2, ln_g, ln_b, dn_alpha, routing=None):
    n_tok, d = x2.shape
    n_exp = w2_all.shape[1]
    n_tiles = n_tok // SEG_TILE
    blk8 = MOE_BLOCK // SEG_ALIGN
    n_blocks = -(-(n_tok * TOP_K + n_tiles * n_exp * (SEG_ALIGN - 1)) // MOE_BLOCK) + n_exp
    n_rows = n_blocks * MOE_BLOCK

    pos, gates, meta = _router(x2, router_w, router_b) if routing is None else routing

    meta = meta.reshape(n_tiles, ROW_SUBLANES, LANES)
    seg8 = meta[:, 0, :n_exp].astype(jnp.int32)
    start8 = meta[:, 1, :n_exp].astype(jnp.int32)
    padded8 = (jnp.sum(seg8, axis=0) + blk8 - 1) // blk8 * blk8
    pends8 = jnp.cumsum(padded8)
    gstart8 = (pends8 - padded8)[None, :] + jnp.cumsum(seg8, axis=0) - seg8
    block_start8 = jnp.arange(n_blocks, dtype=jnp.int32) * blk8
    block_e = jnp.minimum(jnp.sum((pends8[None, :] <= block_start8[:, None]).astype(jnp.int32), axis=1),
                          n_exp - 1).astype(jnp.int32)
    n_used = (pends8[-1:] // blk8).astype(jnp.int32)
    ids = jnp.arange(n_exp, dtype=jnp.int32)
    later = jnp.where((ids[None, :] > ids[:, None]) & (padded8[None, :] > 0), ids[None, :], n_exp)
    next_present = jnp.min(later, axis=1)
    next_e = jnp.where(next_present < n_exp, next_present, -1)[block_e].astype(jnp.int32)
    tabs = (seg8.reshape(-1), start8.reshape(-1), gstart8.reshape(-1).astype(jnp.int32),
            jnp.sum(seg8, axis=1).astype(jnp.int32))

    b1_l = jnp.concatenate([b1[:, 0::2], b1[:, 1::2]], axis=-1)

    xs = _dispatch(x2, pos, tabs, n_rows, n_exp)
    ys = _experts(xs, block_e, n_used, next_e, w1_all, w2_all, layer, b1_l, b2)
    return _combine_ln(ys, pos, gates, tabs, x2, ln_g, ln_b, n_exp, dn_alpha)


def _rope_lane_tables(positions):
    half = ROPE_DIM // 2
    inv_freq = ROPE_THETA ** (-jnp.arange(0, ROPE_DIM, 2, dtype=F32) / ROPE_DIM)
    ang = positions.astype(F32)[..., None] * inv_freq
    cos, sin = jnp.cos(ang), jnp.sin(ang)
    z = jnp.zeros_like(cos)
    assert 4 * half == LANES
    return (jnp.concatenate([cos, z, cos, z], axis=-1),
            jnp.concatenate([-sin, z, sin, z], axis=-1))


def _rope_lane_columns(w_rope):
    half = ROPE_DIM // 2
    z = jnp.zeros(w_rope.shape[:-1] + (half,), w_rope.dtype)
    return jnp.concatenate([w_rope[..., :half], z, w_rope[..., half:], z], axis=-1)


def kernel(x, positions, ln_g, ln_b, hg_w_in, hg_lb, hg_gnorm, hg_w_o, mla_w_dq, mla_q_norm,
           mla_w_uq, mla_w_o, kv_w_a, kv_norm, kv_w_b, router_w, router_b, moe_w1, moe_b1,
           moe_w2, moe_b2):
    bsz, s_len, d = x.shape
    depth = ln_g.shape[0]
    n_a = hg_w_in.shape[0]
    dn_alpha = (2.0 * depth) ** 0.25
    scale = (NOPE_DIM + ROPE_DIM) ** -0.5 * LOG2_E

    lb_soft = jax.nn.softmax(hg_lb.astype(F32), axis=0)
    lower_bounds = jnp.cumsum(lb_soft, axis=0) - lb_soft[0]

    q_rank = mla_w_uq.shape[1]
    mla_heads = mla_w_uq.shape[2] // (NOPE_DIM + ROPE_DIM)
    cc = ss = k_full = v_full = None

    for layer in range(depth):
        routing = None
        if layer < n_a:
            x = _hgrn_layer(x, hg_w_in[layer], lower_bounds[layer], hg_gnorm[layer], hg_w_o[layer],
                            ln_g[layer, 0], ln_b[layer, 0], dn_alpha)
        else:
            j = layer - n_a
            if layer == n_a:
                cc, ss = _rope_lane_tables(positions)
                kv_w_a_l = jnp.concatenate(
                    [kv_w_a[:, :KV_RANK], _rope_lane_columns(kv_w_a[:, KV_RANK:])], axis=-1).astype(BF16)
                k_full, v_full = _shared_kv(x, kv_w_a_l, kv_norm, kv_w_b, cc, ss)
            w_uq = mla_w_uq[j].reshape(q_rank, mla_heads, NOPE_DIM + ROPE_DIM)
            w_uq_l = jnp.concatenate(
                [w_uq[..., :NOPE_DIM], _rope_lane_columns(w_uq[..., NOPE_DIM:])],
                axis=-1).reshape(q_rank, mla_heads * QK_DIM).astype(BF16)
            q_full = _mla_queries(x, mla_w_dq[j], mla_q_norm[j], w_uq_l, cc, ss, scale)
            o = _mla_attention(q_full, k_full, v_full)
            x, routing = _proj_residual_ln_route(
                o.reshape(bsz * s_len, -1), x.reshape(bsz * s_len, d), mla_w_o[j], ln_g[layer, 0],
                ln_b[layer, 0], router_w[layer], router_b[layer], dn_alpha)
        x = _moe_layer(x.reshape(bsz * s_len, d), router_w[layer], router_b[layer], moe_w1, moe_w2, layer,
                       moe_b1[layer], moe_b2[layer], ln_g[layer, 1], ln_b[layer, 1],
                       dn_alpha, routing).reshape(bsz, s_len, d)
    return x
```

```python
import functools

import jax
import jax.numpy as jnp
from jax import lax
from jax.experimental import pallas as pl
from jax.experimental.pallas import tpu as pltpu

F32 = jnp.float32
BF16 = jnp.bfloat16

HG_HEAD_DIM = 128
NOPE_DIM = 128
ROPE_DIM = 64
V_DIM = 128
VO_DIM = 256
LOG2_E = 1.4426950408889634
KV_RANK = 128
ROPE_THETA = 10000.0
TOP_K = 4
SWIGLU_ALPHA = 1.702
SWIGLU_LIMIT = 7.0
LN_EPS = 1e-5
RMS_EPS = 1e-6

LANES = 128
ROW_SUBLANES = 8
QK_DIM = 256
VMEM_LIMIT = 56 * 1024 * 1024

ROW_TILE = 256
MLA_TILE = 512
HG_GROUP = 128
HG_HEAD_UNROLL = 8
ATT_TQ = 512
ATT_HEADS = 4
MOE_BLOCK = 512
SEG_TILE = 512
ROUTE_TILES = 2
SEG_ALIGN = ROW_SUBLANES
SEG_BITS = (SEG_TILE * TOP_K // SEG_ALIGN).bit_length()
SEG_LOW_BITS = 4
PERM_CHUNK = 256


def _cparams(*sem):
    return pltpu.CompilerParams(dimension_semantics=sem, vmem_limit_bytes=VMEM_LIMIT)


def _layer_norm(y, g, b):
    mu = jnp.mean(y, axis=-1, keepdims=True)
    d = y - mu
    var = jnp.mean(d * d, axis=-1, keepdims=True)
    return d * lax.rsqrt(var + LN_EPS) * g + b


def _dot(a, b):
    return jnp.dot(a, b, preferred_element_type=F32)


def _dot_nt(a, b):
    return lax.dot_general(a, b, (((1,), (1,)), ((), ())), preferred_element_type=F32)


def _dot_tn(a, b):
    return lax.dot_general(a, b, (((0,), (0,)), ((), ())), preferred_element_type=F32)


def _sigmoid(x):
    return 0.5 * jnp.tanh(0.5 * x) + 0.5


def _hgrn_kernel(x_ref, w_in_ref, lb_ref, gn_ref, w_o_ref, lg_ref, lbias_ref, out_ref,
                 q_s, f_s, i_s, g_s, mix_s, st_s, *, dn_alpha):
    ts, d = x_ref.shape[1], x_ref.shape[2]
    n_heads = d // HG_HEAD_DIM
    grp = HG_GROUP

    @pl.when(pl.program_id(1) == 0)
    def _():
        st_s[...] = jnp.zeros_like(st_s)

    x = x_ref[0]
    xb = x.astype(BF16)
    lb = lb_ref[...]

    for sec, dst in enumerate((q_s, f_s, i_s, g_s)):
        p = _dot(xb, w_in_ref[:, sec * d:(sec + 1) * d])
        if sec == 0 or sec == 3:
            p = p * _sigmoid(p)
        elif sec == 1:
            p = lb + (1.0 - lb) * _sigmoid(p)
        for h in range(n_heads):
            dst[h] = p[:, h * HG_HEAD_DIM:(h + 1) * HG_HEAD_DIM]

    row = lax.broadcasted_iota(jnp.int32, (grp, grp), 0)
    col = lax.broadcasted_iota(jnp.int32, (grp, grp), 1)
    n_levels = grp.bit_length() - 1
    pair_masks = [(((row >> lvl) ^ (col >> lvl)) == 1) & (row > col) for lvl in range(n_levels)]

    def head_body(h, carry):
        for r0 in range(0, ts, grp):
            q = q_s[h, r0:r0 + grp, :]
            fg = f_s[h, r0:r0 + grp, :]
            iv = i_s[h, r0:r0 + grp, :].astype(BF16)
            k = 1.0 - fg
            ep, es, et = fg, None, fg
            a = jnp.where(row == col, _dot_nt(q.astype(BF16), k.astype(BF16)), 0.0)
            for lvl in range(n_levels):
                half = 1 << lvl
                kl = k if es is None else k * es
                a = jnp.where(pair_masks[lvl], _dot_nt((q * ep).astype(BF16), kl.astype(BF16)), a)
                if half < ROW_SUBLANES:
                    odd = (row & half) != 0
                    et3 = et.reshape(grp // ROW_SUBLANES, ROW_SUBLANES, LANES)
                    other = jnp.where(odd, pltpu.roll(et3, half, 1).reshape(et.shape),
                                      pltpu.roll(et3, ROW_SUBLANES - half, 1).reshape(et.shape))
                    ep = jnp.where(odd, ep * other, ep)
                    es = jnp.where(odd, 1.0, other) if es is None else jnp.where(odd, es, es * other)
                    et = et * other
                else:
                    ep_p, es_p, et_p = [], [], []
                    for b0 in range(0, grp, 2 * half):
                        lo, mid, hi = b0, b0 + half, b0 + 2 * half
                        tot = et[lo:mid] * et[mid:hi]
                        ep_p += [ep[lo:mid], ep[mid:hi] * et[lo:mid]]
                        es_p += [es[lo:mid] * et[mid:hi], es[mid:hi]]
                        et_p += [tot, tot]
                    ep = jnp.concatenate(ep_p, axis=0)
                    es = jnp.concatenate(es_p, axis=0)
                    et = jnp.concatenate(et_p, axis=0)
            st = st_s[h]
            o = _dot(a.astype(BF16), iv) + _dot_nt((q * ep).astype(BF16), st.astype(BF16))
            st_s[h] = st * et[0:1, :] + _dot_tn(iv, (k * es).astype(BF16))
            ms = jnp.mean(o * o, axis=-1, keepdims=True)
            y = o * lax.rsqrt(ms + RMS_EPS) * gn_ref[...] * g_s[h, r0:r0 + grp, :]
            mix_s[h, r0:r0 + grp, :] = y.astype(BF16)
        return carry

    lax.fori_loop(0, n_heads, head_body, 0, unroll=HG_HEAD_UNROLL)

    acc = dn_alpha * x
    for h in range(0, n_heads, 2):
        acc = acc + _dot(jnp.concatenate([mix_s[h], mix_s[h + 1]], axis=-1), w_o_ref[h // 2])
    out_ref[0] = _layer_norm(acc, lg_ref[...], lbias_ref[...])


def _hgrn_layer(x, w_in, lb, gnorm, w_o, ln_g, ln_b, dn_alpha):
    bsz, s_len, d = x.shape
    n_heads = d // HG_HEAD_DIM
    ts = ROW_TILE
    const2 = lambda b, s: (0, 0)
    gate_scratch = pltpu.VMEM((n_heads, ts, HG_HEAD_DIM), F32)
    return pl.pallas_call(
        functools.partial(_hgrn_kernel, dn_alpha=dn_alpha),
        grid=(bsz, s_len // ts),
        in_specs=[
            pl.BlockSpec((1, ts, d), lambda b, s: (b, s, 0)),
            pl.BlockSpec((d, 4 * d), const2),
            pl.BlockSpec((1, d), const2),
            pl.BlockSpec((1, HG_HEAD_DIM), const2),
            pl.BlockSpec((n_heads // 2, 2 * HG_HEAD_DIM, d), lambda b, s: (0, 0, 0)),
            pl.BlockSpec((1, d), const2),
            pl.BlockSpec((1, d), const2),
        ],
        out_specs=pl.BlockSpec((1, ts, d), lambda b, s: (b, s, 0)),
        out_shape=jax.ShapeDtypeStruct((bsz, s_len, d), F32),
        scratch_shapes=[
            gate_scratch, gate_scratch, gate_scratch, gate_scratch,
            pltpu.VMEM((n_heads, ts, HG_HEAD_DIM), BF16),
            pltpu.VMEM((n_heads, HG_HEAD_DIM, HG_HEAD_DIM), F32),
        ],
        compiler_params=_cparams("parallel", "arbitrary"),
        name="hgrn2_layer",
    )(x, w_in.astype(BF16), lb.reshape(1, d), gnorm.reshape(1, HG_HEAD_DIM),
      w_o.astype(BF16).reshape(n_heads // 2, 2 * HG_HEAD_DIM, d), ln_g.reshape(1, d), ln_b.reshape(1, d))


def _rope_lanes(t, cc, ss):
    return t * cc + pltpu.roll(t, LANES // 2, 1) * ss


def _kv_kernel(x_ref, wa_ref, kvn_ref, wb_ref, cc_ref, ss_ref, k_ref, v_ref):
    n_heads = k_ref.shape[1]
    xb = x_ref[0].astype(BF16)
    ckr = _dot(xb, wa_ref[...])
    c = ckr[:, :KV_RANK]
    c = c * lax.rsqrt(jnp.mean(c * c, axis=-1, keepdims=True) + RMS_EPS) * kvn_ref[...]
    kr = _rope_lanes(ckr[:, KV_RANK:], cc_ref[0], ss_ref[0]).astype(BF16)
    cb = c.astype(BF16)
    for h in range(n_heads):
        kv = _dot(cb, wb_ref[:, h * (NOPE_DIM + V_DIM):(h + 1) * (NOPE_DIM + V_DIM)])
        k_ref[0, h, :, :NOPE_DIM] = kv[:, :NOPE_DIM].astype(BF16)
        k_ref[0, h, :, NOPE_DIM:] = kr
        v_ref[0, h, :, :V_DIM] = kv[:, NOPE_DIM:].astype(BF16)
        v_ref[0, h, :, V_DIM:] = jnp.ones((kv.shape[0], VO_DIM - V_DIM), BF16)


def _shared_kv(x, kv_w_a_l, kv_norm, kv_w_b, cc, ss):
    bsz, s_len, d = x.shape
    n_heads = kv_w_b.shape[1] // (NOPE_DIM + V_DIM)
    ts = MLA_TILE
    const2 = lambda b, s: (0, 0)
    return pl.pallas_call(
        _kv_kernel,
        grid=(bsz, s_len // ts),
        in_specs=[
            pl.BlockSpec((1, ts, d), lambda b, s: (b, s, 0)),
            pl.BlockSpec((d, KV_RANK + LANES), const2),
            pl.BlockSpec((1, KV_RANK), const2),
            pl.BlockSpec((KV_RANK, n_heads * (NOPE_DIM + V_DIM)), const2),
            pl.BlockSpec((1, ts, LANES), lambda b, s: (b, s, 0)),
            pl.BlockSpec((1, ts, LANES), lambda b, s: (b, s, 0)),
        ],
        out_specs=[
            pl.BlockSpec((1, n_heads, ts, QK_DIM), lambda b, s: (b, 0, s, 0)),
            pl.BlockSpec((1, n_heads, ts, VO_DIM), lambda b, s: (b, 0, s, 0)),
        ],
        out_shape=[
            jax.ShapeDtypeStruct((bsz, n_heads, s_len, QK_DIM), BF16),
            jax.ShapeDtypeStruct((bsz, n_heads, s_len, VO_DIM), BF16),
        ],
        compiler_params=_cparams("parallel", "parallel"),
        name="mla_shared_kv",
    )(x, kv_w_a_l, kv_norm.reshape(1, KV_RANK), kv_w_b.astype(BF16), cc, ss)


def _q_kernel(x_ref, wdq_ref, qn_ref, wuq_ref, cc_ref, ss_ref, q_ref, *, scale):
    n_heads = q_ref.shape[1]
    xb = x_ref[0].astype(BF16)
    c = _dot(xb, wdq_ref[...])
    c = c * lax.rsqrt(jnp.mean(c * c, axis=-1, keepdims=True) + RMS_EPS) * qn_ref[...]
    cb = c.astype(BF16)
    cc = cc_ref[0] * scale
    ss = ss_ref[0] * scale
    for h in range(n_heads):
        qh = _dot(cb, wuq_ref[:, h * QK_DIM:(h + 1) * QK_DIM])
        q_ref[0, h, :, :NOPE_DIM] = (qh[:, :NOPE_DIM] * scale).astype(BF16)
        q_ref[0, h, :, NOPE_DIM:] = _rope_lanes(qh[:, NOPE_DIM:], cc, ss).astype(BF16)


def _mla_queries(x, w_dq, q_norm, w_uq_l, cc, ss, scale):
    bsz, s_len, d = x.shape
    q_rank = w_dq.shape[1]
    n_heads = w_uq_l.shape[1] // QK_DIM
    ts = MLA_TILE
    const2 = lambda b, s: (0, 0)
    return pl.pallas_call(
        functools.partial(_q_kernel, scale=scale),
        grid=(bsz, s_len // ts),
        in_specs=[
            pl.BlockSpec((1, ts, d), lambda b, s: (b, s, 0)),
            pl.BlockSpec((d, q_rank), const2),
            pl.BlockSpec((1, q_rank), const2),
            pl.BlockSpec((q_rank, n_heads * QK_DIM), const2),
            pl.BlockSpec((1, ts, LANES), lambda b, s: (b, s, 0)),
            pl.BlockSpec((1, ts, LANES), lambda b, s: (b, s, 0)),
        ],
        out_specs=pl.BlockSpec((1, n_heads, ts, QK_DIM), lambda b, s: (b, 0, s, 0)),
        out_shape=jax.ShapeDtypeStruct((bsz, n_heads, s_len, QK_DIM), BF16),
        compiler_params=_cparams("parallel", "parallel"),
        name="mla_queries",
    )(x, w_dq.astype(BF16), q_norm.reshape(1, q_rank), w_uq_l, cc, ss)


def _attn_kernel(q_ref, k_ref, v_ref, o_ref):
    s_len = q_ref.shape[2]
    tq = min(ATT_TQ, s_len)
    row = lax.broadcasted_iota(jnp.int32, (tq, tq), 0)
    col = lax.broadcasted_iota(jnp.int32, (tq, tq), 1)
    for hh in range(q_ref.shape[1]):
        for qi in range(s_len // tq):
            q = q_ref[0, hh, qi * tq:(qi + 1) * tq, :]
            m = jnp.full((tq, 1), -jnp.inf, F32)
            acc = jnp.zeros((tq, VO_DIM), F32)
            for kj in range(qi + 1):
                s = _dot_nt(q, k_ref[0, hh, kj * tq:(kj + 1) * tq, :])
                if kj == qi:
                    s = jnp.where(col <= row, s, -jnp.inf)
                m_new = jnp.maximum(m, jnp.max(s, axis=-1, keepdims=True))
                p = jnp.exp2(s - m_new)
                acc = jnp.exp2(m - m_new) * acc + _dot(p.astype(BF16), v_ref[0, hh, kj * tq:(kj + 1) * tq, :])
                m = m_new
            o_ref[0, qi * tq:(qi + 1) * tq, hh * V_DIM:(hh + 1) * V_DIM] = (
                acc[:, :V_DIM] / acc[:, V_DIM:V_DIM + 1]).astype(o_ref.dtype)


def _mla_attention(q, k, v):
    bsz, n_heads, s_len, _ = q.shape
    return pl.pallas_call(
        _attn_kernel,
        grid=(bsz, n_heads // ATT_HEADS),
        in_specs=[
            pl.BlockSpec((1, ATT_HEADS, s_len, QK_DIM), lambda b, h: (b, h, 0, 0)),
            pl.BlockSpec((1, ATT_HEADS, s_len, QK_DIM), lambda b, h: (b, h, 0, 0)),
            pl.BlockSpec((1, ATT_HEADS, s_len, VO_DIM), lambda b, h: (b, h, 0, 0)),
        ],
        out_specs=pl.BlockSpec((1, s_len, ATT_HEADS * V_DIM), lambda b, h: (b, 0, h)),
        out_shape=jax.ShapeDtypeStruct((bsz, s_len, n_heads * V_DIM), BF16),
        compiler_params=_cparams("parallel", "parallel"),
        name="mla_attention",
    )(q, k, v)


def _split_hi_lo(v):
    hi = lax.bitcast_convert_type(lax.bitcast_convert_type(v, jnp.uint32) & jnp.uint32(0xFFFF0000), F32)
    return hi.astype(BF16), (v - hi).astype(BF16)


def _route_tile(x, w_ref, b_ref, pos_ref, gate_ref, meta_ref):
    tr = x.shape[0]

    x_hi, x_lo = _split_hi_lo(x)
    hi = _dot(x_hi, w_ref[...])
    logits = hi[:, :LANES] + (_dot(x_lo, w_ref[:, :LANES]) + hi[:, LANES:]) + b_ref[...]
    lane = lax.broadcasted_iota(jnp.int32, (tr, LANES), 1)
    work = logits
    sel = jnp.zeros((tr, LANES), F32)
    ids, vals = [], []
    for _ in range(TOP_K):
        mx = jnp.max(work, axis=-1, keepdims=True)
        idx = jnp.min(jnp.where(work == mx, lane, LANES), axis=-1, keepdims=True)
        hit = lane == idx
        ids.append(idx)
        vals.append(mx)
        sel = jnp.where(hit, 1.0, sel)
        work = jnp.where(hit, -jnp.inf, work)
    exps = [jnp.exp(v - vals[0]) for v in vals]
    denom = exps[0] + exps[1] + exps[2] + exps[3]

    r = lax.broadcasted_iota(jnp.int32, (tr, tr), 0)
    c = lax.broadcasted_iota(jnp.int32, (tr, tr), 1)
    before = _dot(jnp.where(c < r, 1.0, 0.0).astype(BF16), sel.astype(BF16))
    seg8 = jnp.floor((jnp.sum(sel, axis=0, keepdims=True) + (SEG_ALIGN - 1)) * (1.0 / SEG_ALIGN))
    er = lax.broadcasted_iota(jnp.int32, (LANES, LANES), 0)
    ec = lax.broadcasted_iota(jnp.int32, (LANES, LANES), 1)
    start8 = _dot(jnp.broadcast_to(seg8, (ROW_SUBLANES, LANES)).astype(BF16),
                  jnp.where(er < ec, 1.0, 0.0).astype(BF16))[0:1]
    slot = start8 * SEG_ALIGN + before

    pos_out = jnp.zeros((tr, LANES), F32)
    gate_out = jnp.zeros((tr, LANES), F32)
    for j in range(TOP_K):
        pos = jnp.sum(jnp.where(lane == ids[j], slot, 0.0), axis=-1, keepdims=True)
        pos_out = jnp.where(lane == j, pos, pos_out)
        gate_out = jnp.where(lane == j, exps[j] / denom, gate_out)
    pos_ref[...] = pos_out
    gate_ref[...] = gate_out
    row8 = lax.broadcasted_iota(jnp.int32, (ROW_SUBLANES, LANES), 0)
    meta_ref[...] = jnp.where(row8 == 0, seg8, jnp.where(row8 == 1, start8, 0.0))


def _route_tiles(x_of, w_ref, b_ref, pos_ref, gate_ref, meta_ref):
    for u in range(ROUTE_TILES):
        rows = pl.ds(u * SEG_TILE, SEG_TILE)
        _route_tile(x_of(rows), w_ref, b_ref, pos_ref.at[rows], gate_ref.at[rows],
                    meta_ref.at[pl.ds(u * ROW_SUBLANES, ROW_SUBLANES)])


def _router_kernel(x_ref, w_ref, b_ref, pos_ref, gate_ref, meta_ref):
    _route_tiles(lambda rows: x_ref[rows, :], w_ref, b_ref, pos_ref, gate_ref, meta_ref)


def _router_operands(router_w, router_b):
    d, n_exp = router_w.shape
    w_hi, w_lo = _split_hi_lo(jnp.zeros((d, LANES), F32).at[:, :n_exp].set(router_w))
    return (jnp.concatenate([w_hi, w_lo], axis=1),
            jnp.full((1, LANES), -jnp.inf, F32).at[0, :n_exp].set(router_b))


def _router_specs(d, n_tok):
    const = lambda i: (0, 0)
    tile = lambda i: (i, 0)
    rows = ROUTE_TILES * SEG_TILE
    in_specs = [pl.BlockSpec((d, 2 * LANES), const), pl.BlockSpec((1, LANES), const)]
    out_specs = [pl.BlockSpec((rows, LANES), tile), pl.BlockSpec((rows, LANES), tile),
                 pl.BlockSpec((ROUTE_TILES * ROW_SUBLANES, LANES), tile)]
    out_shape = [
        jax.ShapeDtypeStruct((n_tok, LANES), F32),
        jax.ShapeDtypeStruct((n_tok, LANES), F32),
        jax.ShapeDtypeStruct((n_tok // SEG_TILE * ROW_SUBLANES, LANES), F32),
    ]
    return in_specs, out_specs, out_shape


def _router(x2, router_w, router_b):
    n_tok, d = x2.shape
    in_specs, out_specs, out_shape = _router_specs(d, n_tok)
    return pl.pallas_call(
        _router_kernel,
        grid=(n_tok // (ROUTE_TILES * SEG_TILE),),
        in_specs=[pl.BlockSpec((ROUTE_TILES * SEG_TILE, d), lambda i: (i, 0))] + in_specs,
        out_specs=out_specs,
        out_shape=out_shape,
        compiler_params=_cparams("parallel"),
        name="moe_router",
    )(x2, *_router_operands(router_w, router_b))


def _proj_ln_route_kernel(o_ref, x_ref, w_ref, lg_ref, lb_ref, rw_ref, rb_ref,
                          out_ref, pos_ref, gate_ref, meta_ref, *, dn_alpha):
    def project(rows):
        y = _layer_norm(dn_alpha * x_ref[rows, :] + _dot(o_ref[rows, :], w_ref[...]), lg_ref[...], lb_ref[...])
        out_ref[rows, :] = y
        return y

    _route_tiles(project, rw_ref, rb_ref, pos_ref, gate_ref, meta_ref)


def _proj_residual_ln_route(o2, x2, w_o, ln_g, ln_b, router_w, router_b, dn_alpha):
    n_tok, d = x2.shape
    kdim = o2.shape[1]
    ts = ROUTE_TILES * SEG_TILE
    const = lambda i: (0, 0)
    r_in, r_out, r_shape = _router_specs(d, n_tok)
    res = pl.pallas_call(
        functools.partial(_proj_ln_route_kernel, dn_alpha=dn_alpha),
        grid=(n_tok // ts,),
        in_specs=[
            pl.BlockSpec((ts, kdim), lambda i: (i, 0)),
            pl.BlockSpec((ts, d), lambda i: (i, 0)),
            pl.BlockSpec((kdim, d), const),
            pl.BlockSpec((1, d), const),
            pl.BlockSpec((1, d), const),
        ] + r_in,
        out_specs=[pl.BlockSpec((ts, d), lambda i: (i, 0))] + r_out,
        out_shape=[jax.ShapeDtypeStruct((n_tok, d), F32)] + r_shape,
        compiler_params=_cparams("parallel"),
        name="mla_out_proj_ln_route",
    )(o2, x2, w_o.astype(BF16), ln_g.reshape(1, d), ln_b.reshape(1, d), *_router_operands(router_w, router_b))
    return res[0], tuple(res[1:])


def _segment_copy(local_ref, hbm_ref, sem, to_hbm, lo8, go8, bit):
    rows = SEG_ALIGN << bit
    aligned = lambda v8: v8 * SEG_ALIGN if isinstance(v8, int) else pl.multiple_of(v8 * SEG_ALIGN, SEG_ALIGN)
    loc = local_ref.at[pl.ds(aligned(lo8), rows)]
    hbm = hbm_ref.at[pl.ds(aligned(go8), rows)]
    return pltpu.make_async_copy(loc, hbm, sem) if to_hbm else pltpu.make_async_copy(hbm, loc, sem)


def _segment_starts(tabs, tile, n_exp, local_ref, hbm_ref, sem, to_hbm):
    seg8_ref, start8_ref, gstart8_ref, _ = tabs

    def body(e, c):
        n = seg8_ref[tile * n_exp + e]
        lo = start8_ref[tile * n_exp + e]
        go = gstart8_ref[tile * n_exp + e]

        def bits(lo_bit, hi_bit):
            for bit in range(lo_bit, hi_bit):
                @pl.when(((n >> bit) & 1) == 1)
                def _():
                    off = (n >> (bit + 1)) << (bit + 1)
                    _segment_copy(local_ref, hbm_ref, sem, to_hbm, lo + off, go + off, bit).start()

        bits(0, SEG_LOW_BITS)

        @pl.when(n >= (1 << SEG_LOW_BITS))
        def _():
            bits(SEG_LOW_BITS, SEG_BITS)
        return c

    lax.fori_loop(0, n_exp, body, 0)


def _segment_wait(tabs, tile, local_ref, hbm_ref, sem, to_hbm):
    total = tabs[3][tile]
    for bit in range(SEG_BITS):
        @pl.when(((total >> bit) & 1) == 1)
        def _():
            _segment_copy(local_ref, hbm_ref, sem, to_hbm, 0, 0, bit).wait()


def _dispatch_kernel(seg8_ref, start8_ref, gstart8_ref, tot8_ref, x_ref, pos_ref, xs_hbm, buf, sems, *, n_exp):
    td = x_ref.shape[0]
    seg_rows = buf.shape[1]
    tabs = (seg8_ref, start8_ref, gstart8_ref, tot8_ref)
    step = pl.program_id(0)
    slot = step % 2

    def wait(tile, sl):
        _segment_wait(tabs, tile, buf.at[sl], xs_hbm, sems.at[sl], True)

    @pl.when(step >= 2)
    def _():
        wait(step - 2, slot)

    xb = x_ref[...].astype(BF16)
    pos_t = pos_ref[...].T
    for r0 in range(0, seg_rows, PERM_CHUNK):
        rr = (lax.broadcasted_iota(jnp.int32, (PERM_CHUNK, td), 0) + r0).astype(F32)
        hit = pos_t[0:1] == rr
        for j in range(1, TOP_K):
            hit = hit | (pos_t[j:j + 1] == rr)
        buf[slot, r0:r0 + PERM_CHUNK, :] = _dot(jnp.where(hit, 1.0, 0.0).astype(BF16), xb)

    _segment_starts(tabs, step, n_exp, buf.at[slot], xs_hbm, sems.at[slot], True)

    @pl.when(step == pl.num_programs(0) - 1)
    def _():
        wait(step, slot)

        @pl.when(step >= 1)
        def _():
            wait(step - 1, 1 - slot)


def _seg_rows(td, n_exp):
    return -(-(td * TOP_K + n_exp * (SEG_ALIGN - 1)) // PERM_CHUNK) * PERM_CHUNK


def _dispatch(x2, pos, tabs, n_rows, n_exp):
    n_tok, d = x2.shape
    td = SEG_TILE
    grid_spec = pltpu.PrefetchScalarGridSpec(
        num_scalar_prefetch=len(tabs),
        grid=(n_tok // td,),
        in_specs=[
            pl.BlockSpec((td, d), lambda i, *_: (i, 0)),
            pl.BlockSpec((td, LANES), lambda i, *_: (i, 0)),
        ],
        out_specs=pl.BlockSpec(memory_space=pl.ANY),
        scratch_shapes=[pltpu.VMEM((2, _seg_rows(td, n_exp), d), F32), pltpu.SemaphoreType.DMA((2,))],
    )
    return pl.pallas_call(
        functools.partial(_dispatch_kernel, n_exp=n_exp),
        grid_spec=grid_spec,
        out_shape=jax.ShapeDtypeStruct((n_rows, d), F32),
        compiler_params=_cparams("arbitrary"),
        name="moe_dispatch",
    )(*tabs, x2, pos)


def _expert_kernel(be_ref, nu_ref, nxt_ref, xs_ref, w1_hbm, b1_ref, w2_hbm, b2_ref, perm_ref, ys_ref,
                   w1f_s, w2f_s, w1b_s, w2b_s, sems, *, layer):
    step = pl.program_id(0)
    d, f = w2b_s.shape[1], w2b_s.shape[0]
    active = step < nu_ref[0]
    new_expert = jnp.logical_or(step == 0, be_ref[step] != be_ref[jnp.maximum(step - 1, 0)])

    def fetch(e):
        return (pltpu.make_async_copy(w1_hbm.at[layer, e], w1f_s, sems.at[0]),
                pltpu.make_async_copy(w2_hbm.at[layer, e], w2f_s, sems.at[1]))

    @pl.when(jnp.logical_and(active, step == 0))
    def _():
        for cp in fetch(be_ref[0]):
            cp.start()

    @pl.when(jnp.logical_and(active, new_expert))
    def _():
        for cp in fetch(be_ref[step]):
            cp.wait()
        width = perm_ref.shape[0]
        for r0 in range(0, d, ROW_TILE):
            for c in range(2 * f // width):
                t = _dot(w1f_s[r0:r0 + ROW_TILE, c * width:(c + 1) * width].astype(BF16), perm_ref[...])
                w1b_s[r0:r0 + ROW_TILE, c * (width // 2):(c + 1) * (width // 2)] = t[:, :width // 2].astype(BF16)
                w1b_s[r0:r0 + ROW_TILE, f + c * (width // 2):f + (c + 1) * (width // 2)] = (
                    t[:, width // 2:].astype(BF16))
        w2b_s[...] = w2f_s[...].astype(BF16)

        @pl.when(nxt_ref[step] >= 0)
        def _():
            for cp in fetch(nxt_ref[step]):
                cp.start()

    @pl.when(active)
    def _():
        h = _dot(xs_ref[...].astype(BF16), w1b_s[...]) + b1_ref[0]
        glu = jnp.minimum(h[:, :f], SWIGLU_LIMIT)
        lin = jnp.clip(h[:, f:], -SWIGLU_LIMIT, SWIGLU_LIMIT)
        a = glu * _sigmoid(SWIGLU_ALPHA * glu) * (lin + 1.0)
        ys_ref[...] = _dot(a.astype(BF16), w2b_s[...]) + b2_ref[0]


def _experts(xs, block_e, n_used, next_e, w1_all, w2_all, layer, b1, b2):
    _, n_exp, f, d = w2_all.shape
    blk = MOE_BLOCK
    n_blocks = xs.shape[0] // blk
    width = 2 * LANES
    r = lax.broadcasted_iota(jnp.int32, (width, width), 0)
    c = lax.broadcasted_iota(jnp.int32, (width, width), 1)
    perm = (c == (r % 2) * (width // 2) + r // 2).astype(BF16)
    row_map = lambda i, be, nu, nx: (jnp.minimum(i, nu[0] - 1), 0)
    exp_map = lambda i, be, nu, nx: (be[i], 0, 0)
    grid_spec = pltpu.PrefetchScalarGridSpec(
        num_scalar_prefetch=3,
        grid=(n_blocks,),
        in_specs=[
            pl.BlockSpec((blk, d), row_map),
            pl.BlockSpec(memory_space=pl.ANY),
            pl.BlockSpec((1, 1, 2 * f), exp_map),
            pl.BlockSpec(memory_space=pl.ANY),
            pl.BlockSpec((1, 1, d), exp_map),
            pl.BlockSpec((width, width), lambda i, be, nu, nx: (0, 0)),
        ],
        out_specs=pl.BlockSpec((blk, d), row_map),
        scratch_shapes=[pltpu.VMEM((d, 2 * f), F32), pltpu.VMEM((f, d), F32),
                        pltpu.VMEM((d, 2 * f), BF16), pltpu.VMEM((f, d), BF16),
                        pltpu.SemaphoreType.DMA((2,))],
    )
    return pl.pallas_call(
        functools.partial(_expert_kernel, layer=layer),
        grid_spec=grid_spec,
        out_shape=jax.ShapeDtypeStruct(xs.shape, F32),
        compiler_params=_cparams("arbitrary"),
        name="moe_experts",
    )(block_e, n_used, next_e, xs, w1_all, b1.reshape(n_exp, 1, 2 * f), w2_all, b2.reshape(n_exp, 1, d), perm)


def _combine_kernel(seg8_ref, start8_ref, gstart8_ref, tot8_ref, ys_hbm, x_ref, pos_ref, gate_ref, lg_ref,
                    lb_ref, out_ref, buf, sems, *, n_exp, dn_alpha):
    td = x_ref.shape[0]
    seg_rows = buf.shape[1]
    tabs = (seg8_ref, start8_ref, gstart8_ref, tot8_ref)
    step = pl.program_id(0)
    slot = step % 2

    def start(tile, sl):
        _segment_starts(tabs, tile, n_exp, buf.at[sl], ys_hbm, sems.at[sl], False)

    @pl.when(step == 0)
    def _():
        buf[...] = jnp.zeros_like(buf)
        start(step, slot)

    @pl.when(step + 1 < pl.num_programs(0))
    def _():
        start(step + 1, 1 - slot)

    _segment_wait(tabs, step, buf.at[slot], ys_hbm, sems.at[slot], False)

    pos = pos_ref[...]
    gates = gate_ref[...]
    acc = dn_alpha * x_ref[...]
    for r0 in range(0, seg_rows, PERM_CHUNK):
        cc = (lax.broadcasted_iota(jnp.int32, (td, PERM_CHUNK), 1) + r0).astype(F32)
        g = jnp.zeros((td, PERM_CHUNK), F32)
        for j in range(TOP_K):
            g = jnp.where(pos[:, j:j + 1] == cc, gates[:, j:j + 1], g)
        acc = acc + _dot(g.astype(BF16), buf[slot, r0:r0 + PERM_CHUNK, :].astype(BF16))
    out_ref[...] = _layer_norm(acc, lg_ref[...], lb_ref[...])


def _combine_ln(ys, pos, gates, tabs, x2, ln_g, ln_b, n_exp, dn_alpha):
    n_tok, d = x2.shape
    td = SEG_TILE
    const = lambda i, *_: (0, 0)
    tile = lambda i, *_: (i, 0)
    grid_spec = pltpu.PrefetchScalarGridSpec(
        num_scalar_prefetch=len(tabs),
        grid=(n_tok // td,),
        in_specs=[
            pl.BlockSpec(memory_space=pl.ANY),
            pl.BlockSpec((td, d), tile),
            pl.BlockSpec((td, LANES), tile),
            pl.BlockSpec((td, LANES), tile),
            pl.BlockSpec((1, d), const),
            pl.BlockSpec((1, d), const),
        ],
        out_specs=pl.BlockSpec((td, d), tile),
        scratch_shapes=[pltpu.VMEM((2, _seg_rows(td, n_exp), d), F32), pltpu.SemaphoreType.DMA((2,))],
    )
    return pl.pallas_call(
        functools.partial(_combine_kernel, n_exp=n_exp, dn_alpha=dn_alpha),
        grid_spec=grid_spec,
        out_shape=jax.ShapeDtypeStruct((n_tok, d), F32),
        compiler_params=_cparams("arbitrary"),
        name="moe_combine_ln",
    )(*tabs, ys, x2, pos, gates, ln_g.reshape(1, d), ln_b.reshape(1, d))


def _moe_layer(x2, router_w, router_b, w1_all, w2_all, layer, b1, b2, ln_g, ln_b, dn_alpha, routing=None):
    n_tok, d = x2.shape
    n_exp = w2_all.shape[1]
    n_tiles = n_tok // SEG_TILE
    blk8 = MOE_BLOCK // SEG_ALIGN
    n_blocks = -(-(n_tok * TOP_K + n_tiles * n_exp * (SEG_ALIGN - 1)) // MOE_BLOCK) + n_exp
    n_rows = n_blocks * MOE_BLOCK

    pos, gates, meta = _router(x2, router_w, router_b) if routing is None else routing

    meta = meta.reshape(n_tiles, ROW_SUBLANES, LANES)
    seg8 = meta[:, 0, :n_exp].astype(jnp.int32)
    start8 = meta[:, 1, :n_exp].astype(jnp.int32)
    padded8 = (jnp.sum(seg8, axis=0) + blk8 - 1) // blk8 * blk8
    pends8 = jnp.cumsum(padded8)
    gstart8 = (pends8 - padded8)[None, :] + jnp.cumsum(seg8, axis=0) - seg8
    block_start8 = jnp.arange(n_blocks, dtype=jnp.int32) * blk8
    block_e = jnp.minimum(jnp.sum((pends8[None, :] <= block_start8[:, None]).astype(jnp.int32), axis=1),
                          n_exp - 1).astype(jnp.int32)
    n_used = (pends8[-1:] // blk8).astype(jnp.int32)
    ids = jnp.arange(n_exp, dtype=jnp.int32)
    later = jnp.where((ids[None, :] > ids[:, None]) & (padded8[None, :] > 0), ids[None, :], n_exp)
    next_present = jnp.min(later, axis=1)
    next_e = jnp.where(next_present < n_exp, next_present, -1)[block_e].astype(jnp.int32)
    tabs = (seg8.reshape(-1), start8.reshape(-1), gstart8.reshape(-1).astype(jnp.int32),
            jnp.sum(seg8, axis=1).astype(jnp.int32))

    b1_l = jnp.concatenate([b1[:, 0::2], b1[:, 1::2]], axis=-1)

    xs = _dispatch(x2, pos, tabs, n_rows, n_exp)
    ys = _experts(xs, block_e, n_used, next_e, w1_all, w2_all, layer, b1_l, b2)
    return _combine_ln(ys, pos, gates, tabs, x2, ln_g, ln_b, n_exp, dn_alpha)


def _rope_lane_tables(positions):
    half = ROPE_DIM // 2
    inv_freq = ROPE_THETA ** (-jnp.arange(0, ROPE_DIM, 2, dtype=F32) / ROPE_DIM)
    ang = positions.astype(F32)[..., None] * inv_freq
    cos, sin = jnp.cos(ang), jnp.sin(ang)
    z = jnp.zeros_like(cos)
    assert 4 * half == LANES
    return (jnp.concatenate([cos, z, cos, z], axis=-1),
            jnp.concatenate([-sin, z, sin, z], axis=-1))


def _rope_lane_columns(w_rope):
    half = ROPE_DIM // 2
    z = jnp.zeros(w_rope.shape[:-1] + (half,), w_rope.dtype)
    return jnp.concatenate([w_rope[..., :half], z, w_rope[..., half:], z], axis=-1)


def kernel(x, positions, ln_g, ln_b, hg_w_in, hg_lb, hg_gnorm, hg_w_o, mla_w_dq, mla_q_norm,
           mla_w_uq, mla_w_o, kv_w_a, kv_norm, kv_w_b, router_w, router_b, moe_w1, moe_b1,
           moe_w2, moe_b2):
    bsz, s_len, d = x.shape
    depth = ln_g.shape[0]
    n_a = hg_w_in.shape[0]
    dn_alpha = (2.0 * depth) ** 0.25
    scale = (NOPE_DIM + ROPE_DIM) ** -0.5 * LOG2_E

    lb_soft = jax.nn.softmax(hg_lb.astype(F32), axis=0)
    lower_bounds = jnp.cumsum(lb_soft, axis=0) - lb_soft[0]

    q_rank = mla_w_uq.shape[1]
    mla_heads = mla_w_uq.shape[2] // (NOPE_DIM + ROPE_DIM)
    cc = ss = k_full = v_full = None

    for layer in range(depth):
        routing = None
        if layer < n_a:
            x = _hgrn_layer(x, hg_w_in[layer], lower_bounds[layer], hg_gnorm[layer], hg_w_o[layer],
                            ln_g[layer, 0], ln_b[layer, 0], dn_alpha)
        else:
            j = layer - n_a
            if layer == n_a:
                cc, ss = _rope_lane_tables(positions)
                kv_w_a_l = jnp.concatenate(
                    [kv_w_a[:, :KV_RANK], _rope_lane_columns(kv_w_a[:, KV_RANK:])], axis=-1).astype(BF16)
                k_full, v_full = _shared_kv(x, kv_w_a_l, kv_norm, kv_w_b, cc, ss)
            w_uq = mla_w_uq[j].reshape(q_rank, mla_heads, NOPE_DIM + ROPE_DIM)
            w_uq_l = jnp.concatenate(
                [w_uq[..., :NOPE_DIM], _rope_lane_columns(w_uq[..., NOPE_DIM:])],
                axis=-1).reshape(q_rank, mla_heads * QK_DIM).astype(BF16)
            q_full = _mla_queries(x, mla_w_dq[j], mla_q_norm[j], w_uq_l, cc, ss, scale)
            o = _mla_attention(q_full, k_full, v_full)
            x, routing = _proj_residual_ln_route(
                o.reshape(bsz * s_len, -1), x.reshape(bsz * s_len, d), mla_w_o[j], ln_g[layer, 0],
                ln_b[layer, 0], router_w[layer], router_b[layer], dn_alpha)
        x = _moe_layer(x.reshape(bsz * s_len, d), router_w[layer], router_b[layer], moe_w1, moe_w2, layer,
                       moe_b1[layer], moe_b2[layer], ln_g[layer, 1], ln_b[layer, 1],
                       dn_alpha, routing).reshape(bsz, s_len, d)
    return x
```

```python
import functools

import jax
import jax.numpy as jnp
from jax import lax
from jax.experimental import pallas as pl
from jax.experimental.pallas import tpu as pltpu

F32 = jnp.float32
BF16 = jnp.bfloat16

HG_HEAD_DIM = 128
NOPE_DIM = 128
ROPE_DIM = 64
V_DIM = 128
VO_DIM = 256
LOG2_E = 1.4426950408889634
KV_RANK = 128
ROPE_THETA = 10000.0
TOP_K = 4
SWIGLU_ALPHA = 1.702
SWIGLU_LIMIT = 7.0
LN_EPS = 1e-5
RMS_EPS = 1e-6

LANES = 128
ROW_SUBLANES = 8
QK_DIM = 256
VMEM_LIMIT = 56 * 1024 * 1024

ROW_TILE = 512
MLA_TILE = 1024
HG_GROUP = 128
HG_HEAD_UNROLL = 8
ATT_TQ = 512
ATT_HEADS = 4
MOE_BLOCK = 512
SEG_TILE = 512
ROUTE_TILES = 2
SEG_ALIGN = ROW_SUBLANES
SEG_BITS = (SEG_TILE * TOP_K // SEG_ALIGN).bit_length()
SEG_LOW_BITS = 4
PERM_CHUNK = 256


def _cparams(*sem):
    return pltpu.CompilerParams(dimension_semantics=sem, vmem_limit_bytes=VMEM_LIMIT)


def _layer_norm(y, g, b):
    mu = jnp.mean(y, axis=-1, keepdims=True)
    d = y - mu
    var = jnp.mean(d * d, axis=-1, keepdims=True)
    return d * lax.rsqrt(var + LN_EPS) * g + b


def _dot(a, b):
    return jnp.dot(a, b, preferred_element_type=F32)


def _dot_nt(a, b):
    return lax.dot_general(a, b, (((1,), (1,)), ((), ())), preferred_element_type=F32)


def _dot_tn(a, b):
    return lax.dot_general(a, b, (((0,), (0,)), ((), ())), preferred_element_type=F32)


def _sigmoid(x):
    return 0.5 * jnp.tanh(0.5 * x) + 0.5


def _hgrn_kernel(x_ref, w_in_ref, lb_ref, gn_ref, w_o_ref, lg_ref, lbias_ref, out_ref,
                 q_s, f_s, i_s, g_s, mix_s, st_s, *, dn_alpha):
    ts, d = x_ref.shape[1], x_ref.shape[2]
    n_heads = d // HG_HEAD_DIM
    grp = HG_GROUP

    @pl.when(pl.program_id(1) == 0)
    def _():
        st_s[...] = jnp.zeros_like(st_s)

    x = x_ref[0]
    xb = x.astype(BF16)
    lb = lb_ref[...]

    for sec, dst in enumerate((q_s, f_s, i_s, g_s)):
        p = _dot(xb, w_in_ref[:, sec * d:(sec + 1) * d])
        if sec == 0 or sec == 3:
            p = p * _sigmoid(p)
        elif sec == 1:
            p = lb + (1.0 - lb) * _sigmoid(p)
        for h in range(n_heads):
            dst[h] = p[:, h * HG_HEAD_DIM:(h + 1) * HG_HEAD_DIM]

    row = lax.broadcasted_iota(jnp.int32, (grp, grp), 0)
    col = lax.broadcasted_iota(jnp.int32, (grp, grp), 1)
    n_levels = grp.bit_length() - 1
    pair_masks = [(((row >> lvl) ^ (col >> lvl)) == 1) & (row > col) for lvl in range(n_levels)]

    def head_body(h, carry):
        for r0 in range(0, ts, grp):
            q = q_s[h, r0:r0 + grp, :]
            fg = f_s[h, r0:r0 + grp, :]
            iv = i_s[h, r0:r0 + grp, :].astype(BF16)
            k = 1.0 - fg
            ep, es, et = fg, None, fg
            a = jnp.where(row == col, _dot_nt(q.astype(BF16), k.astype(BF16)), 0.0)
            for lvl in range(n_levels):
                half = 1 << lvl
                kl = k if es is None else k * es
                a = jnp.where(pair_masks[lvl], _dot_nt((q * ep).astype(BF16), kl.astype(BF16)), a)
                if half < ROW_SUBLANES:
                    odd = (row & half) != 0
                    et3 = et.reshape(grp // ROW_SUBLANES, ROW_SUBLANES, LANES)
                    other = jnp.where(odd, pltpu.roll(et3, half, 1).reshape(et.shape),
                                      pltpu.roll(et3, ROW_SUBLANES - half, 1).reshape(et.shape))
                    ep = jnp.where(odd, ep * other, ep)
                    es = jnp.where(odd, 1.0, other) if es is None else jnp.where(odd, es, es * other)
                    et = et * other
                else:
                    ep_p, es_p, et_p = [], [], []
                    for b0 in range(0, grp, 2 * half):
                        lo, mid, hi = b0, b0 + half, b0 + 2 * half
                        tot = et[lo:mid] * et[mid:hi]
                        ep_p += [ep[lo:mid], ep[mid:hi] * et[lo:mid]]
                        es_p += [es[lo:mid] * et[mid:hi], es[mid:hi]]
                        et_p += [tot, tot]
                    ep = jnp.concatenate(ep_p, axis=0)
                    es = jnp.concatenate(es_p, axis=0)
                    et = jnp.concatenate(et_p, axis=0)
            st = st_s[h]
            o = _dot(a.astype(BF16), iv) + _dot_nt((q * ep).astype(BF16), st.astype(BF16))
            st_s[h] = st * et[0:1, :] + _dot_tn(iv, (k * es).astype(BF16))
            ms = jnp.mean(o * o, axis=-1, keepdims=True)
            y = o * lax.rsqrt(ms + RMS_EPS) * gn_ref[...] * g_s[h, r0:r0 + grp, :]
            mix_s[h, r0:r0 + grp, :] = y.astype(BF16)
        return carry

    lax.fori_loop(0, n_heads, head_body, 0, unroll=HG_HEAD_UNROLL)

    acc = dn_alpha * x
    for h in range(0, n_heads, 2):
        acc = acc + _dot(jnp.concatenate([mix_s[h], mix_s[h + 1]], axis=-1), w_o_ref[h // 2])
    out_ref[0] = _layer_norm(acc, lg_ref[...], lbias_ref[...])


def _hgrn_layer(x, w_in, lb, gnorm, w_o, ln_g, ln_b, dn_alpha):
    bsz, s_len, d = x.shape
    n_heads = d // HG_HEAD_DIM
    ts = ROW_TILE
    const2 = lambda b, s: (0, 0)
    gate_scratch = pltpu.VMEM((n_heads, ts, HG_HEAD_DIM), F32)
    return pl.pallas_call(
        functools.partial(_hgrn_kernel, dn_alpha=dn_alpha),
        grid=(bsz, s_len // ts),
        in_specs=[
            pl.BlockSpec((1, ts, d), lambda b, s: (b, s, 0)),
            pl.BlockSpec((d, 4 * d), const2),
            pl.BlockSpec((1, d), const2),
            pl.BlockSpec((1, HG_HEAD_DIM), const2),
            pl.BlockSpec((n_heads // 2, 2 * HG_HEAD_DIM, d), lambda b, s: (0, 0, 0)),
            pl.BlockSpec((1, d), const2),
            pl.BlockSpec((1, d), const2),
        ],
        out_specs=pl.BlockSpec((1, ts, d), lambda b, s: (b, s, 0)),
        out_shape=jax.ShapeDtypeStruct((bsz, s_len, d), F32),
        scratch_shapes=[
            gate_scratch, gate_scratch, gate_scratch, gate_scratch,
            pltpu.VMEM((n_heads, ts, HG_HEAD_DIM), BF16),
            pltpu.VMEM((n_heads, HG_HEAD_DIM, HG_HEAD_DIM), F32),
        ],
        compiler_params=_cparams("parallel", "arbitrary"),
        name="hgrn2_layer",
    )(x, w_in.astype(BF16), lb.reshape(1, d), gnorm.reshape(1, HG_HEAD_DIM),
      w_o.astype(BF16).reshape(n_heads // 2, 2 * HG_HEAD_DIM, d), ln_g.reshape(1, d), ln_b.reshape(1, d))


def _rope_lanes(t, cc, ss):
    return t * cc + pltpu.roll(t, LANES // 2, 1) * ss


def _kv_kernel(x_ref, wa_ref, kvn_ref, wb_ref, cc_ref, ss_ref, k_ref, v_ref):
    n_heads = k_ref.shape[1]
    xb = x_ref[0].astype(BF16)
    ckr = _dot(xb, wa_ref[...])
    c = ckr[:, :KV_RANK]
    c = c * lax.rsqrt(jnp.mean(c * c, axis=-1, keepdims=True) + RMS_EPS) * kvn_ref[...]
    kr = _rope_lanes(ckr[:, KV_RANK:], cc_ref[0], ss_ref[0]).astype(BF16)
    cb = c.astype(BF16)
    for h in range(n_heads):
        kv = _dot(cb, wb_ref[:, h * (NOPE_DIM + V_DIM):(h + 1) * (NOPE_DIM + V_DIM)])
        k_ref[0, h, :, :NOPE_DIM] = kv[:, :NOPE_DIM].astype(BF16)
        k_ref[0, h, :, NOPE_DIM:] = kr
        v_ref[0, h, :, :V_DIM] = kv[:, NOPE_DIM:].astype(BF16)
        v_ref[0, h, :, V_DIM:] = jnp.ones((kv.shape[0], VO_DIM - V_DIM), BF16)


def _shared_kv(x, kv_w_a_l, kv_norm, kv_w_b, cc, ss):
    bsz, s_len, d = x.shape
    n_heads = kv_w_b.shape[1] // (NOPE_DIM + V_DIM)
    ts = MLA_TILE
    const2 = lambda b, s: (0, 0)
    return pl.pallas_call(
        _kv_kernel,
        grid=(bsz, s_len // ts),
        in_specs=[
            pl.BlockSpec((1, ts, d), lambda b, s: (b, s, 0)),
            pl.BlockSpec((d, KV_RANK + LANES), const2),
            pl.BlockSpec((1, KV_RANK), const2),
            pl.BlockSpec((KV_RANK, n_heads * (NOPE_DIM + V_DIM)), const2),
            pl.BlockSpec((1, ts, LANES), lambda b, s: (b, s, 0)),
            pl.BlockSpec((1, ts, LANES), lambda b, s: (b, s, 0)),
        ],
        out_specs=[
            pl.BlockSpec((1, n_heads, ts, QK_DIM), lambda b, s: (b, 0, s, 0)),
            pl.BlockSpec((1, n_heads, ts, VO_DIM), lambda b, s: (b, 0, s, 0)),
        ],
        out_shape=[
            jax.ShapeDtypeStruct((bsz, n_heads, s_len, QK_DIM), BF16),
            jax.ShapeDtypeStruct((bsz, n_heads, s_len, VO_DIM), BF16),
        ],
        compiler_params=_cparams("parallel", "parallel"),
        name="mla_shared_kv",
    )(x, kv_w_a_l, kv_norm.reshape(1, KV_RANK), kv_w_b.astype(BF16), cc, ss)


def _q_kernel(x_ref, wdq_ref, qn_ref, wuq_ref, cc_ref, ss_ref, q_ref, *, scale):
    n_heads = q_ref.shape[1]
    xb = x_ref[0].astype(BF16)
    c = _dot(xb, wdq_ref[...])
    c = c * lax.rsqrt(jnp.mean(c * c, axis=-1, keepdims=True) + RMS_EPS) * qn_ref[...]
    cb = c.astype(BF16)
    cc = cc_ref[0] * scale
    ss = ss_ref[0] * scale
    for h in range(n_heads):
        qh = _dot(cb, wuq_ref[:, h * QK_DIM:(h + 1) * QK_DIM])
        q_ref[0, h, :, :NOPE_DIM] = (qh[:, :NOPE_DIM] * scale).astype(BF16)
        q_ref[0, h, :, NOPE_DIM:] = _rope_lanes(qh[:, NOPE_DIM:], cc, ss).astype(BF16)


def _mla_queries(x, w_dq, q_norm, w_uq_l, cc, ss, scale):
    bsz, s_len, d = x.shape
    q_rank = w_dq.shape[1]
    n_heads = w_uq_l.shape[1] // QK_DIM
    ts = MLA_TILE
    const2 = lambda b, s: (0, 0)
    return pl.pallas_call(
        functools.partial(_q_kernel, scale=scale),
        grid=(bsz, s_len // ts),
        in_specs=[
            pl.BlockSpec((1, ts, d), lambda b, s: (b, s, 0)),
            pl.BlockSpec((d, q_rank), const2),
            pl.BlockSpec((1, q_rank), const2),
            pl.BlockSpec((q_rank, n_heads * QK_DIM), const2),
            pl.BlockSpec((1, ts, LANES), lambda b, s: (b, s, 0)),
            pl.BlockSpec((1, ts, LANES), lambda b, s: (b, s, 0)),
        ],
        out_specs=pl.BlockSpec((1, n_heads, ts, QK_DIM), lambda b, s: (b, 0, s, 0)),
        out_shape=jax.ShapeDtypeStruct((bsz, n_heads, s_len, QK_DIM), BF16),
        compiler_params=_cparams("parallel", "parallel"),
        name="mla_queries",
    )(x, w_dq.astype(BF16), q_norm.reshape(1, q_rank), w_uq_l, cc, ss)


def _attn_kernel(q_ref, k_ref, v_ref, o_ref):
    s_len = q_ref.shape[2]
    tq = min(ATT_TQ, s_len)
    row = lax.broadcasted_iota(jnp.int32, (tq, tq), 0)
    col = lax.broadcasted_iota(jnp.int32, (tq, tq), 1)
    for hh in range(q_ref.shape[1]):
        for qi in range(s_len // tq):
            q = q_ref[0, hh, qi * tq:(qi + 1) * tq, :]
            m = jnp.full((tq, 1), -jnp.inf, F32)
            acc = jnp.zeros((tq, VO_DIM), F32)
            for kj in range(qi + 1):
                s = _dot_nt(q, k_ref[0, hh, kj * tq:(kj + 1) * tq, :])
                if kj == qi:
                    s = jnp.where(col <= row, s, -jnp.inf)
                m_new = jnp.maximum(m, jnp.max(s, axis=-1, keepdims=True))
                p = jnp.exp2(s - m_new)
                acc = jnp.exp2(m - m_new) * acc + _dot(p.astype(BF16), v_ref[0, hh, kj * tq:(kj + 1) * tq, :])
                m = m_new
            o_ref[0, qi * tq:(qi + 1) * tq, hh * V_DIM:(hh + 1) * V_DIM] = (
                acc[:, :V_DIM] / acc[:, V_DIM:V_DIM + 1]).astype(o_ref.dtype)


def _mla_attention(q, k, v):
    bsz, n_heads, s_len, _ = q.shape
    return pl.pallas_call(
        _attn_kernel,
        grid=(bsz, n_heads // ATT_HEADS),
        in_specs=[
            pl.BlockSpec((1, ATT_HEADS, s_len, QK_DIM), lambda b, h: (b, h, 0, 0)),
            pl.BlockSpec((1, ATT_HEADS, s_len, QK_DIM), lambda b, h: (b, h, 0, 0)),
            pl.BlockSpec((1, ATT_HEADS, s_len, VO_DIM), lambda b, h: (b, h, 0, 0)),
        ],
        out_specs=pl.BlockSpec((1, s_len, ATT_HEADS * V_DIM), lambda b, h: (b, 0, h)),
        out_shape=jax.ShapeDtypeStruct((bsz, s_len, n_heads * V_DIM), BF16),
        compiler_params=_cparams("parallel", "parallel"),
        name="mla_attention",
    )(q, k, v)


def _split_hi_lo(v):
    hi = lax.bitcast_convert_type(lax.bitcast_convert_type(v, jnp.uint32) & jnp.uint32(0xFFFF0000), F32)
    return hi.astype(BF16), (v - hi).astype(BF16)


def _route_tile(x, w_ref, b_ref, pos_ref, gate_ref, meta_ref):
    tr = x.shape[0]

    x_hi, x_lo = _split_hi_lo(x)
    hi = _dot(x_hi, w_ref[...])
    logits = hi[:, :LANES] + (_dot(x_lo, w_ref[:, :LANES]) + hi[:, LANES:]) + b_ref[...]
    lane = lax.broadcasted_iota(jnp.int32, (tr, LANES), 1)
    work = logits
    sel = jnp.zeros((tr, LANES), F32)
    ids, vals = [], []
    for _ in range(TOP_K):
        mx = jnp.max(work, axis=-1, keepdims=True)
        idx = jnp.min(jnp.where(work == mx, lane, LANES), axis=-1, keepdims=True)
        hit = lane == idx
        ids.append(idx)
        vals.append(mx)
        sel = jnp.where(hit, 1.0, sel)
        work = jnp.where(hit, -jnp.inf, work)
    exps = [jnp.exp(v - vals[0]) for v in vals]
    denom = exps[0] + exps[1] + exps[2] + exps[3]

    r = lax.broadcasted_iota(jnp.int32, (tr, tr), 0)
    c = lax.broadcasted_iota(jnp.int32, (tr, tr), 1)
    before = _dot(jnp.where(c < r, 1.0, 0.0).astype(BF16), sel.astype(BF16))
    seg8 = jnp.floor((jnp.sum(sel, axis=0, keepdims=True) + (SEG_ALIGN - 1)) * (1.0 / SEG_ALIGN))
    er = lax.broadcasted_iota(jnp.int32, (LANES, LANES), 0)
    ec = lax.broadcasted_iota(jnp.int32, (LANES, LANES), 1)
    start8 = _dot(jnp.broadcast_to(seg8, (ROW_SUBLANES, LANES)).astype(BF16),
                  jnp.where(er < ec, 1.0, 0.0).astype(BF16))[0:1]
    slot = start8 * SEG_ALIGN + before

    pos_out = jnp.zeros((tr, LANES), F32)
    gate_out = jnp.zeros((tr, LANES), F32)
    for j in range(TOP_K):
        pos = jnp.sum(jnp.where(lane == ids[j], slot, 0.0), axis=-1, keepdims=True)
        pos_out = jnp.where(lane == j, pos, pos_out)
        gate_out = jnp.where(lane == j, exps[j] / denom, gate_out)
    pos_ref[...] = pos_out
    gate_ref[...] = gate_out
    row8 = lax.broadcasted_iota(jnp.int32, (ROW_SUBLANES, LANES), 0)
    meta_ref[...] = jnp.where(row8 == 0, seg8, jnp.where(row8 == 1, start8, 0.0))


def _route_tiles(x_of, w_ref, b_ref, pos_ref, gate_ref, meta_ref):
    for u in range(ROUTE_TILES):
        rows = pl.ds(u * SEG_TILE, SEG_TILE)
        _route_tile(x_of(rows), w_ref, b_ref, pos_ref.at[rows], gate_ref.at[rows],
                    meta_ref.at[pl.ds(u * ROW_SUBLANES, ROW_SUBLANES)])


def _router_kernel(x_ref, w_ref, b_ref, pos_ref, gate_ref, meta_ref):
    _route_tiles(lambda rows: x_ref[rows, :], w_ref, b_ref, pos_ref, gate_ref, meta_ref)


def _router_operands(router_w, router_b):
    d, n_exp = router_w.shape
    w_hi, w_lo = _split_hi_lo(jnp.zeros((d, LANES), F32).at[:, :n_exp].set(router_w))
    return (jnp.concatenate([w_hi, w_lo], axis=1),
            jnp.full((1, LANES), -jnp.inf, F32).at[0, :n_exp].set(router_b))


def _router_specs(d, n_tok):
    const = lambda i: (0, 0)
    tile = lambda i: (i, 0)
    rows = ROUTE_TILES * SEG_TILE
    in_specs = [pl.BlockSpec((d, 2 * LANES), const), pl.BlockSpec((1, LANES), const)]
    out_specs = [pl.BlockSpec((rows, LANES), tile), pl.BlockSpec((rows, LANES), tile),
                 pl.BlockSpec((ROUTE_TILES * ROW_SUBLANES, LANES), tile)]
    out_shape = [
        jax.ShapeDtypeStruct((n_tok, LANES), F32),
        jax.ShapeDtypeStruct((n_tok, LANES), F32),
        jax.ShapeDtypeStruct((n_tok // SEG_TILE * ROW_SUBLANES, LANES), F32),
    ]
    return in_specs, out_specs, out_shape


def _router(x2, router_w, router_b):
    n_tok, d = x2.shape
    in_specs, out_specs, out_shape = _router_specs(d, n_tok)
    return pl.pallas_call(
        _router_kernel,
        grid=(n_tok // (ROUTE_TILES * SEG_TILE),),
        in_specs=[pl.BlockSpec((ROUTE_TILES * SEG_TILE, d), lambda i: (i, 0))] + in_specs,
        out_specs=out_specs,
        out_shape=out_shape,
        compiler_params=_cparams("parallel"),
        name="moe_router",
    )(x2, *_router_operands(router_w, router_b))


def _proj_ln_route_kernel(o_ref, x_ref, w_ref, lg_ref, lb_ref, rw_ref, rb_ref,
                          out_ref, pos_ref, gate_ref, meta_ref, *, dn_alpha):
    def project(rows):
        y = _layer_norm(dn_alpha * x_ref[rows, :] + _dot(o_ref[rows, :], w_ref[...]), lg_ref[...], lb_ref[...])
        out_ref[rows, :] = y
        return y

    _route_tiles(project, rw_ref, rb_ref, pos_ref, gate_ref, meta_ref)


def _proj_residual_ln_route(o2, x2, w_o, ln_g, ln_b, router_w, router_b, dn_alpha):
    n_tok, d = x2.shape
    kdim = o2.shape[1]
    ts = ROUTE_TILES * SEG_TILE
    const = lambda i: (0, 0)
    r_in, r_out, r_shape = _router_specs(d, n_tok)
    res = pl.pallas_call(
        functools.partial(_proj_ln_route_kernel, dn_alpha=dn_alpha),
        grid=(n_tok // ts,),
        in_specs=[
            pl.BlockSpec((ts, kdim), lambda i: (i, 0)),
            pl.BlockSpec((ts, d), lambda i: (i, 0)),
            pl.BlockSpec((kdim, d), const),
            pl.BlockSpec((1, d), const),
            pl.BlockSpec((1, d), const),
        ] + r_in,
        out_specs=[pl.BlockSpec((ts, d), lambda i: (i, 0))] + r_out,
        out_shape=[jax.ShapeDtypeStruct((n_tok, d), F32)] + r_shape,
        compiler_params=_cparams("parallel"),
        name="mla_out_proj_ln_route",
    )(o2, x2, w_o.astype(BF16), ln_g.reshape(1, d), ln_b.reshape(1, d), *_router_operands(router_w, router_b))
    return res[0], tuple(res[1:])


def _segment_copy(local_ref, hbm_ref, sem, to_hbm, lo8, go8, bit):
    rows = SEG_ALIGN << bit
    aligned = lambda v8: v8 * SEG_ALIGN if isinstance(v8, int) else pl.multiple_of(v8 * SEG_ALIGN, SEG_ALIGN)
    loc = local_ref.at[pl.ds(aligned(lo8), rows)]
    hbm = hbm_ref.at[pl.ds(aligned(go8), rows)]
    return pltpu.make_async_copy(loc, hbm, sem) if to_hbm else pltpu.make_async_copy(hbm, loc, sem)


def _segment_starts(tabs, tile, n_exp, local_ref, hbm_ref, sem, to_hbm):
    seg8_ref, start8_ref, gstart8_ref, _ = tabs

    def body(e, c):
        n = seg8_ref[tile * n_exp + e]
        lo = start8_ref[tile * n_exp + e]
        go = gstart8_ref[tile * n_exp + e]

        def bits(lo_bit, hi_bit):
            for bit in range(lo_bit, hi_bit):
                @pl.when(((n >> bit) & 1) == 1)
                def _():
                    off = (n >> (bit + 1)) << (bit + 1)
                    _segment_copy(local_ref, hbm_ref, sem, to_hbm, lo + off, go + off, bit).start()

        bits(0, SEG_LOW_BITS)

        @pl.when(n >= (1 << SEG_LOW_BITS))
        def _():
            bits(SEG_LOW_BITS, SEG_BITS)
        return c

    lax.fori_loop(0, n_exp, body, 0)


def _segment_wait(tabs, tile, local_ref, hbm_ref, sem, to_hbm):
    total = tabs[3][tile]
    for bit in range(SEG_BITS):
        @pl.when(((total >> bit) & 1) == 1)
        def _():
            _segment_copy(local_ref, hbm_ref, sem, to_hbm, 0, 0, bit).wait()


def _dispatch_kernel(seg8_ref, start8_ref, gstart8_ref, tot8_ref, x_ref, pos_ref, xs_hbm, buf, sems, *, n_exp):
    td = x_ref.shape[0]
    seg_rows = buf.shape[1]
    tabs = (seg8_ref, start8_ref, gstart8_ref, tot8_ref)
    step = pl.program_id(0)
    slot = step % 2

    def wait(tile, sl):
        _segment_wait(tabs, tile, buf.at[sl], xs_hbm, sems.at[sl], True)

    @pl.when(step >= 2)
    def _():
        wait(step - 2, slot)

    xb = x_ref[...].astype(BF16)
    pos_t = pos_ref[...].T
    for r0 in range(0, seg_rows, PERM_CHUNK):
        rr = (lax.broadcasted_iota(jnp.int32, (PERM_CHUNK, td), 0) + r0).astype(F32)
        hit = pos_t[0:1] == rr
        for j in range(1, TOP_K):
            hit = hit | (pos_t[j:j + 1] == rr)
        buf[slot, r0:r0 + PERM_CHUNK, :] = _dot(jnp.where(hit, 1.0, 0.0).astype(BF16), xb)

    _segment_starts(tabs, step, n_exp, buf.at[slot], xs_hbm, sems.at[slot], True)

    @pl.when(step == pl.num_programs(0) - 1)
    def _():
        wait(step, slot)

        @pl.when(step >= 1)
        def _():
            wait(step - 1, 1 - slot)


def _seg_rows(td, n_exp):
    return -(-(td * TOP_K + n_exp * (SEG_ALIGN - 1)) // PERM_CHUNK) * PERM_CHUNK


def _dispatch(x2, pos, tabs, n_rows, n_exp):
    n_tok, d = x2.shape
    td = SEG_TILE
    grid_spec = pltpu.PrefetchScalarGridSpec(
        num_scalar_prefetch=len(tabs),
        grid=(n_tok // td,),
        in_specs=[
            pl.BlockSpec((td, d), lambda i, *_: (i, 0)),
            pl.BlockSpec((td, LANES), lambda i, *_: (i, 0)),
        ],
        out_specs=pl.BlockSpec(memory_space=pl.ANY),
        scratch_shapes=[pltpu.VMEM((2, _seg_rows(td, n_exp), d), F32), pltpu.SemaphoreType.DMA((2,))],
    )
    return pl.pallas_call(
        functools.partial(_dispatch_kernel, n_exp=n_exp),
        grid_spec=grid_spec,
        out_shape=jax.ShapeDtypeStruct((n_rows, d), F32),
        compiler_params=_cparams("arbitrary"),
        name="moe_dispatch",
    )(*tabs, x2, pos)


def _expert_kernel(be_ref, nu_ref, nxt_ref, xs_ref, w1_hbm, b1_ref, w2_hbm, b2_ref, perm_ref, ys_ref,
                   w1f_s, w2f_s, w1b_s, w2b_s, sems, *, layer):
    step = pl.program_id(0)
    d, f = w2b_s.shape[1], w2b_s.shape[0]
    active = step < nu_ref[0]
    new_expert = jnp.logical_or(step == 0, be_ref[step] != be_ref[jnp.maximum(step - 1, 0)])

    def fetch(e):
        return (pltpu.make_async_copy(w1_hbm.at[layer, e], w1f_s, sems.at[0]),
                pltpu.make_async_copy(w2_hbm.at[layer, e], w2f_s, sems.at[1]))

    @pl.when(jnp.logical_and(active, step == 0))
    def _():
        for cp in fetch(be_ref[0]):
            cp.start()

    @pl.when(jnp.logical_and(active, new_expert))
    def _():
        for cp in fetch(be_ref[step]):
            cp.wait()
        width = perm_ref.shape[0]
        for r0 in range(0, d, ROW_TILE):
            for c in range(2 * f // width):
                t = _dot(w1f_s[r0:r0 + ROW_TILE, c * width:(c + 1) * width].astype(BF16), perm_ref[...])
                w1b_s[r0:r0 + ROW_TILE, c * (width // 2):(c + 1) * (width // 2)] = t[:, :width // 2].astype(BF16)
                w1b_s[r0:r0 + ROW_TILE, f + c * (width // 2):f + (c + 1) * (width // 2)] = (
                    t[:, width // 2:].astype(BF16))
        w2b_s[...] = w2f_s[...].astype(BF16)

        @pl.when(nxt_ref[step] >= 0)
        def _():
            for cp in fetch(nxt_ref[step]):
                cp.start()

    @pl.when(active)
    def _():
        h = _dot(xs_ref[...].astype(BF16), w1b_s[...]) + b1_ref[0]
        glu = jnp.minimum(h[:, :f], SWIGLU_LIMIT)
        lin = jnp.clip(h[:, f:], -SWIGLU_LIMIT, SWIGLU_LIMIT)
        a = glu * _sigmoid(SWIGLU_ALPHA * glu) * (lin + 1.0)
        ys_ref[...] = _dot(a.astype(BF16), w2b_s[...]) + b2_ref[0]


def _experts(xs, block_e, n_used, next_e, w1_all, w2_all, layer, b1, b2):
    _, n_exp, f, d = w2_all.shape
    blk = MOE_BLOCK
    n_blocks = xs.shape[0] // blk
    width = 2 * LANES
    r = lax.broadcasted_iota(jnp.int32, (width, width), 0)
    c = lax.broadcasted_iota(jnp.int32, (width, width), 1)
    perm = (c == (r % 2) * (width // 2) + r // 2).astype(BF16)
    row_map = lambda i, be, nu, nx: (jnp.minimum(i, nu[0] - 1), 0)
    exp_map = lambda i, be, nu, nx: (be[i], 0, 0)
    grid_spec = pltpu.PrefetchScalarGridSpec(
        num_scalar_prefetch=3,
        grid=(n_blocks,),
        in_specs=[
            pl.BlockSpec((blk, d), row_map),
            pl.BlockSpec(memory_space=pl.ANY),
            pl.BlockSpec((1, 1, 2 * f), exp_map),
            pl.BlockSpec(memory_space=pl.ANY),
            pl.BlockSpec((1, 1, d), exp_map),
            pl.BlockSpec((width, width), lambda i, be, nu, nx: (0, 0)),
        ],
        out_specs=pl.BlockSpec((blk, d), row_map),
        scratch_shapes=[pltpu.VMEM((d, 2 * f), F32), pltpu.VMEM((f, d), F32),
                        pltpu.VMEM((d, 2 * f), BF16), pltpu.VMEM((f, d), BF16),
                        pltpu.SemaphoreType.DMA((2,))],
    )
    return pl.pallas_call(
        functools.partial(_expert_kernel, layer=layer),
        grid_spec=grid_spec,
        out_shape=jax.ShapeDtypeStruct(xs.shape, F32),
        compiler_params=_cparams("arbitrary"),
        name="moe_experts",
    )(block_e, n_used, next_e, xs, w1_all, b1.reshape(n_exp, 1, 2 * f), w2_all, b2.reshape(n_exp, 1, d), perm)


def _combine_kernel(seg8_ref, start8_ref, gstart8_ref, tot8_ref, ys_hbm, x_ref, pos_ref, gate_ref, lg_ref,
                    lb_ref, out_ref, buf, sems, *, n_exp, dn_alpha):
    td = x_ref.shape[0]
    seg_rows = buf.shape[1]
    tabs = (seg8_ref, start8_ref, gstart8_ref, tot8_ref)
    step = pl.program_id(0)
    slot = step % 2

    def start(tile, sl):
        _segment_starts(tabs, tile, n_exp, buf.at[sl], ys_hbm, sems.at[sl], False)

    @pl.when(step == 0)
    def _():
        buf[...] = jnp.zeros_like(buf)
        start(step, slot)

    @pl.when(step + 1 < pl.num_programs(0))
    def _():
        start(step + 1, 1 - slot)

    _segment_wait(tabs, step, buf.at[slot], ys_hbm, sems.at[slot], False)

    pos = pos_ref[...]
    gates = gate_ref[...]
    acc = dn_alpha * x_ref[...]
    for r0 in range(0, seg_rows, PERM_CHUNK):
        cc = (lax.broadcasted_iota(jnp.int32, (td, PERM_CHUNK), 1) + r0).astype(F32)
        g = jnp.zeros((td, PERM_CHUNK), F32)
        for j in range(TOP_K):
            g = jnp.where(pos[:, j:j + 1] == cc, gates[:, j:j + 1], g)
        acc = acc + _dot(g.astype(BF16), buf[slot, r0:r0 + PERM_CHUNK, :].astype(BF16))
    out_ref[...] = _layer_norm(acc, lg_ref[...], lb_ref[...])


def _combine_ln(ys, pos, gates, tabs, x2, ln_g, ln_b, n_exp, dn_alpha):
    n_tok, d = x2.shape
    td = SEG_TILE
    const = lambda i, *_: (0, 0)
    tile = lambda i, *_: (i, 0)
    grid_spec = pltpu.PrefetchScalarGridSpec(
        num_scalar_prefetch=len(tabs),
        grid=(n_tok // td,),
        in_specs=[
            pl.BlockSpec(memory_space=pl.ANY),
            pl.BlockSpec((td, d), tile),
            pl.BlockSpec((td, LANES), tile),
            pl.BlockSpec((td, LANES), tile),
            pl.BlockSpec((1, d), const),
            pl.BlockSpec((1, d), const),
        ],
        out_specs=pl.BlockSpec((td, d), tile),
        scratch_shapes=[pltpu.VMEM((2, _seg_rows(td, n_exp), d), F32), pltpu.SemaphoreType.DMA((2,))],
    )
    return pl.pallas_call(
        functools.partial(_combine_kernel, n_exp=n_exp, dn_alpha=dn_alpha),
        grid_spec=grid_spec,
        out_shape=jax.ShapeDtypeStruct((n_tok, d), F32),
        compiler_params=_cparams("arbitrary"),
        name="moe_combine_ln",
    )(*tabs, ys, x2, pos, gates, ln_g.reshape(1, d), ln_b.reshape(1, d))


def _moe_layer(x2, router_w, router_b, w1_all, w2_all, layer, b1, b2, ln_g, ln_b, dn_alpha, routing=None):
    n_tok, d = x2.shape
    n_exp = w2_all.shape[1]
    n_tiles = n_tok // SEG_TILE
    blk8 = MOE_BLOCK // SEG_ALIGN
    n_blocks = -(-(n_tok * TOP_K + n_tiles * n_exp * (SEG_ALIGN - 1)) // MOE_BLOCK) + n_exp
    n_rows = n_blocks * MOE_BLOCK

    pos, gates, meta = _router(x2, router_w, router_b) if routing is None else routing

    meta = meta.reshape(n_tiles, ROW_SUBLANES, LANES)
    seg8 = meta[:, 0, :n_exp].astype(jnp.int32)
    start8 = meta[:, 1, :n_exp].astype(jnp.int32)
    padded8 = (jnp.sum(seg8, axis=0) + blk8 - 1) // blk8 * blk8
    pends8 = jnp.cumsum(padded8)
    gstart8 = (pends8 - padded8)[None, :] + jnp.cumsum(seg8, axis=0) - seg8
    block_start8 = jnp.arange(n_blocks, dtype=jnp.int32) * blk8
    block_e = jnp.minimum(jnp.sum((pends8[None, :] <= block_start8[:, None]).astype(jnp.int32), axis=1),
                          n_exp - 1).astype(jnp.int32)
    n_used = (pends8[-1:] // blk8).astype(jnp.int32)
    ids = jnp.arange(n_exp, dtype=jnp.int32)
    later = jnp.where((ids[None, :] > ids[:, None]) & (padded8[None, :] > 0), ids[None, :], n_exp)
    next_present = jnp.min(later, axis=1)
    next_e = jnp.where(next_present < n_exp, next_present, -1)[block_e].astype(jnp.int32)
    tabs = (seg8.reshape(-1), start8.reshape(-1), gstart8.reshape(-1).astype(jnp.int32),
            jnp.sum(seg8, axis=1).astype(jnp.int32))

    b1_l = jnp.concatenate([b1[:, 0::2], b1[:, 1::2]], axis=-1)

    xs = _dispatch(x2, pos, tabs, n_rows, n_exp)
    ys = _experts(xs, block_e, n_used, next_e, w1_all, w2_all, layer, b1_l, b2)
    return _combine_ln(ys, pos, gates, tabs, x2, ln_g, ln_b, n_exp, dn_alpha)


def _rope_lane_tables(positions):
    half = ROPE_DIM // 2
    inv_freq = ROPE_THETA ** (-jnp.arange(0, ROPE_DIM, 2, dtype=F32) / ROPE_DIM)
    ang = positions.astype(F32)[..., None] * inv_freq
    cos, sin = jnp.cos(ang), jnp.sin(ang)
    z = jnp.zeros_like(cos)
    assert 4 * half == LANES
    return (jnp.concatenate([cos, z, cos, z], axis=-1),
            jnp.concatenate([-sin, z, sin, z], axis=-1))


def _rope_lane_columns(w_rope):
    half = ROPE_DIM // 2
    z = jnp.zeros(w_rope.shape[:-1] + (half,), w_rope.dtype)
    return jnp.concatenate([w_rope[..., :half], z, w_rope[..., half:], z], axis=-1)


def kernel(x, positions, ln_g, ln_b, hg_w_in, hg_lb, hg_gnorm, hg_w_o, mla_w_dq, mla_q_norm,
           mla_w_uq, mla_w_o, kv_w_a, kv_norm, kv_w_b, router_w, router_b, moe_w1, moe_b1,
           moe_w2, moe_b2):
    bsz, s_len, d = x.shape
    depth = ln_g.shape[0]
    n_a = hg_w_in.shape[0]
    dn_alpha = (2.0 * depth) ** 0.25
    scale = (NOPE_DIM + ROPE_DIM) ** -0.5 * LOG2_E

    lb_soft = jax.nn.softmax(hg_lb.astype(F32), axis=0)
    lower_bounds = jnp.cumsum(lb_soft, axis=0) - lb_soft[0]

    q_rank = mla_w_uq.shape[1]
    mla_heads = mla_w_uq.shape[2] // (NOPE_DIM + ROPE_DIM)
    cc = ss = k_full = v_full = None

    for layer in range(depth):
        routing = None
        if layer < n_a:
            x = _hgrn_layer(x, hg_w_in[layer], lower_bounds[layer], hg_gnorm[layer], hg_w_o[layer],
                            ln_g[layer, 0], ln_b[layer, 0], dn_alpha)
        else:
            j = layer - n_a
            if layer == n_a:
                cc, ss = _rope_lane_tables(positions)
                kv_w_a_l = jnp.concatenate(
                    [kv_w_a[:, :KV_RANK], _rope_lane_columns(kv_w_a[:, KV_RANK:])], axis=-1).astype(BF16)
                k_full, v_full = _shared_kv(x, kv_w_a_l, kv_norm, kv_w_b, cc, ss)
            w_uq = mla_w_uq[j].reshape(q_rank, mla_heads, NOPE_DIM + ROPE_DIM)
            w_uq_l = jnp.concatenate(
                [w_uq[..., :NOPE_DIM], _rope_lane_columns(w_uq[..., NOPE_DIM:])],
                axis=-1).reshape(q_rank, mla_heads * QK_DIM).astype(BF16)
            q_full = _mla_queries(x, mla_w_dq[j], mla_q_norm[j], w_uq_l, cc, ss, scale)
            o = _mla_attention(q_full, k_full, v_full)
            x, routing = _proj_residual_ln_route(
                o.reshape(bsz * s_len, -1), x.reshape(bsz * s_len, d), mla_w_o[j], ln_g[layer, 0],
                ln_b[layer, 0], router_w[layer], router_b[layer], dn_alpha)
        x = _moe_layer(x.reshape(bsz * s_len, d), router_w[layer], router_b[layer], moe_w1, moe_w2, layer,
                       moe_b1[layer], moe_b2[layer], ln_g[layer, 1], ln_b[layer, 1],
                       dn_alpha, routing).reshape(bsz, s_len, d)
    return x
```

```python
import functools

import jax
import jax.numpy as jnp
from jax import lax
from jax.experimental import pallas as pl
from jax.experimental.pallas import tpu as pltpu

F32 = jnp.float32
BF16 = jnp.bfloat16

HG_HEAD_DIM = 128
NOPE_DIM = 128
ROPE_DIM = 64
V_DIM = 128
VO_DIM = 256
LOG2_E = 1.4426950408889634
KV_RANK = 128
ROPE_THETA = 10000.0
TOP_K = 4
SWIGLU_ALPHA = 1.702
SWIGLU_LIMIT = 7.0
LN_EPS = 1e-5
RMS_EPS = 1e-6

LANES = 128
ROW_SUBLANES = 8
QK_DIM = 256
VMEM_LIMIT = 56 * 1024 * 1024

ROW_TILE = 512
MLA_TILE = 1024
HG_GROUP = 128
HG_HEAD_UNROLL = 8
ATT_TQ = 512
ATT_HEADS = 4
MOE_BLOCK = 512
SEG_TILE = 512
ROUTE_TILES = 2
SEG_ALIGN = ROW_SUBLANES
SEG_BITS = (SEG_TILE * TOP_K // SEG_ALIGN).bit_length()
SEG_LOW_BITS = 4
PERM_CHUNK = 256


def _cparams(*sem):
    return pltpu.CompilerParams(dimension_semantics=sem, vmem_limit_bytes=VMEM_LIMIT)


def _layer_norm(y, g, b):
    mu = jnp.mean(y, axis=-1, keepdims=True)
    d = y - mu
    var = jnp.mean(d * d, axis=-1, keepdims=True)
    return d * lax.rsqrt(var + LN_EPS) * g + b


def _dot(a, b):
    return jnp.dot(a, b, preferred_element_type=F32)


def _dot_nt(a, b):
    return lax.dot_general(a, b, (((1,), (1,)), ((), ())), preferred_element_type=F32)


def _dot_tn(a, b):
    return lax.dot_general(a, b, (((0,), (0,)), ((), ())), preferred_element_type=F32)


def _sigmoid(x):
    return 0.5 * jnp.tanh(0.5 * x) + 0.5


def _hgrn_kernel(x_ref, w_in_ref, lb_ref, gn_ref, w_o_ref, lg_ref, lbias_ref, out_ref,
                 q_s, f_s, i_s, g_s, mix_s, st_s, *, dn_alpha):
    ts, d = x_ref.shape[1], x_ref.shape[2]
    n_heads = d // HG_HEAD_DIM
    grp = HG_GROUP

    @pl.when(pl.program_id(1) == 0)
    def _():
        st_s[...] = jnp.zeros_like(st_s)

    x = x_ref[0]
    xb = x.astype(BF16)
    lb = lb_ref[...]

    for sec, dst in enumerate((q_s, f_s, i_s, g_s)):
        p = _dot(xb, w_in_ref[:, sec * d:(sec + 1) * d])
        if sec == 0 or sec == 3:
            p = p * _sigmoid(p)
        elif sec == 1:
            p = lb + (1.0 - lb) * _sigmoid(p)
        for h in range(n_heads):
            dst[h] = p[:, h * HG_HEAD_DIM:(h + 1) * HG_HEAD_DIM]

    row = lax.broadcasted_iota(jnp.int32, (grp, grp), 0)
    col = lax.broadcasted_iota(jnp.int32, (grp, grp), 1)
    n_levels = grp.bit_length() - 1
    pair_masks = [(((row >> lvl) ^ (col >> lvl)) == 1) & (row > col) for lvl in range(n_levels)]

    def head_body(h, carry):
        for r0 in range(0, ts, grp):
            q = q_s[h, r0:r0 + grp, :]
            fg = f_s[h, r0:r0 + grp, :]
            iv = i_s[h, r0:r0 + grp, :].astype(BF16)
            k = 1.0 - fg
            ep, es, et = fg, None, fg
            a = jnp.where(row == col, _dot_nt(q.astype(BF16), k.astype(BF16)), 0.0)
            for lvl in range(n_levels):
                half = 1 << lvl
                kl = k if es is None else k * es
                a = jnp.where(pair_masks[lvl], _dot_nt((q * ep).astype(BF16), kl.astype(BF16)), a)
                if half < ROW_SUBLANES:
                    odd = (row & half) != 0
                    et3 = et.reshape(grp // ROW_SUBLANES, ROW_SUBLANES, LANES)
                    other = jnp.where(odd, pltpu.roll(et3, half, 1).reshape(et.shape),
                                      pltpu.roll(et3, ROW_SUBLANES - half, 1).reshape(et.shape))
                    ep = jnp.where(odd, ep * other, ep)
                    es = jnp.where(odd, 1.0, other) if es is None else jnp.where(odd, es, es * other)
                    et = et * other
                else:
                    ep_p, es_p, et_p = [], [], []
                    for b0 in range(0, grp, 2 * half):
                        lo, mid, hi = b0, b0 + half, b0 + 2 * half
                        tot = et[lo:mid] * et[mid:hi]
                        ep_p += [ep[lo:mid], ep[mid:hi] * et[lo:mid]]
                        es_p += [es[lo:mid] * et[mid:hi], es[mid:hi]]
                        et_p += [tot, tot]
                    ep = jnp.concatenate(ep_p, axis=0)
                    es = jnp.concatenate(es_p, axis=0)
                    et = jnp.concatenate(et_p, axis=0)
            st = st_s[h]
            o = _dot(a.astype(BF16), iv) + _dot_nt((q * ep).astype(BF16), st.astype(BF16))
            st_s[h] = st * et[0:1, :] + _dot_tn(iv, (k * es).astype(BF16))
            ms = jnp.mean(o * o, axis=-1, keepdims=True)
            y = o * lax.rsqrt(ms + RMS_EPS) * gn_ref[...] * g_s[h, r0:r0 + grp, :]
            mix_s[h, r0:r0 + grp, :] = y.astype(BF16)
        return carry

    lax.fori_loop(0, n_heads, head_body, 0, unroll=HG_HEAD_UNROLL)

    acc = dn_alpha * x
    for h in range(0, n_heads, 2):
        acc = acc + _dot(jnp.concatenate([mix_s[h], mix_s[h + 1]], axis=-1), w_o_ref[h // 2])
    out_ref[0] = _layer_norm(acc, lg_ref[...], lbias_ref[...])


def _hgrn_layer(x, w_in, lb, gnorm, w_o, ln_g, ln_b, dn_alpha):
    bsz, s_len, d = x.shape
    n_heads = d // HG_HEAD_DIM
    ts = ROW_TILE
    const2 = lambda b, s: (0, 0)
    gate_scratch = pltpu.VMEM((n_heads, ts, HG_HEAD_DIM), F32)
    return pl.pallas_call(
        functools.partial(_hgrn_kernel, dn_alpha=dn_alpha),
        grid=(bsz, s_len // ts),
        in_specs=[
            pl.BlockSpec((1, ts, d), lambda b, s: (b, s, 0)),
            pl.BlockSpec((d, 4 * d), const2),
            pl.BlockSpec((1, d), const2),
            pl.BlockSpec((1, HG_HEAD_DIM), const2),
            pl.BlockSpec((n_heads // 2, 2 * HG_HEAD_DIM, d), lambda b, s: (0, 0, 0)),
            pl.BlockSpec((1, d), const2),
            pl.BlockSpec((1, d), const2),
        ],
        out_specs=pl.BlockSpec((1, ts, d), lambda b, s: (b, s, 0)),
        out_shape=jax.ShapeDtypeStruct((bsz, s_len, d), F32),
        scratch_shapes=[
            gate_scratch, gate_scratch, gate_scratch, gate_scratch,
            pltpu.VMEM((n_heads, ts, HG_HEAD_DIM), BF16),
            pltpu.VMEM((n_heads, HG_HEAD_DIM, HG_HEAD_DIM), F32),
        ],
        compiler_params=_cparams("parallel", "arbitrary"),
        name="hgrn2_layer",
    )(x, w_in.astype(BF16), lb.reshape(1, d), gnorm.reshape(1, HG_HEAD_DIM),
      w_o.astype(BF16).reshape(n_heads // 2, 2 * HG_HEAD_DIM, d), ln_g.reshape(1, d), ln_b.reshape(1, d))


def _rope_lanes(t, cc, ss):
    return t * cc + pltpu.roll(t, LANES // 2, 1) * ss


def _kv_kernel(x_ref, wa_ref, kvn_ref, wb_ref, cc_ref, ss_ref, k_ref, v_ref):
    n_heads = k_ref.shape[1]
    xb = x_ref[0].astype(BF16)
    ckr = _dot(xb, wa_ref[...])
    c = ckr[:, :KV_RANK]
    c = c * lax.rsqrt(jnp.mean(c * c, axis=-1, keepdims=True) + RMS_EPS) * kvn_ref[...]
    kr = _rope_lanes(ckr[:, KV_RANK:], cc_ref[0], ss_ref[0]).astype(BF16)
    cb = c.astype(BF16)
    for h in range(n_heads):
        kv = _dot(cb, wb_ref[:, h * (NOPE_DIM + V_DIM):(h + 1) * (NOPE_DIM + V_DIM)])
        k_ref[0, h, :, :NOPE_DIM] = kv[:, :NOPE_DIM].astype(BF16)
        k_ref[0, h, :, NOPE_DIM:] = kr
        v_ref[0, h, :, :V_DIM] = kv[:, NOPE_DIM:].astype(BF16)
        v_ref[0, h, :, V_DIM:] = jnp.ones((kv.shape[0], VO_DIM - V_DIM), BF16)


def _shared_kv(x, kv_w_a_l, kv_norm, kv_w_b, cc, ss):
    bsz, s_len, d = x.shape
    n_heads = kv_w_b.shape[1] // (NOPE_DIM + V_DIM)
    ts = MLA_TILE
    const2 = lambda b, s: (0, 0)
    return pl.pallas_call(
        _kv_kernel,
        grid=(bsz, s_len // ts),
        in_specs=[
            pl.BlockSpec((1, ts, d), lambda b, s: (b, s, 0)),
            pl.BlockSpec((d, KV_RANK + LANES), const2),
            pl.BlockSpec((1, KV_RANK), const2),
            pl.BlockSpec((KV_RANK, n_heads * (NOPE_DIM + V_DIM)), const2),
            pl.BlockSpec((1, ts, LANES), lambda b, s: (b, s, 0)),
            pl.BlockSpec((1, ts, LANES), lambda b, s: (b, s, 0)),
        ],
        out_specs=[
            pl.BlockSpec((1, n_heads, ts, QK_DIM), lambda b, s: (b, 0, s, 0)),
            pl.BlockSpec((1, n_heads, ts, VO_DIM), lambda b, s: (b, 0, s, 0)),
        ],
        out_shape=[
            jax.ShapeDtypeStruct((bsz, n_heads, s_len, QK_DIM), BF16),
            jax.ShapeDtypeStruct((bsz, n_heads, s_len, VO_DIM), BF16),
        ],
        compiler_params=_cparams("parallel", "parallel"),
        name="mla_shared_kv",
    )(x, kv_w_a_l, kv_norm.reshape(1, KV_RANK), kv_w_b.astype(BF16), cc, ss)


def _q_kernel(x_ref, wdq_ref, qn_ref, wuq_ref, cc_ref, ss_ref, q_ref, *, scale):
    n_heads = q_ref.shape[1]
    xb = x_ref[0].astype(BF16)
    c = _dot(xb, wdq_ref[...])
    c = c * lax.rsqrt(jnp.mean(c * c, axis=-1, keepdims=True) + RMS_EPS) * qn_ref[...]
    cb = c.astype(BF16)
    cc = cc_ref[0] * scale
    ss = ss_ref[0] * scale
    for h in range(n_heads):
        qh = _dot(cb, wuq_ref[:, h * QK_DIM:(h + 1) * QK_DIM])
        q_ref[0, h, :, :NOPE_DIM] = (qh[:, :NOPE_DIM] * scale).astype(BF16)
        q_ref[0, h, :, NOPE_DIM:] = _rope_lanes(qh[:, NOPE_DIM:], cc, ss).astype(BF16)


def _mla_queries(x, w_dq, q_norm, w_uq_l, cc, ss, scale):
    bsz, s_len, d = x.shape
    q_rank = w_dq.shape[1]
    n_heads = w_uq_l.shape[1] // QK_DIM
    ts = MLA_TILE
    const2 = lambda b, s: (0, 0)
    return pl.pallas_call(
        functools.partial(_q_kernel, scale=scale),
        grid=(bsz, s_len // ts),
        in_specs=[
            pl.BlockSpec((1, ts, d), lambda b, s: (b, s, 0)),
            pl.BlockSpec((d, q_rank), const2),
            pl.BlockSpec((1, q_rank), const2),
            pl.BlockSpec((q_rank, n_heads * QK_DIM), const2),
            pl.BlockSpec((1, ts, LANES), lambda b, s: (b, s, 0)),
            pl.BlockSpec((1, ts, LANES), lambda b, s: (b, s, 0)),
        ],
        out_specs=pl.BlockSpec((1, n_heads, ts, QK_DIM), lambda b, s: (b, 0, s, 0)),
        out_shape=jax.ShapeDtypeStruct((bsz, n_heads, s_len, QK_DIM), BF16),
        compiler_params=_cparams("parallel", "parallel"),
        name="mla_queries",
    )(x, w_dq.astype(BF16), q_norm.reshape(1, q_rank), w_uq_l, cc, ss)


def _attn_kernel(q_ref, k_ref, v_ref, o_ref):
    s_len = q_ref.shape[2]
    tq = min(ATT_TQ, s_len)
    row = lax.broadcasted_iota(jnp.int32, (tq, tq), 0)
    col = lax.broadcasted_iota(jnp.int32, (tq, tq), 1)
    for hh in range(q_ref.shape[1]):
        for qi in range(s_len // tq):
            q = q_ref[0, hh, qi * tq:(qi + 1) * tq, :]
            m = jnp.full((tq, 1), -jnp.inf, F32)
            acc = jnp.zeros((tq, VO_DIM), F32)
            for kj in range(qi + 1):
                s = _dot_nt(q, k_ref[0, hh, kj * tq:(kj + 1) * tq, :])
                if kj == qi:
                    s = jnp.where(col <= row, s, -jnp.inf)
                m_new = jnp.maximum(m, jnp.max(s, axis=-1, keepdims=True))
                p = jnp.exp2(s - m_new)
                acc = jnp.exp2(m - m_new) * acc + _dot(p.astype(BF16), v_ref[0, hh, kj * tq:(kj + 1) * tq, :])
                m = m_new
            o_ref[0, qi * tq:(qi + 1) * tq, hh * V_DIM:(hh + 1) * V_DIM] = (
                acc[:, :V_DIM] / acc[:, V_DIM:V_DIM + 1]).astype(o_ref.dtype)


def _mla_attention(q, k, v):
    bsz, n_heads, s_len, _ = q.shape
    return pl.pallas_call(
        _attn_kernel,
        grid=(bsz, n_heads // ATT_HEADS),
        in_specs=[
            pl.BlockSpec((1, ATT_HEADS, s_len, QK_DIM), lambda b, h: (b, h, 0, 0)),
            pl.BlockSpec((1, ATT_HEADS, s_len, QK_DIM), lambda b, h: (b, h, 0, 0)),
            pl.BlockSpec((1, ATT_HEADS, s_len, VO_DIM), lambda b, h: (b, h, 0, 0)),
        ],
        out_specs=pl.BlockSpec((1, s_len, ATT_HEADS * V_DIM), lambda b, h: (b, 0, h)),
        out_shape=jax.ShapeDtypeStruct((bsz, s_len, n_heads * V_DIM), BF16),
        compiler_params=_cparams("parallel", "parallel"),
        name="mla_attention",
    )(q, k, v)


def _split_hi_lo(v):
    hi = lax.bitcast_convert_type(lax.bitcast_convert_type(v, jnp.uint32) & jnp.uint32(0xFFFF0000), F32)
    return hi.astype(BF16), (v - hi).astype(BF16)


def _route_tile(x, w_ref, b_ref, pos_ref, gate_ref, meta_ref):
    tr = x.shape[0]

    x_hi, x_lo = _split_hi_lo(x)
    hi = _dot(x_hi, w_ref[...])
    logits = hi[:, :LANES] + (_dot(x_lo, w_ref[:, :LANES]) + hi[:, LANES:]) + b_ref[...]
    lane = lax.broadcasted_iota(jnp.int32, (tr, LANES), 1)
    work = logits
    sel = jnp.zeros((tr, LANES), F32)
    hits, vals = [], []
    lr = lax.broadcasted_iota(jnp.int32, (LANES, LANES), 0)
    lc = lax.broadcasted_iota(jnp.int32, (LANES, LANES), 1)
    prefix = jnp.where(lr <= lc, 1.0, 0.0).astype(BF16)
    for _ in range(TOP_K):
        mx = jnp.max(work, axis=-1, keepdims=True)
        eq = work == mx
        hit = eq & (_dot(jnp.where(eq, 1.0, 0.0).astype(BF16), prefix) == 1.0)
        hits.append(hit)
        vals.append(mx)
        sel = jnp.where(hit, 1.0, sel)
        work = jnp.where(hit, -jnp.inf, work)
    exps = [jnp.exp(v - vals[0]) for v in vals]
    denom = exps[0] + exps[1] + exps[2] + exps[3]

    r = lax.broadcasted_iota(jnp.int32, (tr, tr), 0)
    c = lax.broadcasted_iota(jnp.int32, (tr, tr), 1)
    before = _dot(jnp.where(c < r, 1.0, 0.0).astype(BF16), sel.astype(BF16))
    seg8 = jnp.floor((jnp.sum(sel, axis=0, keepdims=True) + (SEG_ALIGN - 1)) * (1.0 / SEG_ALIGN))
    er = lax.broadcasted_iota(jnp.int32, (LANES, LANES), 0)
    ec = lax.broadcasted_iota(jnp.int32, (LANES, LANES), 1)
    start8 = _dot(jnp.broadcast_to(seg8, (ROW_SUBLANES, LANES)).astype(BF16),
                  jnp.where(er < ec, 1.0, 0.0).astype(BF16))[0:1]
    slot = start8 * SEG_ALIGN + before

    pos_out = jnp.zeros((tr, LANES), F32)
    gate_out = jnp.zeros((tr, LANES), F32)
    for j in range(TOP_K):
        pos = jnp.sum(jnp.where(hits[j], slot, 0.0), axis=-1, keepdims=True)
        pos_out = jnp.where(lane == j, pos, pos_out)
        gate_out = jnp.where(lane == j, exps[j] / denom, gate_out)
    pos_ref[...] = pos_out
    gate_ref[...] = gate_out
    row8 = lax.broadcasted_iota(jnp.int32, (ROW_SUBLANES, LANES), 0)
    meta_ref[...] = jnp.where(row8 == 0, seg8, jnp.where(row8 == 1, start8, 0.0))


def _route_tiles(x_of, w_ref, b_ref, pos_ref, gate_ref, meta_ref):
    for u in range(ROUTE_TILES):
        rows = pl.ds(u * SEG_TILE, SEG_TILE)
        _route_tile(x_of(rows), w_ref, b_ref, pos_ref.at[rows], gate_ref.at[rows],
                    meta_ref.at[pl.ds(u * ROW_SUBLANES, ROW_SUBLANES)])


def _router_kernel(x_ref, w_ref, b_ref, pos_ref, gate_ref, meta_ref):
    _route_tiles(lambda rows: x_ref[rows, :], w_ref, b_ref, pos_ref, gate_ref, meta_ref)


def _router_operands(router_w, router_b):
    d, n_exp = router_w.shape
    w_hi, w_lo = _split_hi_lo(jnp.zeros((d, LANES), F32).at[:, :n_exp].set(router_w))
    return (jnp.concatenate([w_hi, w_lo], axis=1),
            jnp.full((1, LANES), -jnp.inf, F32).at[0, :n_exp].set(router_b))


def _router_specs(d, n_tok):
    const = lambda i: (0, 0)
    tile = lambda i: (i, 0)
    rows = ROUTE_TILES * SEG_TILE
    in_specs = [pl.BlockSpec((d, 2 * LANES), const), pl.BlockSpec((1, LANES), const)]
    out_specs = [pl.BlockSpec((rows, LANES), tile), pl.BlockSpec((rows, LANES), tile),
                 pl.BlockSpec((ROUTE_TILES * ROW_SUBLANES, LANES), tile)]
    out_shape = [
        jax.ShapeDtypeStruct((n_tok, LANES), F32),
        jax.ShapeDtypeStruct((n_tok, LANES), F32),
        jax.ShapeDtypeStruct((n_tok // SEG_TILE * ROW_SUBLANES, LANES), F32),
    ]
    return in_specs, out_specs, out_shape


def _router(x2, router_w, router_b):
    n_tok, d = x2.shape
    in_specs, out_specs, out_shape = _router_specs(d, n_tok)
    return pl.pallas_call(
        _router_kernel,
        grid=(n_tok // (ROUTE_TILES * SEG_TILE),),
        in_specs=[pl.BlockSpec((ROUTE_TILES * SEG_TILE, d), lambda i: (i, 0))] + in_specs,
        out_specs=out_specs,
        out_shape=out_shape,
        compiler_params=_cparams("parallel"),
        name="moe_router",
    )(x2, *_router_operands(router_w, router_b))


def _proj_ln_route_kernel(o_ref, x_ref, w_ref, lg_ref, lb_ref, rw_ref, rb_ref,
                          out_ref, pos_ref, gate_ref, meta_ref, *, dn_alpha):
    def project(rows):
        y = _layer_norm(dn_alpha * x_ref[rows, :] + _dot(o_ref[rows, :], w_ref[...]), lg_ref[...], lb_ref[...])
        out_ref[rows, :] = y
        return y

    _route_tiles(project, rw_ref, rb_ref, pos_ref, gate_ref, meta_ref)


def _proj_residual_ln_route(o2, x2, w_o, ln_g, ln_b, router_w, router_b, dn_alpha):
    n_tok, d = x2.shape
    kdim = o2.shape[1]
    ts = ROUTE_TILES * SEG_TILE
    const = lambda i: (0, 0)
    r_in, r_out, r_shape = _router_specs(d, n_tok)
    res = pl.pallas_call(
        functools.partial(_proj_ln_route_kernel, dn_alpha=dn_alpha),
        grid=(n_tok // ts,),
        in_specs=[
            pl.BlockSpec((ts, kdim), lambda i: (i, 0)),
            pl.BlockSpec((ts, d), lambda i: (i, 0)),
            pl.BlockSpec((kdim, d), const),
            pl.BlockSpec((1, d), const),
            pl.BlockSpec((1, d), const),
        ] + r_in,
        out_specs=[pl.BlockSpec((ts, d), lambda i: (i, 0))] + r_out,
        out_shape=[jax.ShapeDtypeStruct((n_tok, d), F32)] + r_shape,
        compiler_params=_cparams("parallel"),
        name="mla_out_proj_ln_route",
    )(o2, x2, w_o.astype(BF16), ln_g.reshape(1, d), ln_b.reshape(1, d), *_router_operands(router_w, router_b))
    return res[0], tuple(res[1:])


def _segment_copy(local_ref, hbm_ref, sem, to_hbm, lo8, go8, bit):
    rows = SEG_ALIGN << bit
    aligned = lambda v8: v8 * SEG_ALIGN if isinstance(v8, int) else pl.multiple_of(v8 * SEG_ALIGN, SEG_ALIGN)
    loc = local_ref.at[pl.ds(aligned(lo8), rows)]
    hbm = hbm_ref.at[pl.ds(aligned(go8), rows)]
    return pltpu.make_async_copy(loc, hbm, sem) if to_hbm else pltpu.make_async_copy(hbm, loc, sem)


def _segment_starts(tabs, tile, n_exp, local_ref, hbm_ref, sem, to_hbm):
    seg8_ref, start8_ref, gstart8_ref, _ = tabs

    def body(e, c):
        n = seg8_ref[tile * n_exp + e]
        lo = start8_ref[tile * n_exp + e]
        go = gstart8_ref[tile * n_exp + e]

        def bits(lo_bit, hi_bit):
            for bit in range(lo_bit, hi_bit):
                @pl.when(((n >> bit) & 1) == 1)
                def _():
                    off = (n >> (bit + 1)) << (bit + 1)
                    _segment_copy(local_ref, hbm_ref, sem, to_hbm, lo + off, go + off, bit).start()

        bits(0, SEG_LOW_BITS)

        @pl.when(n >= (1 << SEG_LOW_BITS))
        def _():
            bits(SEG_LOW_BITS, SEG_BITS)
        return c

    lax.fori_loop(0, n_exp, body, 0)


def _segment_wait(tabs, tile, local_ref, hbm_ref, sem, to_hbm):
    total = tabs[3][tile]
    for bit in range(SEG_BITS):
        @pl.when(((total >> bit) & 1) == 1)
        def _():
            _segment_copy(local_ref, hbm_ref, sem, to_hbm, 0, 0, bit).wait()


def _dispatch_kernel(seg8_ref, start8_ref, gstart8_ref, tot8_ref, x_ref, pos_ref, xs_hbm, buf, sems, *, n_exp):
    td = x_ref.shape[0]
    seg_rows = buf.shape[1]
    tabs = (seg8_ref, start8_ref, gstart8_ref, tot8_ref)
    step = pl.program_id(0)
    slot = step % 2

    def wait(tile, sl):
        _segment_wait(tabs, tile, buf.at[sl], xs_hbm, sems.at[sl], True)

    @pl.when(step >= 2)
    def _():
        wait(step - 2, slot)

    xb = x_ref[...].astype(BF16)
    pos_t = pos_ref[...].T
    for r0 in range(0, seg_rows, PERM_CHUNK):
        rr = (lax.broadcasted_iota(jnp.int32, (PERM_CHUNK, td), 0) + r0).astype(F32)
        hit = pos_t[0:1] == rr
        for j in range(1, TOP_K):
            hit = hit | (pos_t[j:j + 1] == rr)
        buf[slot, r0:r0 + PERM_CHUNK, :] = _dot(jnp.where(hit, 1.0, 0.0).astype(BF16), xb)

    _segment_starts(tabs, step, n_exp, buf.at[slot], xs_hbm, sems.at[slot], True)

    @pl.when(step == pl.num_programs(0) - 1)
    def _():
        wait(step, slot)

        @pl.when(step >= 1)
        def _():
            wait(step - 1, 1 - slot)


def _seg_rows(td, n_exp):
    return -(-(td * TOP_K + n_exp * (SEG_ALIGN - 1)) // PERM_CHUNK) * PERM_CHUNK


def _dispatch(x2, pos, tabs, n_rows, n_exp):
    n_tok, d = x2.shape
    td = SEG_TILE
    grid_spec = pltpu.PrefetchScalarGridSpec(
        num_scalar_prefetch=len(tabs),
        grid=(n_tok // td,),
        in_specs=[
            pl.BlockSpec((td, d), lambda i, *_: (i, 0)),
            pl.BlockSpec((td, LANES), lambda i, *_: (i, 0)),
        ],
        out_specs=pl.BlockSpec(memory_space=pl.ANY),
        scratch_shapes=[pltpu.VMEM((2, _seg_rows(td, n_exp), d), F32), pltpu.SemaphoreType.DMA((2,))],
    )
    return pl.pallas_call(
        functools.partial(_dispatch_kernel, n_exp=n_exp),
        grid_spec=grid_spec,
        out_shape=jax.ShapeDtypeStruct((n_rows, d), F32),
        compiler_params=_cparams("arbitrary"),
        name="moe_dispatch",
    )(*tabs, x2, pos)


def _expert_kernel(be_ref, nu_ref, nxt_ref, xs_ref, w1_hbm, b1_ref, w2_hbm, b2_ref, perm_ref, ys_ref,
                   w1f_s, w2f_s, w1b_s, w2b_s, sems, *, layer):
    step = pl.program_id(0)
    d, f = w2b_s.shape[1], w2b_s.shape[0]
    active = step < nu_ref[0]
    new_expert = jnp.logical_or(step == 0, be_ref[step] != be_ref[jnp.maximum(step - 1, 0)])

    def fetch(e):
        return (pltpu.make_async_copy(w1_hbm.at[layer, e], w1f_s, sems.at[0]),
                pltpu.make_async_copy(w2_hbm.at[layer, e], w2f_s, sems.at[1]))

    @pl.when(jnp.logical_and(active, step == 0))
    def _():
        for cp in fetch(be_ref[0]):
            cp.start()

    @pl.when(jnp.logical_and(active, new_expert))
    def _():
        for cp in fetch(be_ref[step]):
            cp.wait()
        width = perm_ref.shape[0]
        for r0 in range(0, d, ROW_TILE):
            for c in range(2 * f // width):
                t = _dot(w1f_s[r0:r0 + ROW_TILE, c * width:(c + 1) * width].astype(BF16), perm_ref[...])
                w1b_s[r0:r0 + ROW_TILE, c * (width // 2):(c + 1) * (width // 2)] = t[:, :width // 2].astype(BF16)
                w1b_s[r0:r0 + ROW_TILE, f + c * (width // 2):f + (c + 1) * (width // 2)] = (
                    t[:, width // 2:].astype(BF16))
        w2b_s[...] = w2f_s[...].astype(BF16)

        @pl.when(nxt_ref[step] >= 0)
        def _():
            for cp in fetch(nxt_ref[step]):
                cp.start()

    @pl.when(active)
    def _():
        h = _dot(xs_ref[...].astype(BF16), w1b_s[...]) + b1_ref[0]
        glu = jnp.minimum(h[:, :f], SWIGLU_LIMIT)
        lin = jnp.clip(h[:, f:], -SWIGLU_LIMIT, SWIGLU_LIMIT)
        a = glu * _sigmoid(SWIGLU_ALPHA * glu) * (lin + 1.0)
        ys_ref[...] = _dot(a.astype(BF16), w2b_s[...]) + b2_ref[0]


def _experts(xs, block_e, n_used, next_e, w1_all, w2_all, layer, b1, b2):
    _, n_exp, f, d = w2_all.shape
    blk = MOE_BLOCK
    n_blocks = xs.shape[0] // blk
    width = 2 * LANES
    r = lax.broadcasted_iota(jnp.int32, (width, width), 0)
    c = lax.broadcasted_iota(jnp.int32, (width, width), 1)
    perm = (c == (r % 2) * (width // 2) + r // 2).astype(BF16)
    row_map = lambda i, be, nu, nx: (jnp.minimum(i, nu[0] - 1), 0)
    exp_map = lambda i, be, nu, nx: (be[i], 0, 0)
    grid_spec = pltpu.PrefetchScalarGridSpec(
        num_scalar_prefetch=3,
        grid=(n_blocks,),
        in_specs=[
            pl.BlockSpec((blk, d), row_map),
            pl.BlockSpec(memory_space=pl.ANY),
            pl.BlockSpec((1, 1, 2 * f), exp_map),
            pl.BlockSpec(memory_space=pl.ANY),
            pl.BlockSpec((1, 1, d), exp_map),
            pl.BlockSpec((width, width), lambda i, be, nu, nx: (0, 0)),
        ],
        out_specs=pl.BlockSpec((blk, d), row_map),
        scratch_shapes=[pltpu.VMEM((d, 2 * f), F32), pltpu.VMEM((f, d), F32),
                        pltpu.VMEM((d, 2 * f), BF16), pltpu.VMEM((f, d), BF16),
                        pltpu.SemaphoreType.DMA((2,))],
    )
    return pl.pallas_call(
        functools.partial(_expert_kernel, layer=layer),
        grid_spec=grid_spec,
        out_shape=jax.ShapeDtypeStruct(xs.shape, F32),
        compiler_params=_cparams("arbitrary"),
        name="moe_experts",
    )(block_e, n_used, next_e, xs, w1_all, b1.reshape(n_exp, 1, 2 * f), w2_all, b2.reshape(n_exp, 1, d), perm)


def _combine_kernel(seg8_ref, start8_ref, gstart8_ref, tot8_ref, ys_hbm, x_ref, pos_ref, gate_ref, lg_ref,
                    lb_ref, out_ref, buf, sems, *, n_exp, dn_alpha):
    td = x_ref.shape[0]
    seg_rows = buf.shape[1]
    tabs = (seg8_ref, start8_ref, gstart8_ref, tot8_ref)
    step = pl.program_id(0)
    slot = step % 2

    def start(tile, sl):
        _segment_starts(tabs, tile, n_exp, buf.at[sl], ys_hbm, sems.at[sl], False)

    @pl.when(step == 0)
    def _():
        buf[...] = jnp.zeros_like(buf)
        start(step, slot)

    @pl.when(step + 1 < pl.num_programs(0))
    def _():
        start(step + 1, 1 - slot)

    _segment_wait(tabs, step, buf.at[slot], ys_hbm, sems.at[slot], False)

    pos = pos_ref[...]
    gates = gate_ref[...]
    acc = dn_alpha * x_ref[...]
    for r0 in range(0, seg_rows, PERM_CHUNK):
        cc = (lax.broadcasted_iota(jnp.int32, (td, PERM_CHUNK), 1) + r0).astype(F32)
        g = jnp.zeros((td, PERM_CHUNK), F32)
        for j in range(TOP_K):
            g = jnp.where(pos[:, j:j + 1] == cc, gates[:, j:j + 1], g)
        acc = acc + _dot(g.astype(BF16), buf[slot, r0:r0 + PERM_CHUNK, :].astype(BF16))
    out_ref[...] = _layer_norm(acc, lg_ref[...], lb_ref[...])


def _combine_ln(ys, pos, gates, tabs, x2, ln_g, ln_b, n_exp, dn_alpha):
    n_tok, d = x2.shape
    td = SEG_TILE
    const = lambda i, *_: (0, 0)
    tile = lambda i, *_: (i, 0)
    grid_spec = pltpu.PrefetchScalarGridSpec(
        num_scalar_prefetch=len(tabs),
        grid=(n_tok // td,),
        in_specs=[
            pl.BlockSpec(memory_space=pl.ANY),
            pl.BlockSpec((td, d), tile),
            pl.BlockSpec((td, LANES), tile),
            pl.BlockSpec((td, LANES), tile),
            pl.BlockSpec((1, d), const),
            pl.BlockSpec((1, d), const),
        ],
        out_specs=pl.BlockSpec((td, d), tile),
        scratch_shapes=[pltpu.VMEM((2, _seg_rows(td, n_exp), d), F32), pltpu.SemaphoreType.DMA((2,))],
    )
    return pl.pallas_call(
        functools.partial(_combine_kernel, n_exp=n_exp, dn_alpha=dn_alpha),
        grid_spec=grid_spec,
        out_shape=jax.ShapeDtypeStruct((n_tok, d), F32),
        compiler_params=_cparams("arbitrary"),
        name="moe_combine_ln",
    )(*tabs, ys, x2, pos, gates, ln_g.reshape(1, d), ln_b.reshape(1, d))


def _moe_layer(x2, router_w, router_b, w1_all, w2_all, layer, b1, b2, ln_g, ln_b, dn_alpha, routing=None):
    n_tok, d = x2.shape
    n_exp = w2_all.shape[1]
    n_tiles = n_tok // SEG_TILE
    blk8 = MOE_BLOCK // SEG_ALIGN
    n_blocks = -(-(n_tok * TOP_K + n_tiles * n_exp * (SEG_ALIGN - 1)) // MOE_BLOCK) + n_exp
    n_rows = n_blocks * MOE_BLOCK

    pos, gates, meta = _router(x2, router_w, router_b) if routing is None else routing

    meta = meta.reshape(n_tiles, ROW_SUBLANES, LANES)
    seg8 = meta[:, 0, :n_exp].astype(jnp.int32)
    start8 = meta[:, 1, :n_exp].astype(jnp.int32)
    padded8 = (jnp.sum(seg8, axis=0) + blk8 - 1) // blk8 * blk8
    pends8 = jnp.cumsum(padded8)
    gstart8 = (pends8 - padded8)[None, :] + jnp.cumsum(seg8, axis=0) - seg8
    block_start8 = jnp.arange(n_blocks, dtype=jnp.int32) * blk8
    block_e = jnp.minimum(jnp.sum((pends8[None, :] <= block_start8[:, None]).astype(jnp.int32), axis=1),
                          n_exp - 1).astype(jnp.int32)
    n_used = (pends8[-1:] // blk8).astype(jnp.int32)
    ids = jnp.arange(n_exp, dtype=jnp.int32)
    later = jnp.where((ids[None, :] > ids[:, None]) & (padded8[None, :] > 0), ids[None, :], n_exp)
    next_present = jnp.min(later, axis=1)
    next_e = jnp.where(next_present < n_exp, next_present, -1)[block_e].astype(jnp.int32)
    tabs = (seg8.reshape(-1), start8.reshape(-1), gstart8.reshape(-1).astype(jnp.int32),
            jnp.sum(seg8, axis=1).astype(jnp.int32))

    b1_l = jnp.concatenate([b1[:, 0::2], b1[:, 1::2]], axis=-1)

    xs = _dispatch(x2, pos, tabs, n_rows, n_exp)
    ys = _experts(xs, block_e, n_used, next_e, w1_all, w2_all, layer, b1_l, b2)
    return _combine_ln(ys, pos, gates, tabs, x2, ln_g, ln_b, n_exp, dn_alpha)


def _rope_lane_tables(positions):
    half = ROPE_DIM // 2
    inv_freq = ROPE_THETA ** (-jnp.arange(0, ROPE_DIM, 2, dtype=F32) / ROPE_DIM)
    ang = positions.astype(F32)[..., None] * inv_freq
    cos, sin = jnp.cos(ang), jnp.sin(ang)
    z = jnp.zeros_like(cos)
    assert 4 * half == LANES
    return (jnp.concatenate([cos, z, cos, z], axis=-1),
            jnp.concatenate([-sin, z, sin, z], axis=-1))


def _rope_lane_columns(w_rope):
    half = ROPE_DIM // 2
    z = jnp.zeros(w_rope.shape[:-1] + (half,), w_rope.dtype)
    return jnp.concatenate([w_rope[..., :half], z, w_rope[..., half:], z], axis=-1)


def kernel(x, positions, ln_g, ln_b, hg_w_in, hg_lb, hg_gnorm, hg_w_o, mla_w_dq, mla_q_norm,
           mla_w_uq, mla_w_o, kv_w_a, kv_norm, kv_w_b, router_w, router_b, moe_w1, moe_b1,
           moe_w2, moe_b2):
    bsz, s_len, d = x.shape
    depth = ln_g.shape[0]
    n_a = hg_w_in.shape[0]
    dn_alpha = (2.0 * depth) ** 0.25
    scale = (NOPE_DIM + ROPE_DIM) ** -0.5 * LOG2_E

    lb_soft = jax.nn.softmax(hg_lb.astype(F32), axis=0)
    lower_bounds = jnp.cumsum(lb_soft, axis=0) - lb_soft[0]

    q_rank = mla_w_uq.shape[1]
    mla_heads = mla_w_uq.shape[2] // (NOPE_DIM + ROPE_DIM)
    cc = ss = k_full = v_full = None

    for layer in range(depth):
        routing = None
        if layer < n_a:
            x = _hgrn_layer(x, hg_w_in[layer], lower_bounds[layer], hg_gnorm[layer], hg_w_o[layer],
                            ln_g[layer, 0], ln_b[layer, 0], dn_alpha)
        else:
            j = layer - n_a
            if layer == n_a:
                cc, ss = _rope_lane_tables(positions)
                kv_w_a_l = jnp.concatenate(
                    [kv_w_a[:, :KV_RANK], _rope_lane_columns(kv_w_a[:, KV_RANK:])], axis=-1).astype(BF16)
                k_full, v_full = _shared_kv(x, kv_w_a_l, kv_norm, kv_w_b, cc, ss)
            w_uq = mla_w_uq[j].reshape(q_rank, mla_heads, NOPE_DIM + ROPE_DIM)
            w_uq_l = jnp.concatenate(
                [w_uq[..., :NOPE_DIM], _rope_lane_columns(w_uq[..., NOPE_DIM:])],
                axis=-1).reshape(q_rank, mla_heads * QK_DIM).astype(BF16)
            q_full = _mla_queries(x, mla_w_dq[j], mla_q_norm[j], w_uq_l, cc, ss, scale)
            o = _mla_attention(q_full, k_full, v_full)
            x, routing = _proj_residual_ln_route(
                o.reshape(bsz * s_len, -1), x.reshape(bsz * s_len, d), mla_w_o[j], ln_g[layer, 0],
                ln_b[layer, 0], router_w[layer], router_b[layer], dn_alpha)
        x = _moe_layer(x.reshape(bsz * s_len, d), router_w[layer], router_b[layer], moe_w1, moe_w2, layer,
                       moe_b1[layer], moe_b2[layer], ln_g[layer, 1], ln_b[layer, 1],
                       dn_alpha, routing).reshape(bsz, s_len, d)
    return x
```
